```python
import math
import jax, jax.numpy as jnp
from jax import lax
import numpy as np

D_MODEL = 1024
BATCH = 8
SEQ = 4096
DEPTH = 2

MEM_LEN = 256
EPS = 1e-6
D_MIX = D_MODEL
RET_W = D_MIX // 4
RET_HEADS = 4
RET_HEAD_DIM = RET_W // RET_HEADS
RET_CHUNK = 128
ROPE_BASE = 10000.0
SSD_W = D_MIX // 2
SSD_HEAD_DIM = 64
SSD_HEADS = SSD_W // SSD_HEAD_DIM
SSD_GROUPS = 2
SSD_STATE = 128
SSD_CONV = 4
SSD_CHUNK = 128
SSD_XBC_W = SSD_W + 2 * SSD_GROUPS * SSD_STATE
S5_W = D_MIX - RET_W - SSD_W
S5_GROUP = 16
S5_GROUPS = S5_W // S5_GROUP
S5_STATE = 64
IN_SIZES = (RET_W, RET_W, RET_W, RET_W, SSD_W, SSD_XBC_W, SSD_HEADS, S5_W)
IN_COLS = RET_W * 4 + SSD_W + SSD_XBC_W + SSD_HEADS + S5_W
CROSS_HEADS = 4
CROSS_HEAD_DIM = D_MODEL // CROSS_HEADS
MOE_GROUPS = 4
EXPERTS_PER_GROUP = 8
N_EXPERTS = MOE_GROUPS * EXPERTS_PER_GROUP
TOP_K_INNER = 2
D_EXPERT = D_MODEL // 2
MOE_BLOCK = 128

kernel_name = "hybrid_ret_ssd_s5_hmoe"

F32 = jnp.float32


def rmsnorm(x, g):
    xf = x.astype(F32)
    y = xf * lax.rsqrt(jnp.mean(xf * xf, axis=-1, keepdims=True) + EPS)
    return (y * g.astype(F32)).astype(x.dtype)


def rotary(t):
    L, d = t.shape[1], t.shape[-1]
    inv = ROPE_BASE ** (-jnp.arange(0, d, 2, dtype=F32) / d)
    ang = jnp.arange(L, dtype=F32)[:, None] * inv[None, :]
    cos = jnp.cos(ang)[None, :, None, :]
    sin = jnp.sin(ang)[None, :, None, :]
    t1, t2 = t[..., : d // 2].astype(F32), t[..., d // 2:].astype(F32)
    return jnp.concatenate([t1 * cos - t2 * sin, t1 * sin + t2 * cos], axis=-1)


def retention(q, k, v):
    b, L, H, dh = q.shape
    C = RET_CHUNK
    nc = L // C
    lg = jnp.log1p(-(2.0 ** (-5.0 - jnp.arange(H, dtype=F32))))
    q = rotary(q)
    k = rotary(k) * (dh ** -0.5)

    def chunks(t):
        return t.reshape(b, nc, C, H, dh).transpose(0, 3, 1, 2, 4)

    qc, kc, vc = chunks(q), chunks(k), chunks(v.astype(F32))
    i = jnp.arange(C, dtype=F32)
    rel = i[:, None] - i[None, :]
    decay = jnp.where(rel[None] >= 0, jnp.exp(lg[:, None, None] * jnp.maximum(rel, 0.0)[None]), 0.0)
    s = jnp.einsum('bhcid,bhcjd->bhcij', qc, kc) * decay[None, :, None]
    y_in = jnp.einsum('bhcij,bhcjd->bhcid', s, vc)
    k_dec = kc * jnp.exp(lg[:, None] * (C - 1.0 - i)[None])[None, :, None, :, None]
    kv = jnp.einsum('bhcjd,bhcje->bhcde', k_dec, vc)
    chunk_decay = jnp.exp(lg * C)[None, :, None, None]

    def step(S, kv_c):
        return S * chunk_decay + kv_c, S

    _, S_prev = lax.scan(step, jnp.zeros_like(kv[:, :, 0]), kv.transpose(2, 0, 1, 3, 4))
    S_prev = S_prev.transpose(1, 2, 0, 3, 4)
    q_dec = qc * jnp.exp(lg[:, None] * (i + 1.0)[None])[None, :, None, :, None]
    y_x = jnp.einsum('bhcid,bhcde->bhcie', q_dec, S_prev)
    return (y_in + y_x).transpose(0, 2, 3, 1, 4).reshape(b, L, H, dh)


def ssd_mixer(z, xbc, dt_raw, conv_w, conv_b, dt_bias, A_log, D, norm_g):
    b, L, _ = xbc.shape
    xpad = jnp.pad(xbc, ((0, 0), (SSD_CONV - 1, 0), (0, 0)))
    conv = conv_b
    for j in range(SSD_CONV):
        conv = conv + xpad[:, j:j + L] * conv_w[j]
    xbc = jax.nn.silu(conv)
    xs = xbc[..., :SSD_W]
    Bm = xbc[..., SSD_W:SSD_W + SSD_GROUPS * SSD_STATE]
    Cm = xbc[..., SSD_W + SSD_GROUPS * SSD_STATE:]
    G, R, P, N = SSD_GROUPS, SSD_HEADS // SSD_GROUPS, SSD_HEAD_DIM, SSD_STATE
    C = SSD_CHUNK
    nc = L // C
    x = xs.astype(F32).reshape(b, nc, C, G, R, P)
    Bc = Bm.astype(F32).reshape(b, nc, C, G, N)
    Cc = Cm.astype(F32).reshape(b, nc, C, G, N)
    dt = jax.nn.softplus((dt_raw + dt_bias).astype(F32)).reshape(b, nc, C, G, R)
    A = -jnp.exp(A_log.astype(F32)).reshape(G, R)
    A_cum = jnp.cumsum(dt * A, axis=2)
    Acum_t = A_cum.transpose(0, 1, 3, 4, 2)
    diff = Acum_t[..., :, None] - Acum_t[..., None, :]
    causal = jnp.tril(jnp.ones((C, C), dtype=bool))
    Lmat = jnp.exp(jnp.where(causal, diff, -jnp.inf))
    CB = jnp.einsum('bclgn,bcsgn->bcgls', Cc, Bc)
    xdt = x * dt[..., None]
    y_diag = jnp.einsum('bcgrls,bcsgrp->bclgrp', CB[:, :, :, None] * Lmat, xdt)
    decay_states = jnp.exp(A_cum[:, :, -1:] - A_cum)
    states = jnp.einsum('bclgn,bclgrp->bcgrpn', Bc, xdt * decay_states[..., None])
    chunk_decay = jnp.exp(A_cum[:, :, -1])

    def step(S, inp):
        st, dc = inp
        return S * dc[..., None, None] + st, S

    _, S_prev = lax.scan(step, jnp.zeros((b, G, R, P, N), F32),
                         (states.transpose(1, 0, 2, 3, 4, 5), chunk_decay.transpose(1, 0, 2, 3)))
    S_prev = S_prev.transpose(1, 0, 2, 3, 4, 5)
    y_off = jnp.einsum('bclgn,bcgrpn->bclgrp', Cc, S_prev) * jnp.exp(A_cum)[..., None]
    y = y_diag + y_off + x * D.astype(F32).reshape(G, R)[..., None]
    y = y.reshape(b, L, SSD_W)
    return rmsnorm(y * jax.nn.silu(z.astype(F32)), norm_g)


def _complex_affine_combine(e1, e2):
    a1r, a1i, b1r, b1i = e1
    a2r, a2i, b2r, b2i = e2
    return (a2r * a1r - a2i * a1i,
            a2r * a1i + a2i * a1r,
            a2r * b1r - a2i * b1i + b2r,
            a2r * b1i + a2i * b1r + b2i)


def s5_mixer(u, A_re, A_im, B_re, B_im, C_re, C_im, log_step, D, w_glu, b_glu):
    b, L, _ = u.shape
    ug = u.astype(F32).reshape(b, L, S5_GROUPS, S5_GROUP)
    delta = jnp.exp(log_step.astype(F32))[:, None]
    ar, ai = A_re.astype(F32), A_im.astype(F32)
    mag = jnp.exp(ar * delta)
    ang = ai * delta
    lr, li = mag * jnp.cos(ang), mag * jnp.sin(ang)
    den = ar * ar + ai * ai
    nr, ni = lr - 1.0, li
    cr = (nr * ar + ni * ai) / den
    ci = (ni * ar - nr * ai) / den
    br, bi = B_re.astype(F32), B_im.astype(F32)
    Bbr = cr[..., None] * br - ci[..., None] * bi
    Bbi = cr[..., None] * bi + ci[..., None] * br
    bu_r = jnp.einsum('blgc,gpc->blgp', ug, Bbr)
    bu_i = jnp.einsum('blgc,gpc->blgp', ug, Bbi)
    a_r = jnp.broadcast_to(lr, bu_r.shape)
    a_i = jnp.broadcast_to(li, bu_r.shape)
    _, _, xr, xi = lax.associative_scan(_complex_affine_combine, (a_r, a_i, bu_r, bu_i), axis=1)
    y = (jnp.einsum('gcp,blgp->blgc', C_re.astype(F32), xr)
         - jnp.einsum('gcp,blgp->blgc', C_im.astype(F32), xi))
    y = y.reshape(b, L, S5_W) + D.astype(F32) * u.astype(F32)
    g = jax.nn.gelu(y)
    return g * jax.nn.sigmoid(g @ w_glu.astype(F32) + b_glu.astype(F32))


def hybrid_mixer(hn, w_in, ret_gn_g, ssd_conv_w, ssd_conv_b, ssd_dt_bias, ssd_A_log, ssd_D, ssd_norm_g,
                 s5_A_re, s5_A_im, s5_B_re, s5_B_im, s5_C_re, s5_C_im, s5_log_step, s5_D, s5_w_glu, s5_b_glu,
                 w_out):
    b, L, _ = hn.shape
    proj = hn @ w_in
    splits = [int(s) for s in np.cumsum(IN_SIZES)[:-1]]
    q, k, v, g, z, xbc, dt_raw, u = jnp.split(proj, splits, axis=-1)
    shp = (b, L, RET_HEADS, RET_HEAD_DIM)
    yr = retention(q.reshape(shp), k.reshape(shp), v.reshape(shp))
    yr = yr * lax.rsqrt(jnp.mean(yr * yr, axis=-1, keepdims=True) + EPS)
    out_r = jax.nn.silu(g.astype(F32)) * (yr.reshape(b, L, RET_W) * ret_gn_g.astype(F32))
    out_m = ssd_mixer(z, xbc, dt_raw, ssd_conv_w, ssd_conv_b, ssd_dt_bias, ssd_A_log, ssd_D, ssd_norm_g)
    out_s = s5_mixer(u, s5_A_re, s5_A_im, s5_B_re, s5_B_im, s5_C_re, s5_C_im, s5_log_step, s5_D,
                     s5_w_glu, s5_b_glu)
    cat = jnp.concatenate([out_r, out_m.astype(F32), out_s], axis=-1).astype(hn.dtype)
    return cat @ w_out


def cross_attend(hn, mn, wq, wk, wv, wo):
    b, L, _ = hn.shape
    q = (hn @ wq).reshape(b, L, CROSS_HEADS, CROSS_HEAD_DIM)
    k = (mn @ wk).reshape(b, MEM_LEN, CROSS_HEADS, CROSS_HEAD_DIM)
    v = (mn @ wv).reshape(b, MEM_LEN, CROSS_HEADS, CROSS_HEAD_DIM)
    s = jnp.einsum('blhd,bmhd->bhlm', q, k).astype(F32) * (CROSS_HEAD_DIM ** -0.5)
    p = jax.nn.softmax(s, axis=-1).astype(v.dtype)
    o = jnp.einsum('bhlm,bmhd->blhd', p, v).reshape(b, L, D_MODEL)
    return o @ wo


def hier_moe(xn, w_rg, b_rg, w_re, b_re, w_gate, w_up, w_down):
    T, d = xn.shape
    glog = (xn @ w_rg + b_rg).astype(F32)
    gsel = jnp.argmax(glog, axis=-1)
    pg = jnp.take_along_axis(jax.nn.softmax(glog, axis=-1), gsel[:, None], axis=-1)
    elog = (xn @ w_re + b_re).astype(F32).reshape(T, MOE_GROUPS, EXPERTS_PER_GROUP)
    elog = jnp.take_along_axis(elog, gsel[:, None, None], axis=1)[:, 0]
    top_p, top_i = lax.top_k(jax.nn.softmax(elog, axis=-1), TOP_K_INNER)
    gate = pg * top_p / jnp.sum(top_p, axis=-1, keepdims=True)
    expert = gsel[:, None] * EXPERTS_PER_GROUP + top_i
    flat_e = expert.reshape(-1)
    flat_t = jnp.repeat(jnp.arange(T), TOP_K_INNER)
    flat_w = gate.reshape(-1)
    M = T * TOP_K_INNER
    order = jnp.argsort(flat_e)
    se, st, sw = flat_e[order], flat_t[order], flat_w[order]
    counts = jnp.bincount(flat_e, length=N_EXPERTS)
    starts = jnp.cumsum(counts) - counts
    padded = (counts + MOE_BLOCK - 1) // MOE_BLOCK * MOE_BLOCK
    pends = jnp.cumsum(padded)
    pstarts = pends - padded
    dest = pstarts[se] + jnp.arange(M) - starts[se]
    nb = -(-M // MOE_BLOCK) + N_EXPERTS
    buf = jnp.zeros((nb * MOE_BLOCK, d), xn.dtype).at[dest].set(xn[st])
    block_e = jnp.minimum(jnp.searchsorted(pends, jnp.arange(nb) * MOE_BLOCK, side='right'), N_EXPERTS - 1)

    def expert_block(args):
        xb, e = args
        h = jax.nn.silu(xb @ w_gate[e]) * (xb @ w_up[e])
        return h @ w_down[e]

    ybuf = lax.map(expert_block, (buf.reshape(nb, MOE_BLOCK, d), block_e)).reshape(nb * MOE_BLOCK, d)
    y = ybuf[dest].astype(F32) * sw[:, None]
    return jax.ops.segment_sum(y, st, num_segments=T).astype(xn.dtype)


def setup_inputs(seed: int = 0) -> dict:
    key = jax.random.key(seed)
    ks = iter(jax.random.split(key, 48))

    def nrm(shape, scale):
        return jax.random.normal(next(ks), shape, F32) * scale

    def unif(shape, lo, hi):
        return jax.random.uniform(next(ks), shape, F32, lo, hi)

    def gain(shape):
        return 1.0 + nrm(shape, 0.02)

    x = nrm((BATCH, SEQ, D_MODEL), 1.0)
    mem = nrm((BATCH, MEM_LEN, D_MODEL), 1.0)
    dt0 = jnp.exp(unif((DEPTH, SSD_HEADS), math.log(1e-3), math.log(1e-1)))
    n_idx = jnp.arange(S5_STATE, dtype=F32)
    return {
        'x': x,
        'mem': mem,
        'norm_mix_g': gain((DEPTH, D_MODEL)),
        'w_in': nrm((DEPTH, D_MODEL, IN_COLS), D_MODEL ** -0.5),
        'ret_gn_g': gain((DEPTH, RET_W)),
        'ssd_conv_w': nrm((DEPTH, SSD_CONV, SSD_XBC_W), SSD_CONV ** -0.5),
        'ssd_conv_b': nrm((DEPTH, SSD_XBC_W), 0.02),
        'ssd_dt_bias': dt0 + jnp.log(-jnp.expm1(-dt0)),
        'ssd_A_log': jnp.log(unif((DEPTH, SSD_HEADS), 1.0, 16.0)),
        'ssd_D': gain((DEPTH, SSD_HEADS)),
        'ssd_norm_g': gain((DEPTH, SSD_W)),
        's5_A_re': -0.5 + nrm((DEPTH, S5_GROUPS, S5_STATE), 0.01),
        's5_A_im': math.pi * n_idx + nrm((DEPTH, S5_GROUPS, S5_STATE), 0.01),
        's5_B_re': nrm((DEPTH, S5_GROUPS, S5_STATE, S5_GROUP), (2 * S5_GROUP) ** -0.5),
        's5_B_im': nrm((DEPTH, S5_GROUPS, S5_STATE, S5_GROUP), (2 * S5_GROUP) ** -0.5),
        's5_C_re': nrm((DEPTH, S5_GROUPS, S5_GROUP, S5_STATE), (2 * S5_STATE) ** -0.5),
        's5_C_im': nrm((DEPTH, S5_GROUPS, S5_GROUP, S5_STATE), (2 * S5_STATE) ** -0.5),
        's5_log_step': unif((DEPTH, S5_GROUPS), math.log(1e-3), math.log(1e-1)),
        's5_D': nrm((DEPTH, S5_W), 0.5),
        's5_w_glu': nrm((DEPTH, S5_W, S5_W), S5_W ** -0.5),
        's5_b_glu': nrm((DEPTH, S5_W), 0.02),
        'w_out': nrm((DEPTH, D_MIX, D_MODEL), D_MIX ** -0.5),
        'norm_cross_g': gain((DEPTH, D_MODEL)),
        'mem_norm_g': gain((D_MODEL,)),
        'w_cq': nrm((DEPTH, D_MODEL, D_MODEL), D_MODEL ** -0.5),
        'w_ck': nrm((DEPTH, D_MODEL, D_MODEL), D_MODEL ** -0.5),
        'w_cv': nrm((DEPTH, D_MODEL, D_MODEL), D_MODEL ** -0.5),
        'w_co': nrm((DEPTH, D_MODEL, D_MODEL), D_MODEL ** -0.5),
        'norm_ffn_g': gain((DEPTH, D_MODEL)),
        'w_route_group': nrm((DEPTH, D_MODEL, MOE_GROUPS), D_MODEL ** -0.5),
        'b_route_group': nrm((DEPTH, MOE_GROUPS), 0.01),
        'w_route_expert': nrm((DEPTH, D_MODEL, N_EXPERTS), D_MODEL ** -0.5),
        'b_route_expert': nrm((DEPTH, N_EXPERTS), 0.01),
        'w_gate': nrm((DEPTH, N_EXPERTS, D_MODEL, D_EXPERT), D_MODEL ** -0.5),
        'w_up': nrm((DEPTH, N_EXPERTS, D_MODEL, D_EXPERT), D_MODEL ** -0.5),
        'w_down': nrm((DEPTH, N_EXPERTS, D_EXPERT, D_MODEL), D_EXPERT ** -0.5),
        'norm_final_g': gain((D_MODEL,)),
    }


def reference(x, mem, norm_mix_g, w_in, ret_gn_g, ssd_conv_w, ssd_conv_b, ssd_dt_bias, ssd_A_log, ssd_D,
              ssd_norm_g, s5_A_re, s5_A_im, s5_B_re, s5_B_im, s5_C_re, s5_C_im, s5_log_step, s5_D, s5_w_glu,
              s5_b_glu, w_out, norm_cross_g, mem_norm_g, w_cq, w_ck, w_cv, w_co, norm_ffn_g, w_route_group,
              b_route_group, w_route_expert, b_route_expert, w_gate, w_up, w_down, norm_final_g):
    b, L, d = x.shape
    h = x
    mn = rmsnorm(mem, mem_norm_g)
    for i in range(DEPTH):
        hn = rmsnorm(h, norm_mix_g[i])
        h = h + hybrid_mixer(hn, w_in[i], ret_gn_g[i], ssd_conv_w[i], ssd_conv_b[i], ssd_dt_bias[i],
                             ssd_A_log[i], ssd_D[i], ssd_norm_g[i], s5_A_re[i], s5_A_im[i], s5_B_re[i],
                             s5_B_im[i], s5_C_re[i], s5_C_im[i], s5_log_step[i], s5_D[i], s5_w_glu[i],
                             s5_b_glu[i], w_out[i]).astype(h.dtype)
        hn = rmsnorm(h, norm_cross_g[i])
        h = h + cross_attend(hn, mn, w_cq[i], w_ck[i], w_cv[i], w_co[i]).astype(h.dtype)
        hn = rmsnorm(h, norm_ffn_g[i]).reshape(b * L, d)
        h = h + hier_moe(hn, w_route_group[i], b_route_group[i], w_route_expert[i], b_route_expert[i],
                         w_gate[i], w_up[i], w_down[i]).reshape(b, L, d).astype(h.dtype)
    return rmsnorm(h, norm_final_g)
```

```python
import functools
import math

import jax
import jax.numpy as jnp
from jax import lax
from jax.experimental import pallas as pl
from jax.experimental.pallas import tpu as pltpu

F32 = jnp.float32
BF16 = jnp.bfloat16
HIGHEST = lax.Precision.HIGHEST

EPS = 1e-6
RET_HEADS = 4
RET_HEAD_DIM = 64
RET_W = RET_HEADS * RET_HEAD_DIM
ROPE_BASE = 10000.0
SSD_HEAD_DIM = 64
SSD_HEADS = 8
SSD_GROUPS = 2
SSD_STATE = 128
SSD_CONV = 4
SSD_W = SSD_HEADS * SSD_HEAD_DIM
SSD_XBC_W = SSD_W + 2 * SSD_GROUPS * SSD_STATE
S5_GROUP = 16
S5_GROUPS = 16
S5_STATE = 64
S5_W = S5_GROUP * S5_GROUPS
CROSS_HEADS = 4
MOE_GROUPS = 4
EXPERTS_PER_GROUP = 8
N_EXPERTS = MOE_GROUPS * EXPERTS_PER_GROUP

LANES = 128
ROW_TILE = 512
RET_CHUNK = 256
SSD_CHUNK = 128
S5_CHUNK = 16
MOE_BLOCK = 256
MOE_ROW_TILE = 256
CONV_PAD = 8
VMEM_LIMIT = 48 * 1024 * 1024


def _cparams(*sem):
    return pltpu.CompilerParams(dimension_semantics=sem, vmem_limit_bytes=VMEM_LIMIT)


def _rms(x, g):
    return x * lax.rsqrt(jnp.mean(x * x, axis=-1, keepdims=True) + EPS) * g


def _silu(x):
    return x * jax.nn.sigmoid(x)


def _dot(a, b):
    return jnp.dot(a, b, preferred_element_type=F32)


def _dot_nt(a, b):
    return lax.dot_general(a, b, (((1,), (1,)), ((), ())), preferred_element_type=F32)


def _dot_tn(a, b):
    return lax.dot_general(a, b, (((0,), (0,)), ((), ())), preferred_element_type=F32)


def _norm_matmul_kernel(x_ref, g_ref, w_ref, o_ref):
    xn = _rms(x_ref[...], g_ref[...]).astype(BF16)
    o_ref[...] = _dot(xn, w_ref[...]).astype(o_ref.dtype)


def _norm_matmul(x, g, w, out_dtype, tm, tn):
    m, d = x.shape
    n = w.shape[1]
    return pl.pallas_call(
        _norm_matmul_kernel,
        grid=(m // tm, n // tn),
        in_specs=[pl.BlockSpec((tm, d), lambda i, j: (i, 0)),
                  pl.BlockSpec((1, d), lambda i, j: (0, 0)),
                  pl.BlockSpec((d, tn), lambda i, j: (0, j))],
        out_specs=pl.BlockSpec((tm, tn), lambda i, j: (i, j)),
        out_shape=jax.ShapeDtypeStruct((m, n), out_dtype),
        compiler_params=_cparams("parallel", "parallel"),
        name="norm_matmul",
    )(x, g, w)


_QKVG_W = 4 * RET_W
_IN_SPLITS = (_QKVG_W, SSD_W, SSD_XBC_W, S5_W, LANES)


def _in_proj_kernel(h_ref, g_ref, w_ref, qkvg_ref, z_ref, xbc_ref, u_ref, dt_ref):
    xn = _rms(h_ref[...], g_ref[...]).astype(BF16)
    lo = 0
    for ref, width in zip((qkvg_ref, z_ref, xbc_ref, u_ref, dt_ref), _IN_SPLITS):
        ref[...] = _dot(xn, w_ref[:, lo:lo + width]).astype(ref.dtype)
        lo += width


def _in_proj(h, g, w_pack):
    t, d = h.shape
    tm = ROW_TILE
    n = w_pack.shape[1]
    dts = (BF16, BF16, BF16, BF16, F32)
    return pl.pallas_call(
        _in_proj_kernel,
        grid=(t // tm,),
        in_specs=[pl.BlockSpec((tm, d), lambda i: (i, 0)),
                  pl.BlockSpec((1, d), lambda i: (0, 0)),
                  pl.BlockSpec((d, n), lambda i: (0, 0))],
        out_specs=[pl.BlockSpec((tm, w), lambda i: (i, 0)) for w in _IN_SPLITS],
        out_shape=[jax.ShapeDtypeStruct((t, w), dt) for w, dt in zip(_IN_SPLITS, dts)],
        compiler_params=_cparams("parallel"),
        name="in_proj",
    )(h, g, w_pack)


def _retention_kernel(qkvg_ref, cos_ref, sin_ref, decay_ref, qdec_ref, kdec_ref, cdec_ref, gn_ref,
                      out_ref, s_ref):
    @pl.when(pl.program_id(1) == 0)
    def _():
        s_ref[...] = jnp.zeros_like(s_ref)

    x = qkvg_ref[...]
    w = RET_W
    q = x[:, 0:w].astype(F32)
    k = x[:, w:2 * w].astype(F32)
    v = x[:, 2 * w:3 * w]
    g = x[:, 3 * w:4 * w].astype(F32)
    half = RET_HEAD_DIM // 2
    lane = lax.broadcasted_iota(jnp.int32, q.shape, 1)
    first_half = (lane % RET_HEAD_DIM) < half

    def rot(t):
        swapped = jnp.where(first_half, pltpu.roll(t, w - half, 1), pltpu.roll(t, half, 1))
        return t * cos_ref[...] + swapped * sin_ref[...]

    qr = rot(q)
    kr = rot(k) * (RET_HEAD_DIM ** -0.5)
    qb = qr.astype(BF16)
    kb = kr.astype(BF16)
    qd = (qr * qdec_ref[...]).astype(BF16)
    kd = (kr * kdec_ref[...]).astype(BF16)
    outs = []
    for h in range(RET_HEADS):
        sl = slice(h * RET_HEAD_DIM, (h + 1) * RET_HEAD_DIM)
        s = _dot_nt(qb[:, sl], kb[:, sl]) * decay_ref[h]
        state = s_ref[h]
        y = _dot(s.astype(BF16), v[:, sl]) + _dot(qd[:, sl], state.astype(BF16))
        s_ref[h] = state * cdec_ref[h] + _dot_tn(kd[:, sl], v[:, sl])
        outs.append(y * lax.rsqrt(jnp.mean(y * y, axis=-1, keepdims=True) + EPS))
    yr = jnp.concatenate(outs, axis=-1)
    out_ref[...] = (_silu(g) * (yr * gn_ref[...])).astype(out_ref.dtype)


def _retention_tables(seq):
    c = RET_CHUNK
    dh = RET_HEAD_DIM
    inv = ROPE_BASE ** (-jnp.arange(0, dh, 2, dtype=F32) / dh)
    ang = jnp.arange(seq, dtype=F32)[:, None] * inv[None, :]
    cos, sin = jnp.cos(ang), jnp.sin(ang)
    cos4 = jnp.tile(jnp.concatenate([cos, cos], axis=-1), (1, RET_HEADS))
    sin4 = jnp.tile(jnp.concatenate([-sin, sin], axis=-1), (1, RET_HEADS))
    lg = jnp.log1p(-(2.0 ** (-5.0 - jnp.arange(RET_HEADS, dtype=F32))))
    i = jnp.arange(c, dtype=F32)
    rel = i[:, None] - i[None, :]
    decay = jnp.where(rel[None] >= 0, jnp.exp(lg[:, None, None] * jnp.maximum(rel, 0.0)[None]), 0.0)
    per_head = lambda t: jnp.repeat(t.T, dh, axis=1)
    qdec = per_head(jnp.exp(lg[:, None] * (i + 1.0)[None]))
    kdec = per_head(jnp.exp(lg[:, None] * (c - 1.0 - i)[None]))
    cdec = jnp.broadcast_to(jnp.exp(lg * c)[:, None, None], (RET_HEADS, dh, dh))
    return cos4, sin4, decay, qdec, kdec, cdec


def _retention(qkvg, tables, gn, batch, seq):
    c = RET_CHUNK
    nc = seq // c
    cos4, sin4, decay, qdec, kdec, cdec = tables
    w = RET_W
    full = lambda shape: pl.BlockSpec(shape, lambda b, j: (0,) * len(shape))
    return pl.pallas_call(
        _retention_kernel,
        grid=(batch, nc),
        in_specs=[pl.BlockSpec((c, _QKVG_W), lambda b, j: (b * nc + j, 0)),
                  pl.BlockSpec((c, w), lambda b, j: (j, 0)),
                  pl.BlockSpec((c, w), lambda b, j: (j, 0)),
                  full((RET_HEADS, c, c)), full((c, w)), full((c, w)),
                  full((RET_HEADS, RET_HEAD_DIM, RET_HEAD_DIM)), full((1, w))],
        out_specs=pl.BlockSpec((c, w), lambda b, j: (b * nc + j, 0)),
        out_shape=jax.ShapeDtypeStruct((batch * seq, w), BF16),
        scratch_shapes=[pltpu.VMEM((RET_HEADS, RET_HEAD_DIM, RET_HEAD_DIM), F32)],
        compiler_params=_cparams("parallel", "arbitrary"),
        name="retention",
    )(qkvg, cos4, sin4, decay, qdec, kdec, cdec, gn)


def _ssd_kernel(z_ref, xbc_ref, dt_ref, cw_ref, cb_ref, dtb_ref, a_ref, d_ref, ng_ref,
                out_ref, xs_ref, s_ref):
    c = SSD_CHUNK
    p = SSD_HEAD_DIM
    n = SSD_STATE

    @pl.when(pl.program_id(1) == 0)
    def _():
        xs_ref[0:CONV_PAD, :] = jnp.zeros((CONV_PAD, SSD_XBC_W), F32)
        s_ref[...] = jnp.zeros_like(s_ref)

    xs_ref[CONV_PAD:CONV_PAD + c, :] = xbc_ref[...].astype(F32)
    conv = jnp.broadcast_to(cb_ref[...], (c, SSD_XBC_W))
    for j in range(SSD_CONV):
        off = CONV_PAD - (SSD_CONV - 1) + j
        conv = conv + xs_ref[off:off + c, :] * cw_ref[j:j + 1, :]
    xs_ref[0:CONV_PAD, :] = xs_ref[c:c + CONV_PAD, :]
    act = _silu(conv)
    xs = act[:, :SSD_W]
    bm = act[:, SSD_W:SSD_W + SSD_GROUPS * n]
    cm = act[:, SSD_W + SSD_GROUPS * n:]

    dt_in = dt_ref[...] + dtb_ref[...]
    dt = jnp.maximum(dt_in, 0.0) + jnp.log1p(jnp.exp(-jnp.abs(dt_in)))
    row = lax.broadcasted_iota(jnp.int32, (c, c), 0)
    col = lax.broadcasted_iota(jnp.int32, (c, c), 1)
    causal = row >= col
    a_cum = jnp.dot(causal.astype(F32), dt * a_ref[...], precision=HIGHEST,
                    preferred_element_type=F32)
    a_cum_t = a_cum.T
    a_last = a_cum[c - 1:c, :]
    dec_state = jnp.exp(a_last - a_cum)
    exp_a = jnp.exp(a_cum)
    chunk_dec = jnp.exp(a_last)

    heads_per_group = SSD_HEADS // SSD_GROUPS
    outs = []
    for g in range(SSD_GROUPS):
        bg = bm[:, g * n:(g + 1) * n].astype(BF16)
        cg = cm[:, g * n:(g + 1) * n].astype(BF16)
        cb = _dot_nt(cg, bg)
        for r in range(heads_per_group):
            h = g * heads_per_group + r
            lmat = jnp.exp(jnp.where(causal, a_cum[:, h:h + 1] - a_cum_t[h:h + 1, :], -jnp.inf))
            xh = xs[:, h * p:(h + 1) * p]
            xdt = xh * dt[:, h:h + 1]
            state = s_ref[h]
            y = _dot((cb * lmat).astype(BF16), xdt.astype(BF16))
            y = y + _dot(cg, state.astype(BF16)) * exp_a[:, h:h + 1]
            s_ref[h] = state * chunk_dec[:, h:h + 1] + _dot_tn(
                bg, (xdt * dec_state[:, h:h + 1]).astype(BF16))
            outs.append(y + xh * d_ref[:, h * p:(h + 1) * p])
    y = jnp.concatenate(outs, axis=-1)
    out_ref[...] = _rms(y * _silu(z_ref[...].astype(F32)), ng_ref[...]).astype(out_ref.dtype)


def _ssd(z, xbc, dt, conv_w, conv_b, dt_bias, a_log, d_skip, norm_g, batch, seq):
    c = SSD_CHUNK
    nc = seq // c
    pad = lambda v: jnp.pad(v, (0, LANES - v.shape[0]))[None, :]
    a_neg = pad(-jnp.exp(a_log))
    d_wide = jnp.repeat(d_skip, SSD_HEAD_DIM)[None, :]
    full = lambda shape: pl.BlockSpec(shape, lambda b, j: (0,) * len(shape))
    blk = lambda w: pl.BlockSpec((c, w), lambda b, j: (b * nc + j, 0))
    return pl.pallas_call(
        _ssd_kernel,
        grid=(batch, nc),
        in_specs=[blk(SSD_W), blk(SSD_XBC_W), blk(LANES),
                  full((SSD_CONV, SSD_XBC_W)), full((1, SSD_XBC_W)), full((1, LANES)), full((1, LANES)),
                  full((1, SSD_W)), full((1, SSD_W))],
        out_specs=blk(SSD_W),
        out_shape=jax.ShapeDtypeStruct((batch * seq, SSD_W), BF16),
        scratch_shapes=[pltpu.VMEM((c + CONV_PAD, SSD_XBC_W), F32),
                        pltpu.VMEM((SSD_HEADS, SSD_STATE, SSD_HEAD_DIM), F32)],
        compiler_params=_cparams("parallel", "arbitrary"),
        name="ssd",
    )(z, xbc, dt, conv_w, conv_b[None, :], pad(dt_bias), a_neg, d_wide, norm_g[None, :])


def _s5_kernel(u_ref, t1_ref, pre_ref, pim_ref, qre_ref, qim_ref, are_ref, aim_ref, y_ref,
               ere_ref, eim_ref, xre_ref, xim_ref, *, rows_per_step, n_steps):
    u = u_ref[0]
    ere_ref[...] = _dot(u, pre_ref[0])
    eim_ref[...] = _dot(u, pim_ref[0])
    shape = (rows_per_step, S5_STATE)
    ar = jnp.broadcast_to(are_ref[0], shape)
    ai = jnp.broadcast_to(aim_ref[0], shape)

    def step(i, carry):
        xr, xi = carry
        rows = pl.ds(pl.multiple_of(i * rows_per_step, rows_per_step), rows_per_step)
        xre_ref[rows, :] = xr
        xim_ref[rows, :] = xi
        return (ar * xr - ai * xi + ere_ref[rows, :], ar * xi + ai * xr + eim_ref[rows, :])

    lax.fori_loop(0, n_steps, step, (jnp.zeros(shape, F32), jnp.zeros(shape, F32)))
    y = _dot(u, t1_ref[0])
    y = y + _dot(xre_ref[...].astype(BF16), qre_ref[0]) + _dot(xim_ref[...].astype(BF16), qim_ref[0])
    y_ref[0] = y.astype(y_ref.dtype)


def _s5_operators(a_re, a_im, b_re, b_im, c_re, c_im, log_step):
    cs = S5_CHUNK
    ein = functools.partial(jnp.einsum, precision=HIGHEST)
    delta = jnp.exp(log_step)[:, None]
    ar, ai = a_re, a_im
    mag = jnp.exp(ar * delta)
    ang = ai * delta
    lr, li = mag * jnp.cos(ang), mag * jnp.sin(ang)
    den = ar * ar + ai * ai
    nr, ni = lr - 1.0, li
    cr = (nr * ar + ni * ai) / den
    ci = (ni * ar - nr * ai) / den
    bbr = cr[..., None] * b_re - ci[..., None] * b_im
    bbi = cr[..., None] * b_im + ci[..., None] * b_re
    k = jnp.arange(cs + 1, dtype=F32)
    pmag = jnp.exp((ar * delta)[..., None] * k)
    pang = ang[..., None] * k
    pr, pi = pmag * jnp.cos(pang), pmag * jnp.sin(pang)
    clr = c_re[..., None] * pr[:, None] - c_im[..., None] * pi[:, None]
    cli = c_re[..., None] * pi[:, None] + c_im[..., None] * pr[:, None]
    kern = ein('gcpk,gpd->gkcd', clr, bbr) - ein('gcpk,gpd->gkcd', cli, bbi)
    kern = jnp.concatenate([kern[:, :cs], jnp.zeros_like(kern[:, :1])], axis=1)
    s = jnp.arange(cs)
    lag = jnp.where(s[None, :] >= s[:, None], s[None, :] - s[:, None], cs)
    t1 = kern[:, lag]
    t1 = t1.transpose(0, 1, 4, 2, 3).reshape(S5_GROUPS, cs * S5_GROUP, cs * S5_GROUP)
    rev = cs - 1 - s
    prr, pri = pr[..., rev], pi[..., rev]
    p_re = prr[..., None] * bbr[:, :, None] - pri[..., None] * bbi[:, :, None]
    p_im = prr[..., None] * bbi[:, :, None] + pri[..., None] * bbr[:, :, None]
    flat_p = lambda t: t.transpose(0, 2, 3, 1).reshape(S5_GROUPS, cs * S5_GROUP, S5_STATE)
    q_re = clr[..., 1:]
    q_im = -cli[..., 1:]
    flat_q = lambda t: t.transpose(0, 2, 3, 1).reshape(S5_GROUPS, S5_STATE, cs * S5_GROUP)
    a_chunk_re = pr[..., cs][:, None, :]
    a_chunk_im = pi[..., cs][:, None, :]
    return (t1.astype(BF16), flat_p(p_re).astype(BF16), flat_p(p_im).astype(BF16),
            flat_q(q_re).astype(BF16), flat_q(q_im).astype(BF16), a_chunk_re, a_chunk_im)


def _s5(u, ops, batch, seq):
    cs = S5_CHUNK
    n_steps = seq // cs
    rows = n_steps * batch
    flat = cs * S5_GROUP
    u4 = u.reshape(batch, n_steps, cs, S5_GROUPS, S5_GROUP).transpose(3, 1, 0, 2, 4)
    u4 = u4.reshape(S5_GROUPS, rows, flat)
    t1, p_re, p_im, q_re, q_im, a_re, a_im = ops
    per_g = lambda a, b: pl.BlockSpec((1, a, b), lambda g: (g, 0, 0))
    y4 = pl.pallas_call(
        functools.partial(_s5_kernel, rows_per_step=batch, n_steps=n_steps),
        grid=(S5_GROUPS,),
        in_specs=[per_g(rows, flat), per_g(flat, flat), per_g(flat, S5_STATE), per_g(flat, S5_STATE),
                  per_g(S5_STATE, flat), per_g(S5_STATE, flat), per_g(1, S5_STATE), per_g(1, S5_STATE)],
        out_specs=per_g(rows, flat),
        out_shape=jax.ShapeDtypeStruct((S5_GROUPS, rows, flat), BF16),
        scratch_shapes=[pltpu.VMEM((rows, S5_STATE), F32)] * 4,
        compiler_params=_cparams("parallel"),
        name="s5",
    )(u4, t1, p_re, p_im, q_re, q_im, a_re, a_im)
    y = y4.reshape(S5_GROUPS, n_steps, batch, cs, S5_GROUP).transpose(2, 1, 3, 0, 4)
    return y.reshape(batch * seq, S5_W)


def _out_proj_kernel(h_ref, r_ref, m_ref, ys_ref, u_ref, d_ref, wg_ref, bg_ref, wo_ref, o_ref):
    y = ys_ref[...].astype(F32) + d_ref[...] * u_ref[...].astype(F32)
    g = jax.nn.gelu(y)
    s = g * jax.nn.sigmoid(_dot(g.astype(BF16), wg_ref[...]) + bg_ref[...])
    acc = _dot(r_ref[...], wo_ref[0:RET_W, :])
    acc = acc + _dot(m_ref[...], wo_ref[RET_W:RET_W + SSD_W, :])
    acc = acc + _dot(s.astype(BF16), wo_ref[RET_W + SSD_W:, :])
    o_ref[...] = h_ref[...] + acc


def _out_proj(h, out_r, out_m, y_s, u, d_s5, w_glu, b_glu, w_out):
    t, d = h.shape
    tm = ROW_TILE
    row = lambda w: pl.BlockSpec((tm, w), lambda i: (i, 0))
    full = lambda a, b: pl.BlockSpec((a, b), lambda i: (0, 0))
    return pl.pallas_call(
        _out_proj_kernel,
        grid=(t // tm,),
        in_specs=[row(d), row(RET_W), row(SSD_W), row(S5_W), row(S5_W),
                  full(1, S5_W), full(S5_W, S5_W), full(1, S5_W), full(d, d)],
        out_specs=row(d),
        out_shape=jax.ShapeDtypeStruct((t, d), F32),
        input_output_aliases={0: 0},
        compiler_params=_cparams("parallel"),
        name="out_proj",
    )(h, out_r, out_m, y_s, u, d_s5, w_glu, b_glu, w_out)


def _cross_kernel(h_ref, g_ref, wq_ref, k_ref, v_ref, wo_ref, o_ref):
    h = h_ref[...]
    d = h.shape[-1]
    dh = d // CROSS_HEADS
    q = _dot(_rms(h, g_ref[...]).astype(BF16), wq_ref[...]).astype(BF16)
    outs = []
    for i in range(CROSS_HEADS):
        sl = slice(i * dh, (i + 1) * dh)
        s = _dot_nt(q[:, sl], k_ref[:, sl]) * (dh ** -0.5)
        p = jnp.exp(s - jnp.max(s, axis=-1, keepdims=True))
        o = _dot(p.astype(BF16), v_ref[:, sl])
        outs.append(o / jnp.sum(p, axis=-1, keepdims=True))
    o = jnp.concatenate(outs, axis=-1).astype(BF16)
    o_ref[...] = h + _dot(o, wo_ref[...])


def _cross(h, g, wq, kv, layer, wo, seq, mem_len):
    t, d = h.shape
    tm = ROW_TILE
    tiles_per_seq = seq // tm
    full = lambda a, b: pl.BlockSpec((a, b), lambda i: (0, 0))
    return pl.pallas_call(
        _cross_kernel,
        grid=(t // tm,),
        in_specs=[pl.BlockSpec((tm, d), lambda i: (i, 0)), full(1, d), full(d, d),
                  pl.BlockSpec((mem_len, d), lambda i: (i // tiles_per_seq, 2 * layer)),
                  pl.BlockSpec((mem_len, d), lambda i: (i // tiles_per_seq, 2 * layer + 1)),
                  full(d, d)],
        out_specs=pl.BlockSpec((tm, d), lambda i: (i, 0)),
        out_shape=jax.ShapeDtypeStruct((t, d), F32),
        input_output_aliases={0: 0},
        compiler_params=_cparams("parallel"),
        name="cross_attn",
    )(h, g, wq, kv, kv, wo)


_GROUP_LANE0 = N_EXPERTS


def _router_kernel(h_ref, g_ref, w_ref, b_ref, info_ref, cnt_ref, carry_ref):
    @pl.when(pl.program_id(0) == 0)
    def _():
        carry_ref[...] = jnp.zeros_like(carry_ref)

    xn = _rms(h_ref[...], g_ref[...])
    logits = jnp.dot(xn, w_ref[...], precision=HIGHEST, preferred_element_type=F32) + b_ref[...]
    tm = logits.shape[0]
    lane = lax.broadcasted_iota(jnp.int32, logits.shape, 1).astype(F32)
    neg = -jnp.inf

    def first_argmax(vals):
        m = jnp.max(vals, axis=-1, keepdims=True)
        return m, jnp.min(jnp.where(vals == m, lane, float(LANES)), axis=-1, keepdims=True)

    gl = jnp.where((lane >= _GROUP_LANE0) & (lane < _GROUP_LANE0 + MOE_GROUPS), logits, neg)
    gmax, glane = first_argmax(gl)
    pg = 1.0 / jnp.sum(jnp.exp(gl - gmax), axis=-1, keepdims=True)
    lo = (glane - _GROUP_LANE0) * EXPERTS_PER_GROUP
    el = jnp.where((lane >= lo) & (lane < lo + EXPERTS_PER_GROUP), logits, neg)
    m1, e1 = first_argmax(el)
    m2, e2 = first_argmax(jnp.where(lane == e1, neg, el))
    p2 = jnp.exp(m2 - m1)
    gate1 = pg / (1.0 + p2)
    gate2 = pg * p2 / (1.0 + p2)

    hot = jnp.where((lane == e1) | (lane == e2), 1.0, 0.0)
    row = lax.broadcasted_iota(jnp.int32, (tm, tm), 0)
    col = lax.broadcasted_iota(jnp.int32, (tm, tm), 1)
    before = jnp.where(row > col, 1.0, 0.0).astype(BF16)
    cum = _dot(before, hot.astype(BF16)) + carry_ref[...]
    rank1 = jnp.sum(jnp.where(lane == e1, cum, 0.0), axis=-1, keepdims=True)
    rank2 = jnp.sum(jnp.where(lane == e2, cum, 0.0), axis=-1, keepdims=True)
    carry_ref[...] = carry_ref[...] + jnp.sum(hot, axis=0, keepdims=True)
    cnt_ref[...] = carry_ref[...]

    info = jnp.zeros(logits.shape, F32)
    for i, val in enumerate((e1.astype(F32), e2.astype(F32), rank1, rank2, gate1, gate2)):
        info = jnp.where(lane == i, val, info)
    info_ref[...] = info


def _router(h, g, w_r, b_r):
    t, d = h.shape
    tm = ROW_TILE
    return pl.pallas_call(
        _router_kernel,
        grid=(t // tm,),
        in_specs=[pl.BlockSpec((tm, d), lambda i: (i, 0)),
                  pl.BlockSpec((1, d), lambda i: (0, 0)),
                  pl.BlockSpec((d, LANES), lambda i: (0, 0)),
                  pl.BlockSpec((1, LANES), lambda i: (0, 0))],
        out_specs=[pl.BlockSpec((tm, LANES), lambda i: (i, 0)),
                   pl.BlockSpec((1, LANES), lambda i: (0, 0))],
        out_shape=[jax.ShapeDtypeStruct((t, LANES), F32), jax.ShapeDtypeStruct((1, LANES), F32)],
        scratch_shapes=[pltpu.VMEM((1, LANES), F32)],
        compiler_params=_cparams("arbitrary"),
        name="moe_router",
    )(h, g, w_r, b_r)


def _row_copy(src_hbm, dst_vmem, src_row, dst_row, sem):
    return pltpu.make_async_copy(src_hbm.at[pl.ds(src_row, 1), :], dst_vmem.at[pl.ds(dst_row, 1), :], sem)


def _expert_kernel(be_ref, nu_ref, src_ref, h_hbm, g_ref, wg_ref, wu_ref, wd_ref, y_ref, xbuf, sem):
    del be_ref
    used = pl.program_id(0) < nu_ref[0]

    @pl.when(used)
    def _():
        def start(r, carry):
            _row_copy(h_hbm, xbuf, src_ref[0, 0, r], r, sem).start()
            return carry

        def wait(r, carry):
            _row_copy(h_hbm, xbuf, src_ref[0, 0, r], r, sem).wait()
            return carry

        lax.fori_loop(0, MOE_BLOCK, start, 0)
        lax.fori_loop(0, MOE_BLOCK, wait, 0)
        xn = _rms(xbuf[...], g_ref[...]).astype(BF16)
        hid = (_silu(_dot(xn, wg_ref[0])) * _dot(xn, wu_ref[0])).astype(BF16)
        y_ref[...] = _dot(hid, wd_ref[0])

    @pl.when(jnp.logical_not(used))
    def _():
        y_ref[...] = jnp.zeros_like(y_ref)


def _experts(h, g, block_e, n_used, src_tok, w_gate, w_up, w_down):
    t, d = h.shape
    nb = block_e.shape[0]
    de = w_gate.shape[-1]
    grid_spec = pltpu.PrefetchScalarGridSpec(
        num_scalar_prefetch=2,
        grid=(nb,),
        in_specs=[pl.BlockSpec((1, 1, MOE_BLOCK), lambda i, be, nu: (i, 0, 0), memory_space=pltpu.SMEM),
                  pl.BlockSpec(memory_space=pl.ANY),
                  pl.BlockSpec((1, d), lambda i, be, nu: (0, 0)),
                  pl.BlockSpec((1, d, de), lambda i, be, nu: (be[i], 0, 0)),
                  pl.BlockSpec((1, d, de), lambda i, be, nu: (be[i], 0, 0)),
                  pl.BlockSpec((1, de, d), lambda i, be, nu: (be[i], 0, 0))],
        out_specs=pl.BlockSpec((MOE_BLOCK, d), lambda i, be, nu: (i, 0)),
        scratch_shapes=[pltpu.VMEM((MOE_BLOCK, d), F32), pltpu.SemaphoreType.DMA(())],
    )
    return pl.pallas_call(
        _expert_kernel,
        grid_spec=grid_spec,
        out_shape=jax.ShapeDtypeStruct((nb * MOE_BLOCK, d), F32),
        compiler_params=_cparams("arbitrary"),
        name="moe_experts",
    )(block_e, n_used, src_tok.reshape(nb, 1, MOE_BLOCK), h, g, w_gate, w_up, w_down)


def _combine_kernel(dest_ref, h_ref, info_ref, y_hbm, fg_ref, o_ref, ybuf, sem, *, final_norm):
    tm = h_ref.shape[0]

    def start(r, carry):
        for k in range(2):
            _row_copy(y_hbm, ybuf.at[k], dest_ref[0, 0, 2 * r + k], r, sem.at[k]).start()
        return carry

    def wait(r, carry):
        for k in range(2):
            _row_copy(y_hbm, ybuf.at[k], dest_ref[0, 0, 2 * r + k], r, sem.at[k]).wait()
        return carry

    lax.fori_loop(0, tm, start, 0)
    lax.fori_loop(0, tm, wait, 0)
    info = info_ref[...]
    out = h_ref[...] + (info[:, 4:5] * ybuf[0] + info[:, 5:6] * ybuf[1])
    if final_norm:
        out = _rms(out, fg_ref[...])
    o_ref[...] = out


def _combine(h, info, dest, ybuf, final_g, final_norm):
    t, d = h.shape
    tm = MOE_ROW_TILE
    return pl.pallas_call(
        functools.partial(_combine_kernel, final_norm=final_norm),
        grid=(t // tm,),
        in_specs=[pl.BlockSpec((1, 1, 2 * tm), lambda i: (i, 0, 0), memory_space=pltpu.SMEM),
                  pl.BlockSpec((tm, d), lambda i: (i, 0)),
                  pl.BlockSpec((tm, LANES), lambda i: (i, 0)),
                  pl.BlockSpec(memory_space=pl.ANY),
                  pl.BlockSpec((1, d), lambda i: (0, 0))],
        out_specs=pl.BlockSpec((tm, d), lambda i: (i, 0)),
        out_shape=jax.ShapeDtypeStruct((t, d), F32),
        scratch_shapes=[pltpu.VMEM((2, tm, d), F32), pltpu.SemaphoreType.DMA((2,))],
        compiler_params=_cparams("arbitrary"),
        name="moe_combine",
    )(dest.reshape(t // tm, 1, 2 * tm), h, info, ybuf, final_g)


def _moe(h, g, w_rg, b_rg, w_re, b_re, w_gate, w_up, w_down, final_g, final_norm):
    t, d = h.shape
    pad_cols = LANES - N_EXPERTS - MOE_GROUPS
    w_r = jnp.concatenate([w_re, w_rg, jnp.zeros((d, pad_cols), F32)], axis=1)
    b_r = jnp.concatenate([b_re, b_rg, jnp.zeros((pad_cols,), F32)])[None, :]
    info, cnt = _router(h, g, w_r, b_r)
    expert = info[:, 0:2].astype(jnp.int32)
    rank = info[:, 2:4].astype(jnp.int32)
    counts = cnt[0, :N_EXPERTS].astype(jnp.int32)
    padded = (counts + MOE_BLOCK - 1) // MOE_BLOCK * MOE_BLOCK
    pends = jnp.cumsum(padded)
    pstarts = pends - padded
    dest = pstarts[expert] + rank
    nb = (2 * t) // MOE_BLOCK + N_EXPERTS
    block_e = jnp.minimum(jnp.searchsorted(pends, jnp.arange(nb) * MOE_BLOCK, side='right'),
                          N_EXPERTS - 1).astype(jnp.int32)
    n_used = (pends[-1:] // MOE_BLOCK).astype(jnp.int32)
    tok = jnp.broadcast_to(jnp.arange(t, dtype=jnp.int32)[:, None], (t, 2))
    src_tok = jnp.zeros((nb * MOE_BLOCK,), jnp.int32).at[dest.reshape(-1)].set(tok.reshape(-1))
    ybuf = _experts(h, g, block_e, n_used, src_tok, w_gate, w_up, w_down)
    return _combine(h, info, dest, ybuf, final_g, final_norm)


def kernel(x, mem, norm_mix_g, w_in, ret_gn_g, ssd_conv_w, ssd_conv_b, ssd_dt_bias, ssd_A_log, ssd_D,
           ssd_norm_g, s5_A_re, s5_A_im, s5_B_re, s5_B_im, s5_C_re, s5_C_im, s5_log_step, s5_D, s5_w_glu,
           s5_b_glu, w_out, norm_cross_g, mem_norm_g, w_cq, w_ck, w_cv, w_co, norm_ffn_g, w_route_group,
           b_route_group, w_route_expert, b_route_expert, w_gate, w_up, w_down, norm_final_g):
    batch, seq, d = x.shape
    depth = w_in.shape[0]
    mem_len = mem.shape[1]
    t = batch * seq
    assert d == RET_W * 4 and t % ROW_TILE == 0 and seq % ROW_TILE == 0
    assert seq % RET_CHUNK == 0 and seq % SSD_CHUNK == 0 and seq % S5_CHUNK == 0 and batch % 8 == 0
    row = lambda v: v[None, :]
    h = x.reshape(t, d)

    w_kv = jnp.concatenate([w for i in range(depth) for w in (w_ck[i], w_cv[i])], axis=1).astype(BF16)
    kv = _norm_matmul(mem.reshape(batch * mem_len, d), row(mem_norm_g), w_kv, BF16,
                      tm=mem_len, tn=d)
    ret_tables = _retention_tables(seq)
    c0 = _QKVG_W + SSD_W + SSD_XBC_W
    c1 = c0 + SSD_HEADS

    for i in range(depth):
        w_pack = jnp.concatenate(
            [w_in[i][:, :c0], w_in[i][:, c1:], w_in[i][:, c0:c1], jnp.zeros((d, LANES - SSD_HEADS), F32)],
            axis=1).astype(BF16)
        qkvg, z, xbc, u, dt = _in_proj(h, row(norm_mix_g[i]), w_pack)
        out_r = _retention(qkvg, ret_tables, row(ret_gn_g[i]), batch, seq)
        out_m = _ssd(z, xbc, dt, ssd_conv_w[i], ssd_conv_b[i], ssd_dt_bias[i], ssd_A_log[i], ssd_D[i],
                     ssd_norm_g[i], batch, seq)
        s5_ops = _s5_operators(s5_A_re[i], s5_A_im[i], s5_B_re[i], s5_B_im[i], s5_C_re[i], s5_C_im[i],
                               s5_log_step[i])
        y_s = _s5(u, s5_ops, batch, seq)
        h = _out_proj(h, out_r, out_m, y_s, u, row(s5_D[i]), s5_w_glu[i].astype(BF16), row(s5_b_glu[i]),
                      w_out[i].astype(BF16))
        h = _cross(h, row(norm_cross_g[i]), w_cq[i].astype(BF16), kv, i, w_co[i].astype(BF16), seq, mem_len)
        h = _moe(h, row(norm_ffn_g[i]), w_route_group[i], b_route_group[i], w_route_expert[i],
                 b_route_expert[i], w_gate[i].astype(BF16), w_up[i].astype(BF16), w_down[i].astype(BF16),
                 row(norm_final_g), final_norm=(i == depth - 1))
    return h.reshape(batch, seq, d)
```

```python
import functools
import math

import jax
import jax.numpy as jnp
from jax import lax
from jax.experimental import pallas as pl
from jax.experimental.pallas import tpu as pltpu

F32 = jnp.float32
BF16 = jnp.bfloat16
HIGHEST = lax.Precision.HIGHEST

EPS = 1e-6
RET_HEADS = 4
RET_HEAD_DIM = 64
RET_W = RET_HEADS * RET_HEAD_DIM
ROPE_BASE = 10000.0
SSD_HEAD_DIM = 64
SSD_HEADS = 8
SSD_GROUPS = 2
SSD_STATE = 128
SSD_CONV = 4
SSD_W = SSD_HEADS * SSD_HEAD_DIM
SSD_XBC_W = SSD_W + 2 * SSD_GROUPS * SSD_STATE
S5_GROUP = 16
S5_GROUPS = 16
S5_STATE = 64
S5_W = S5_GROUP * S5_GROUPS
CROSS_HEADS = 4
MOE_GROUPS = 4
EXPERTS_PER_GROUP = 8
N_EXPERTS = MOE_GROUPS * EXPERTS_PER_GROUP

LANES = 128
ROW_TILE = 512
RET_CHUNK = 256
SSD_CHUNK = 128
S5_CHUNK = 8
S5_LANE_GROUPS = LANES // S5_GROUP
S5_HALVES = S5_W // LANES
S5_BLOCK_STEPS = 64
MOE_BLOCK = 256
MOE_ROW_TILE = 256
CONV_PAD = 8
VMEM_LIMIT = 48 * 1024 * 1024


def _cparams(*sem):
    return pltpu.CompilerParams(dimension_semantics=sem, vmem_limit_bytes=VMEM_LIMIT)


def _rms(x, g):
    return x * lax.rsqrt(jnp.mean(x * x, axis=-1, keepdims=True) + EPS) * g


def _silu(x):
    return x * jax.nn.sigmoid(x)


def _dot(a, b):
    return jnp.dot(a, b, preferred_element_type=F32)


def _dot_nt(a, b):
    return lax.dot_general(a, b, (((1,), (1,)), ((), ())), preferred_element_type=F32)


def _dot_tn(a, b):
    return lax.dot_general(a, b, (((0,), (0,)), ((), ())), preferred_element_type=F32)


def _norm_matmul_kernel(x_ref, g_ref, w_ref, o_ref):
    xn = _rms(x_ref[...], g_ref[...]).astype(BF16)
    o_ref[...] = _dot(xn, w_ref[...]).astype(o_ref.dtype)


def _norm_matmul(x, g, w, out_dtype, tm, tn):
    m, d = x.shape
    n = w.shape[1]
    return pl.pallas_call(
        _norm_matmul_kernel,
        grid=(m // tm, n // tn),
        in_specs=[pl.BlockSpec((tm, d), lambda i, j: (i, 0)),
                  pl.BlockSpec((1, d), lambda i, j: (0, 0)),
                  pl.BlockSpec((d, tn), lambda i, j: (0, j))],
        out_specs=pl.BlockSpec((tm, tn), lambda i, j: (i, j)),
        out_shape=jax.ShapeDtypeStruct((m, n), out_dtype),
        compiler_params=_cparams("parallel", "parallel"),
        name="norm_matmul",
    )(x, g, w)


_QKVG_W = 4 * RET_W
_IN_SPLITS = (_QKVG_W, SSD_W, SSD_XBC_W, S5_W, LANES)


def _in_proj_kernel(h_ref, g_ref, w_ref, qkvg_ref, z_ref, xbc_ref, u_ref, dt_ref):
    xn = _rms(h_ref[...], g_ref[...]).astype(BF16)
    lo = 0
    for ref, width in zip((qkvg_ref, z_ref, xbc_ref, u_ref, dt_ref), _IN_SPLITS):
        ref[...] = _dot(xn, w_ref[:, lo:lo + width]).astype(ref.dtype)
        lo += width


def _in_proj(h, g, w_pack):
    t, d = h.shape
    tm = ROW_TILE
    n = w_pack.shape[1]
    dts = (BF16, BF16, BF16, F32, F32)
    return pl.pallas_call(
        _in_proj_kernel,
        grid=(t // tm,),
        in_specs=[pl.BlockSpec((tm, d), lambda i: (i, 0)),
                  pl.BlockSpec((1, d), lambda i: (0, 0)),
                  pl.BlockSpec((d, n), lambda i: (0, 0))],
        out_specs=[pl.BlockSpec((tm, w), lambda i: (i, 0)) for w in _IN_SPLITS],
        out_shape=[jax.ShapeDtypeStruct((t, w), dt) for w, dt in zip(_IN_SPLITS, dts)],
        compiler_params=_cparams("parallel"),
        name="in_proj",
    )(h, g, w_pack)


def _retention_kernel(qkvg_ref, cos_ref, sin_ref, decay_ref, qdec_ref, kdec_ref, cdec_ref, gn_ref,
                      out_ref, s_ref):
    @pl.when(pl.program_id(1) == 0)
    def _():
        s_ref[...] = jnp.zeros_like(s_ref)

    x = qkvg_ref[...]
    w = RET_W
    q = x[:, 0:w].astype(F32)
    k = x[:, w:2 * w].astype(F32)
    v = x[:, 2 * w:3 * w]
    g = x[:, 3 * w:4 * w].astype(F32)
    half = RET_HEAD_DIM // 2
    lane = lax.broadcasted_iota(jnp.int32, q.shape, 1)
    first_half = (lane % RET_HEAD_DIM) < half

    def rot(t):
        swapped = jnp.where(first_half, pltpu.roll(t, w - half, 1), pltpu.roll(t, half, 1))
        return t * cos_ref[...] + swapped * sin_ref[...]

    qr = rot(q)
    kr = rot(k) * (RET_HEAD_DIM ** -0.5)
    qb = qr.astype(BF16)
    kb = kr.astype(BF16)
    qd = (qr * qdec_ref[...]).astype(BF16)
    kd = (kr * kdec_ref[...]).astype(BF16)
    outs = []
    for h in range(RET_HEADS):
        sl = slice(h * RET_HEAD_DIM, (h + 1) * RET_HEAD_DIM)
        s = _dot_nt(qb[:, sl], kb[:, sl]) * decay_ref[h]
        state = s_ref[h]
        y = _dot(s.astype(BF16), v[:, sl]) + _dot(qd[:, sl], state.astype(BF16))
        s_ref[h] = state * cdec_ref[h] + _dot_tn(kd[:, sl], v[:, sl])
        outs.append(y * lax.rsqrt(jnp.mean(y * y, axis=-1, keepdims=True) + EPS))
    yr = jnp.concatenate(outs, axis=-1)
    out_ref[...] = (_silu(g) * (yr * gn_ref[...])).astype(out_ref.dtype)


def _retention_tables(seq):
    c = RET_CHUNK
    dh = RET_HEAD_DIM
    inv = ROPE_BASE ** (-jnp.arange(0, dh, 2, dtype=F32) / dh)
    ang = jnp.arange(seq, dtype=F32)[:, None] * inv[None, :]
    cos, sin = jnp.cos(ang), jnp.sin(ang)
    cos4 = jnp.tile(jnp.concatenate([cos, cos], axis=-1), (1, RET_HEADS))
    sin4 = jnp.tile(jnp.concatenate([-sin, sin], axis=-1), (1, RET_HEADS))
    lg = jnp.log1p(-(2.0 ** (-5.0 - jnp.arange(RET_HEADS, dtype=F32))))
    i = jnp.arange(c, dtype=F32)
    rel = i[:, None] - i[None, :]
    decay = jnp.where(rel[None] >= 0, jnp.exp(lg[:, None, None] * jnp.maximum(rel, 0.0)[None]), 0.0)
    per_head = lambda t: jnp.repeat(t.T, dh, axis=1)
    qdec = per_head(jnp.exp(lg[:, None] * (i + 1.0)[None]))
    kdec = per_head(jnp.exp(lg[:, None] * (c - 1.0 - i)[None]))
    cdec = jnp.broadcast_to(jnp.exp(lg * c)[:, None, None], (RET_HEADS, dh, dh))
    return cos4, sin4, decay, qdec, kdec, cdec


def _retention(qkvg, tables, gn, batch, seq):
    c = RET_CHUNK
    nc = seq // c
    cos4, sin4, decay, qdec, kdec, cdec = tables
    w = RET_W
    full = lambda shape: pl.BlockSpec(shape, lambda b, j: (0,) * len(shape))
    return pl.pallas_call(
        _retention_kernel,
        grid=(batch, nc),
        in_specs=[pl.BlockSpec((c, _QKVG_W), lambda b, j: (b * nc + j, 0)),
                  pl.BlockSpec((c, w), lambda b, j: (j, 0)),
                  pl.BlockSpec((c, w), lambda b, j: (j, 0)),
                  full((RET_HEADS, c, c)), full((c, w)), full((c, w)),
                  full((RET_HEADS, RET_HEAD_DIM, RET_HEAD_DIM)), full((1, w))],
        out_specs=pl.BlockSpec((c, w), lambda b, j: (b * nc + j, 0)),
        out_shape=jax.ShapeDtypeStruct((batch * seq, w), BF16),
        scratch_shapes=[pltpu.VMEM((RET_HEADS, RET_HEAD_DIM, RET_HEAD_DIM), F32)],
        compiler_params=_cparams("parallel", "arbitrary"),
        name="retention",
    )(qkvg, cos4, sin4, decay, qdec, kdec, cdec, gn)


def _ssd_kernel(z_ref, xbc_ref, dt_ref, cw_ref, cb_ref, dtb_ref, a_ref, d_ref, ng_ref,
                out_ref, xs_ref, s_ref):
    c = SSD_CHUNK
    p = SSD_HEAD_DIM
    n = SSD_STATE

    @pl.when(pl.program_id(1) == 0)
    def _():
        xs_ref[0:CONV_PAD, :] = jnp.zeros((CONV_PAD, SSD_XBC_W), F32)
        s_ref[...] = jnp.zeros_like(s_ref)

    xs_ref[CONV_PAD:CONV_PAD + c, :] = xbc_ref[...].astype(F32)
    conv = jnp.broadcast_to(cb_ref[...], (c, SSD_XBC_W))
    for j in range(SSD_CONV):
        off = CONV_PAD - (SSD_CONV - 1) + j
        conv = conv + xs_ref[off:off + c, :] * cw_ref[j:j + 1, :]
    xs_ref[0:CONV_PAD, :] = xs_ref[c:c + CONV_PAD, :]
    act = _silu(conv)
    xs = act[:, :SSD_W]
    bm = act[:, SSD_W:SSD_W + SSD_GROUPS * n]
    cm = act[:, SSD_W + SSD_GROUPS * n:]

    dt_in = dt_ref[...] + dtb_ref[...]
    dt = jnp.maximum(dt_in, 0.0) + jnp.log1p(jnp.exp(-jnp.abs(dt_in)))
    row = lax.broadcasted_iota(jnp.int32, (c, c), 0)
    col = lax.broadcasted_iota(jnp.int32, (c, c), 1)
    causal = row >= col
    a_cum = jnp.dot(causal.astype(F32), dt * a_ref[...], precision=HIGHEST,
                    preferred_element_type=F32)
    a_cum_t = a_cum.T
    a_last = a_cum[c - 1:c, :]
    dec_state = jnp.exp(a_last - a_cum)
    exp_a = jnp.exp(a_cum)
    chunk_dec = jnp.exp(a_last)

    heads_per_group = SSD_HEADS // SSD_GROUPS
    outs = []
    for g in range(SSD_GROUPS):
        bg = bm[:, g * n:(g + 1) * n].astype(BF16)
        cg = cm[:, g * n:(g + 1) * n].astype(BF16)
        cb = _dot_nt(cg, bg)
        for r in range(heads_per_group):
            h = g * heads_per_group + r
            lmat = jnp.exp(jnp.where(causal, a_cum[:, h:h + 1] - a_cum_t[h:h + 1, :], -jnp.inf))
            xh = xs[:, h * p:(h + 1) * p]
            xdt = xh * dt[:, h:h + 1]
            state = s_ref[h]
            y = _dot((cb * lmat).astype(BF16), xdt.astype(BF16))
            y = y + _dot(cg, state.astype(BF16)) * exp_a[:, h:h + 1]
            s_ref[h] = state * chunk_dec[:, h:h + 1] + _dot_tn(
                bg, (xdt * dec_state[:, h:h + 1]).astype(BF16))
            outs.append(y + xh * d_ref[:, h * p:(h + 1) * p])
    y = jnp.concatenate(outs, axis=-1)
    out_ref[...] = _rms(y * _silu(z_ref[...].astype(F32)), ng_ref[...]).astype(out_ref.dtype)


def _ssd(z, xbc, dt, conv_w, conv_b, dt_bias, a_log, d_skip, norm_g, batch, seq):
    c = SSD_CHUNK
    nc = seq // c
    pad = lambda v: jnp.pad(v, (0, LANES - v.shape[0]))[None, :]
    a_neg = pad(-jnp.exp(a_log))
    d_wide = jnp.repeat(d_skip, SSD_HEAD_DIM)[None, :]
    full = lambda shape: pl.BlockSpec(shape, lambda b, j: (0,) * len(shape))
    blk = lambda w: pl.BlockSpec((c, w), lambda b, j: (b * nc + j, 0))
    return pl.pallas_call(
        _ssd_kernel,
        grid=(batch, nc),
        in_specs=[blk(SSD_W), blk(SSD_XBC_W), blk(LANES),
                  full((SSD_CONV, SSD_XBC_W)), full((1, SSD_XBC_W)), full((1, LANES)), full((1, LANES)),
                  full((1, SSD_W)), full((1, SSD_W))],
        out_specs=blk(SSD_W),
        out_shape=jax.ShapeDtypeStruct((batch * seq, SSD_W), BF16),
        scratch_shapes=[pltpu.VMEM((c + CONV_PAD, SSD_XBC_W), F32),
                        pltpu.VMEM((SSD_HEADS, SSD_STATE, SSD_HEAD_DIM), F32)],
        compiler_params=_cparams("parallel", "arbitrary"),
        name="ssd",
    )(z, xbc, dt, conv_w, conv_b[None, :], pad(dt_bias), a_neg, d_wide, norm_g[None, :])


def _s5_kernel(u_ref, t1_ref, pre_ref, pim_ref, qre_ref, qim_ref, are_ref, aim_ref, y_ref,
               ere_ref, eim_ref, xre_ref, xim_ref, sre_ref, sim_ref):
    batch, tb, _ = u_ref.shape
    cs = S5_CHUNK
    ns = tb // cs

    @pl.when(pl.program_id(1) == 0)
    def _():
        sre_ref[...] = jnp.zeros_like(sre_ref)
        sim_ref[...] = jnp.zeros_like(sim_ref)

    u = jnp.concatenate(
        [jnp.concatenate([u_ref[b, pl.ds(s, ns, stride=cs), :] for s in range(cs)], axis=-1)
         for b in range(batch)], axis=0).astype(BF16)
    n_tiles = ere_ref.shape[0]
    lanes_of = lambda j: slice(j * LANES, (j + 1) * LANES)
    e_re = _dot(u, pre_ref[0])
    e_im = _dot(u, pim_ref[0])
    for j in range(n_tiles):
        ere_ref[j] = e_re[:, lanes_of(j)]
        eim_ref[j] = e_im[:, lanes_of(j)]
    shape = (batch, LANES)
    ar = [jnp.broadcast_to(are_ref[0, :, lanes_of(j)], shape) for j in range(n_tiles)]
    ai = [jnp.broadcast_to(aim_ref[0, :, lanes_of(j)], shape) for j in range(n_tiles)]

    def step(n, carry):
        rows = pl.ds(n, batch, stride=ns)
        out = []
        for j in range(n_tiles):
            xr, xi = carry[j]
            xre_ref[j, rows, :] = xr
            xim_ref[j, rows, :] = xi
            out.append((ar[j] * xr - ai[j] * xi + ere_ref[j, rows, :],
                        ar[j] * xi + ai[j] * xr + eim_ref[j, rows, :]))
        return tuple(out)

    init = tuple((sre_ref[j], sim_ref[j]) for j in range(n_tiles))
    final = lax.fori_loop(0, ns, step, init)
    for j in range(n_tiles):
        sre_ref[j], sim_ref[j] = final[j]
    x_re = jnp.concatenate([xre_ref[j] for j in range(n_tiles)], axis=-1).astype(BF16)
    x_im = jnp.concatenate([xim_ref[j] for j in range(n_tiles)], axis=-1).astype(BF16)
    y = _dot(u, t1_ref[0]) + _dot(x_re, qre_ref[0]) + _dot(x_im, qim_ref[0])
    for b in range(batch):
        for s in range(cs):
            y_ref[b, pl.ds(s, ns, stride=cs), :] = y[b * ns:(b + 1) * ns, s * LANES:(s + 1) * LANES]


def _s5_operators(a_re, a_im, b_re, b_im, c_re, c_im, log_step):
    cs = S5_CHUNK
    ein = functools.partial(jnp.einsum, precision=HIGHEST)
    delta = jnp.exp(log_step)[:, None]
    ar, ai = a_re, a_im
    mag = jnp.exp(ar * delta)
    ang = ai * delta
    lr, li = mag * jnp.cos(ang), mag * jnp.sin(ang)
    den = ar * ar + ai * ai
    nr, ni = lr - 1.0, li
    cr = (nr * ar + ni * ai) / den
    ci = (ni * ar - nr * ai) / den
    bbr = cr[..., None] * b_re - ci[..., None] * b_im
    bbi = cr[..., None] * b_im + ci[..., None] * b_re
    k = jnp.arange(cs + 1, dtype=F32)
    pmag = jnp.exp((ar * delta)[..., None] * k)
    pang = ang[..., None] * k
    pr, pi = pmag * jnp.cos(pang), pmag * jnp.sin(pang)
    clr = c_re[..., None] * pr[:, None] - c_im[..., None] * pi[:, None]
    cli = c_re[..., None] * pi[:, None] + c_im[..., None] * pr[:, None]
    kern = ein('gcpk,gpd->gkcd', clr, bbr) - ein('gcpk,gpd->gkcd', cli, bbi)
    kern = jnp.concatenate([kern[:, :cs], jnp.zeros_like(kern[:, :1])], axis=1)
    s = jnp.arange(cs)
    lag = jnp.where(s[None, :] >= s[:, None], s[None, :] - s[:, None], cs)
    nh, ng = S5_HALVES, S5_LANE_GROUPS
    eye = jnp.eye(ng, dtype=F32)
    halves = lambda t: t.reshape((nh, ng) + t.shape[1:])
    t1 = halves(kern[:, lag].transpose(0, 1, 4, 2, 3))
    t1 = t1.transpose(0, 2, 1, 3, 4, 5)[:, :, :, :, :, None, :] * eye[None, None, :, None, None, :, None]
    t1 = t1.reshape(nh, cs * LANES, cs * LANES)
    rev = cs - 1 - s
    prr, pri = pr[..., rev], pi[..., rev]
    p_re = prr[..., None] * bbr[:, :, None] - pri[..., None] * bbi[:, :, None]
    p_im = prr[..., None] * bbi[:, :, None] + pri[..., None] * bbr[:, :, None]

    def flat_p(t):
        t = halves(t.transpose(0, 2, 3, 1)).transpose(0, 2, 1, 3, 4)
        t = t[:, :, :, :, None, :] * eye[None, None, :, None, :, None]
        return t.reshape(nh, cs * LANES, ng * S5_STATE).astype(BF16)

    def flat_q(t):
        t = halves(t.transpose(0, 2, 3, 1))
        t = t[:, :, :, :, None, :] * eye[None, :, None, None, :, None]
        return t.reshape(nh, ng * S5_STATE, cs * LANES).astype(BF16)

    a_chunk_re = pr[..., cs].reshape(nh, 1, ng * S5_STATE)
    a_chunk_im = pi[..., cs].reshape(nh, 1, ng * S5_STATE)
    return (t1.astype(BF16), flat_p(p_re), flat_p(p_im), flat_q(clr[..., 1:]), flat_q(-cli[..., 1:]),
            a_chunk_re, a_chunk_im)


def _s5(u, ops, batch, seq):
    tb = S5_CHUNK * S5_BLOCK_STEPS
    rows = batch * S5_BLOCK_STEPS
    flat = S5_CHUNK * LANES
    nstate = S5_LANE_GROUPS * S5_STATE
    per_h = lambda a, b: pl.BlockSpec((1, a, b), lambda h, j: (h, 0, 0))
    seq_blk = pl.BlockSpec((batch, tb, LANES), lambda h, j: (0, j, h))
    y = pl.pallas_call(
        _s5_kernel,
        grid=(S5_HALVES, seq // tb),
        in_specs=[seq_blk, per_h(flat, flat), per_h(flat, nstate), per_h(flat, nstate),
                  per_h(nstate, flat), per_h(nstate, flat), per_h(1, nstate), per_h(1, nstate)],
        out_specs=seq_blk,
        out_shape=jax.ShapeDtypeStruct((batch, seq, S5_W), F32),
        scratch_shapes=([pltpu.VMEM((nstate // LANES, rows, LANES), F32)] * 4
                        + [pltpu.VMEM((nstate // LANES, batch, LANES), F32)] * 2),
        compiler_params=_cparams("parallel", "arbitrary"),
        name="s5",
    )(u.reshape(batch, seq, S5_W), *ops)
    return y.reshape(batch * seq, S5_W)


def _out_proj_kernel(h_ref, r_ref, m_ref, ys_ref, u_ref, d_ref, wg_ref, bg_ref, wo_ref, o_ref):
    y = ys_ref[...].astype(F32) + d_ref[...] * u_ref[...].astype(F32)
    g = jax.nn.gelu(y)
    s = g * jax.nn.sigmoid(_dot(g.astype(BF16), wg_ref[...]) + bg_ref[...])
    acc = _dot(r_ref[...], wo_ref[0:RET_W, :])
    acc = acc + _dot(m_ref[...], wo_ref[RET_W:RET_W + SSD_W, :])
    acc = acc + _dot(s.astype(BF16), wo_ref[RET_W + SSD_W:, :])
    o_ref[...] = h_ref[...] + acc


def _out_proj(h, out_r, out_m, y_s, u, d_s5, w_glu, b_glu, w_out):
    t, d = h.shape
    tm = ROW_TILE
    row = lambda w: pl.BlockSpec((tm, w), lambda i: (i, 0))
    full = lambda a, b: pl.BlockSpec((a, b), lambda i: (0, 0))
    return pl.pallas_call(
        _out_proj_kernel,
        grid=(t // tm,),
        in_specs=[row(d), row(RET_W), row(SSD_W), row(S5_W), row(S5_W),
                  full(1, S5_W), full(S5_W, S5_W), full(1, S5_W), full(d, d)],
        out_specs=row(d),
        out_shape=jax.ShapeDtypeStruct((t, d), F32),
        input_output_aliases={0: 0},
        compiler_params=_cparams("parallel"),
        name="out_proj",
    )(h, out_r, out_m, y_s, u, d_s5, w_glu, b_glu, w_out)


def _cross_kernel(h_ref, g_ref, wq_ref, k_ref, v_ref, wo_ref, o_ref):
    h = h_ref[...]
    d = h.shape[-1]
    dh = d // CROSS_HEADS
    q = _dot(_rms(h, g_ref[...]).astype(BF16), wq_ref[...]).astype(BF16)
    outs = []
    for i in range(CROSS_HEADS):
        sl = slice(i * dh, (i + 1) * dh)
        s = _dot_nt(q[:, sl], k_ref[:, sl]) * (dh ** -0.5)
        p = jnp.exp(s - jnp.max(s, axis=-1, keepdims=True))
        o = _dot(p.astype(BF16), v_ref[:, sl])
        outs.append(o / jnp.sum(p, axis=-1, keepdims=True))
    o = jnp.concatenate(outs, axis=-1).astype(BF16)
    o_ref[...] = h + _dot(o, wo_ref[...])


def _cross(h, g, wq, kv, layer, wo, seq, mem_len):
    t, d = h.shape
    tm = ROW_TILE
    tiles_per_seq = seq // tm
    full = lambda a, b: pl.BlockSpec((a, b), lambda i: (0, 0))
    return pl.pallas_call(
        _cross_kernel,
        grid=(t // tm,),
        in_specs=[pl.BlockSpec((tm, d), lambda i: (i, 0)), full(1, d), full(d, d),
                  pl.BlockSpec((mem_len, d), lambda i: (i // tiles_per_seq, 2 * layer)),
                  pl.BlockSpec((mem_len, d), lambda i: (i // tiles_per_seq, 2 * layer + 1)),
                  full(d, d)],
        out_specs=pl.BlockSpec((tm, d), lambda i: (i, 0)),
        out_shape=jax.ShapeDtypeStruct((t, d), F32),
        input_output_aliases={0: 0},
        compiler_params=_cparams("parallel"),
        name="cross_attn",
    )(h, g, wq, kv, kv, wo)


_GROUP_LANE0 = N_EXPERTS


def _router_kernel(h_ref, g_ref, w_ref, b_ref, info_ref, infot_ref, cnt_ref, carry_ref):
    @pl.when(pl.program_id(0) == 0)
    def _():
        carry_ref[...] = jnp.zeros_like(carry_ref)

    xn = _rms(h_ref[...], g_ref[...])
    logits = jnp.dot(xn, w_ref[...], precision=HIGHEST, preferred_element_type=F32) + b_ref[...]
    tm = logits.shape[0]
    lane = lax.broadcasted_iota(jnp.int32, logits.shape, 1).astype(F32)
    neg = -jnp.inf

    def first_argmax(vals):
        m = jnp.max(vals, axis=-1, keepdims=True)
        return m, jnp.min(jnp.where(vals == m, lane, float(LANES)), axis=-1, keepdims=True)

    gl = jnp.where((lane >= _GROUP_LANE0) & (lane < _GROUP_LANE0 + MOE_GROUPS), logits, neg)
    gmax, glane = first_argmax(gl)
    pg = 1.0 / jnp.sum(jnp.exp(gl - gmax), axis=-1, keepdims=True)
    lo = (glane - _GROUP_LANE0) * EXPERTS_PER_GROUP
    el = jnp.where((lane >= lo) & (lane < lo + EXPERTS_PER_GROUP), logits, neg)
    m1, e1 = first_argmax(el)
    m2, e2 = first_argmax(jnp.where(lane == e1, neg, el))
    p2 = jnp.exp(m2 - m1)
    gate1 = pg / (1.0 + p2)
    gate2 = pg * p2 / (1.0 + p2)

    hot = jnp.where((lane == e1) | (lane == e2), 1.0, 0.0)
    row = lax.broadcasted_iota(jnp.int32, (tm, tm), 0)
    col = lax.broadcasted_iota(jnp.int32, (tm, tm), 1)
    before = jnp.where(row > col, 1.0, 0.0).astype(BF16)
    cum = _dot(before, hot.astype(BF16)) + carry_ref[...]
    rank1 = jnp.sum(jnp.where(lane == e1, cum, 0.0), axis=-1, keepdims=True)
    rank2 = jnp.sum(jnp.where(lane == e2, cum, 0.0), axis=-1, keepdims=True)
    carry_ref[...] = carry_ref[...] + jnp.sum(hot, axis=0, keepdims=True)
    cnt_ref[...] = carry_ref[...]

    info = jnp.zeros(logits.shape, F32)
    for i, val in enumerate((e1, e2, rank1, rank2, gate1, gate2)):
        info = jnp.where(lane == i, val, info)
    info_ref[...] = info
    infot_ref[...] = info.T[0:_INFO_ROWS, :]


_INFO_ROWS = 8


def _router(h, g, w_r, b_r):
    t, d = h.shape
    tm = ROW_TILE
    return pl.pallas_call(
        _router_kernel,
        grid=(t // tm,),
        in_specs=[pl.BlockSpec((tm, d), lambda i: (i, 0)),
                  pl.BlockSpec((1, d), lambda i: (0, 0)),
                  pl.BlockSpec((d, LANES), lambda i: (0, 0)),
                  pl.BlockSpec((1, LANES), lambda i: (0, 0))],
        out_specs=[pl.BlockSpec((tm, LANES), lambda i: (i, 0)),
                   pl.BlockSpec((_INFO_ROWS, tm), lambda i: (0, i)),
                   pl.BlockSpec((1, LANES), lambda i: (0, 0))],
        out_shape=[jax.ShapeDtypeStruct((t, LANES), F32), jax.ShapeDtypeStruct((_INFO_ROWS, t), F32),
                   jax.ShapeDtypeStruct((1, LANES), F32)],
        scratch_shapes=[pltpu.VMEM((1, LANES), F32)],
        compiler_params=_cparams("arbitrary"),
        name="moe_router",
    )(h, g, w_r, b_r)


def _row_copy(src_hbm, dst_vmem, src_row, dst_row, sem):
    return pltpu.make_async_copy(src_hbm.at[pl.ds(src_row, 1), :], dst_vmem.at[pl.ds(dst_row, 1), :], sem)


_ISSUE_UNROLL = 8


def _start_row_gather(src_hbm, idx_ref, n_rows, dst, sem):
    def body(j, carry):
        for p in range(2):
            r = 2 * j + p
            _row_copy(src_hbm, dst, idx_ref[0, 0, r], r, sem).start(priority=p)
        return carry

    lax.fori_loop(0, n_rows // 2, body, 0, unroll=_ISSUE_UNROLL)


def _wait_row_gather(src_hbm, n_rows, dst, sem):
    pltpu.make_async_copy(src_hbm.at[pl.ds(0, n_rows), :], dst, sem).wait()


def _expert_kernel(be_ref, nu_ref, cur_ref, nxt_ref, h_hbm, g_ref, wg_ref, wu_ref, wd_ref, y_ref,
                   xbuf, wg_s, wu_s, wd_s, sem):
    i = pl.program_id(0)
    n_used = nu_ref[0]
    slot = i % 2
    used = i < n_used

    @pl.when((i == 0) & used)
    def _():
        _start_row_gather(h_hbm, cur_ref, MOE_BLOCK, xbuf.at[0], sem.at[0])

    @pl.when(i + 1 < n_used)
    def _():
        _start_row_gather(h_hbm, nxt_ref, MOE_BLOCK, xbuf.at[1 - slot], sem.at[1 - slot])

    @pl.when(used & ((i == 0) | (be_ref[i] != be_ref[jnp.maximum(i - 1, 0)])))
    def _():
        wg_s[...] = wg_ref[0].astype(BF16)
        wu_s[...] = wu_ref[0].astype(BF16)
        wd_s[...] = wd_ref[0].astype(BF16)

    @pl.when(used)
    def _():
        _wait_row_gather(h_hbm, MOE_BLOCK, xbuf.at[slot], sem.at[slot])
        xn = _rms(xbuf[slot], g_ref[...]).astype(BF16)
        hid = (_silu(_dot(xn, wg_s[...])) * _dot(xn, wu_s[...])).astype(BF16)
        y_ref[...] = _dot(hid, wd_s[...])

    @pl.when(jnp.logical_not(used))
    def _():
        y_ref[...] = jnp.zeros_like(y_ref)


def _experts(h, g, block_e, n_used, src_tok, w_gate, w_up, w_down):
    t, d = h.shape
    nb = block_e.shape[0]
    de = w_gate.shape[-1]
    idx_blk = lambda f: pl.BlockSpec((1, 1, MOE_BLOCK), lambda i, be, nu: (f(i), 0, 0),
                                     memory_space=pltpu.SMEM)
    grid_spec = pltpu.PrefetchScalarGridSpec(
        num_scalar_prefetch=2,
        grid=(nb,),
        in_specs=[idx_blk(lambda i: i), idx_blk(lambda i: jnp.minimum(i + 1, nb - 1)),
                  pl.BlockSpec(memory_space=pl.ANY),
                  pl.BlockSpec((1, d), lambda i, be, nu: (0, 0)),
                  pl.BlockSpec((1, d, de), lambda i, be, nu: (be[i], 0, 0)),
                  pl.BlockSpec((1, d, de), lambda i, be, nu: (be[i], 0, 0)),
                  pl.BlockSpec((1, de, d), lambda i, be, nu: (be[i], 0, 0))],
        out_specs=pl.BlockSpec((MOE_BLOCK, d), lambda i, be, nu: (i, 0)),
        scratch_shapes=[pltpu.VMEM((2, MOE_BLOCK, d), F32), pltpu.VMEM((d, de), BF16),
                        pltpu.VMEM((d, de), BF16), pltpu.VMEM((de, d), BF16),
                        pltpu.SemaphoreType.DMA((2,))],
    )
    src3 = src_tok.reshape(nb, 1, MOE_BLOCK)
    return pl.pallas_call(
        _expert_kernel,
        grid_spec=grid_spec,
        out_shape=jax.ShapeDtypeStruct((nb * MOE_BLOCK, d), F32),
        compiler_params=_cparams("arbitrary"),
        name="moe_experts",
    )(block_e, n_used, src3, src3, h, g, w_gate, w_up, w_down)


def _combine_kernel(dest_ref, h_ref, info_ref, y_hbm, fg_ref, o_ref, ybuf, sem, *, final_norm):
    tm = h_ref.shape[0]

    def start(r, carry):
        for k in range(2):
            _row_copy(y_hbm, ybuf.at[k], dest_ref[0, 0, k * tm + r], r, sem.at[k]).start(priority=k)
        return carry

    lax.fori_loop(0, tm, start, 0, unroll=_ISSUE_UNROLL)
    for k in range(2):
        _wait_row_gather(y_hbm, tm, ybuf.at[k], sem.at[k])
    info = info_ref[...]
    out = h_ref[...] + (info[:, 4:5] * ybuf[0] + info[:, 5:6] * ybuf[1])
    if final_norm:
        out = _rms(out, fg_ref[...])
    o_ref[...] = out


def _combine(h, info, dest, ybuf, final_g, final_norm):
    t, d = h.shape
    tm = MOE_ROW_TILE
    return pl.pallas_call(
        functools.partial(_combine_kernel, final_norm=final_norm),
        grid=(t // tm,),
        in_specs=[pl.BlockSpec((1, 1, 2 * tm), lambda i: (i, 0, 0), memory_space=pltpu.SMEM),
                  pl.BlockSpec((tm, d), lambda i: (i, 0)),
                  pl.BlockSpec((tm, LANES), lambda i: (i, 0)),
                  pl.BlockSpec(memory_space=pl.ANY),
                  pl.BlockSpec((1, d), lambda i: (0, 0))],
        out_specs=pl.BlockSpec((tm, d), lambda i: (i, 0)),
        out_shape=jax.ShapeDtypeStruct((t, d), F32),
        scratch_shapes=[pltpu.VMEM((2, tm, d), F32), pltpu.SemaphoreType.DMA((2,))],
        compiler_params=_cparams("arbitrary"),
        name="moe_combine",
    )(dest.reshape(2, t // tm, tm).transpose(1, 0, 2).reshape(t // tm, 1, 2 * tm), h, info, ybuf, final_g)


def _moe(h, g, w_rg, b_rg, w_re, b_re, w_gate, w_up, w_down, final_g, final_norm):
    t, d = h.shape
    pad_cols = LANES - N_EXPERTS - MOE_GROUPS
    w_r = jnp.concatenate([w_re, w_rg, jnp.zeros((d, pad_cols), F32)], axis=1)
    b_r = jnp.concatenate([b_re, b_rg, jnp.zeros((pad_cols,), F32)])[None, :]
    info, infot, cnt = _router(h, g, w_r, b_r)
    expert = infot[0:2].astype(jnp.int32)
    rank = infot[2:4].astype(jnp.int32)
    counts = cnt[0, :N_EXPERTS].astype(jnp.int32)
    padded = (counts + MOE_BLOCK - 1) // MOE_BLOCK * MOE_BLOCK
    pends = jnp.cumsum(padded)
    pstarts = pends - padded
    starts = jnp.cumsum(counts) - counts
    ids = jnp.arange(N_EXPERTS, dtype=jnp.int32)
    dest = jnp.sum(jnp.where(expert[..., None] == ids, pstarts, 0), axis=-1) + rank
    nb = (2 * t) // MOE_BLOCK + N_EXPERTS
    blk = jnp.arange(nb, dtype=jnp.int32)
    block_e = jnp.minimum(jnp.sum(pends[None, :] <= blk[:, None] * MOE_BLOCK, axis=1), N_EXPERTS - 1)
    block_e = block_e.astype(jnp.int32)
    n_used = (pends[-1:] // MOE_BLOCK).astype(jnp.int32)
    keys = expert * t + jnp.arange(t, dtype=jnp.int32)[None, :]
    sorted_tok = jnp.sort(keys.reshape(-1)) % t
    shift = jnp.sum(jnp.where(block_e[:, None] == ids, pstarts - starts, 0), axis=-1)
    pair = blk[:, None] * MOE_BLOCK + jnp.arange(MOE_BLOCK, dtype=jnp.int32)[None, :] - shift[:, None]
    src_tok = sorted_tok[jnp.clip(pair, 0, 2 * t - 1)]
    ybuf = _experts(h, g, block_e, n_used, src_tok, w_gate, w_up, w_down)
    return _combine(h, info, dest, ybuf, final_g, final_norm)


def kernel(x, mem, norm_mix_g, w_in, ret_gn_g, ssd_conv_w, ssd_conv_b, ssd_dt_bias, ssd_A_log, ssd_D,
           ssd_norm_g, s5_A_re, s5_A_im, s5_B_re, s5_B_im, s5_C_re, s5_C_im, s5_log_step, s5_D, s5_w_glu,
           s5_b_glu, w_out, norm_cross_g, mem_norm_g, w_cq, w_ck, w_cv, w_co, norm_ffn_g, w_route_group,
           b_route_group, w_route_expert, b_route_expert, w_gate, w_up, w_down, norm_final_g):
    batch, seq, d = x.shape
    depth = w_in.shape[0]
    mem_len = mem.shape[1]
    t = batch * seq
    assert d == RET_W * 4 and t % ROW_TILE == 0 and seq % ROW_TILE == 0
    assert seq % RET_CHUNK == 0 and seq % SSD_CHUNK == 0 and seq % (S5_CHUNK * S5_BLOCK_STEPS) == 0
    assert t % MOE_BLOCK == 0 and t % MOE_ROW_TILE == 0
    row = lambda v: v[None, :]
    h = x.reshape(t, d)

    w_kv = jnp.concatenate([w for i in range(depth) for w in (w_ck[i], w_cv[i])], axis=1).astype(BF16)
    kv = _norm_matmul(mem.reshape(batch * mem_len, d), row(mem_norm_g), w_kv, BF16,
                      tm=mem_len, tn=d)
    ret_tables = _retention_tables(seq)
    c0 = _QKVG_W + SSD_W + SSD_XBC_W
    c1 = c0 + SSD_HEADS

    for i in range(depth):
        w_pack = jnp.concatenate(
            [w_in[i][:, :c0], w_in[i][:, c1:], w_in[i][:, c0:c1], jnp.zeros((d, LANES - SSD_HEADS), F32)],
            axis=1).astype(BF16)
        qkvg, z, xbc, u, dt = _in_proj(h, row(norm_mix_g[i]), w_pack)
        out_r = _retention(qkvg, ret_tables, row(ret_gn_g[i]), batch, seq)
        out_m = _ssd(z, xbc, dt, ssd_conv_w[i], ssd_conv_b[i], ssd_dt_bias[i], ssd_A_log[i], ssd_D[i],
                     ssd_norm_g[i], batch, seq)
        s5_ops = _s5_operators(s5_A_re[i], s5_A_im[i], s5_B_re[i], s5_B_im[i], s5_C_re[i], s5_C_im[i],
                               s5_log_step[i])
        y_s = _s5(u, s5_ops, batch, seq)
        h = _out_proj(h, out_r, out_m, y_s, u, row(s5_D[i]), s5_w_glu[i].astype(BF16), row(s5_b_glu[i]),
                      w_out[i].astype(BF16))
        h = _cross(h, row(norm_cross_g[i]), w_cq[i].astype(BF16), kv, i, w_co[i].astype(BF16), seq, mem_len)
        h = _moe(h, row(norm_ffn_g[i]), w_route_group[i], b_route_group[i], w_route_expert[i],
                 b_route_expert[i], w_gate[i], w_up[i], w_down[i], row(norm_final_g),
                 final_norm=(i == depth - 1))
    return h.reshape(batch, seq, d)
```

```python
import functools
import math

import jax
import jax.numpy as jnp
from jax import lax
from jax.experimental import pallas as pl
from jax.experimental.pallas import tpu as pltpu

F32 = jnp.float32
BF16 = jnp.bfloat16
HIGHEST = lax.Precision.HIGHEST

EPS = 1e-6
RET_HEADS = 4
RET_HEAD_DIM = 64
RET_W = RET_HEADS * RET_HEAD_DIM
ROPE_BASE = 10000.0
SSD_HEAD_DIM = 64
SSD_HEADS = 8
SSD_GROUPS = 2
SSD_STATE = 128
SSD_CONV = 4
SSD_W = SSD_HEADS * SSD_HEAD_DIM
SSD_XBC_W = SSD_W + 2 * SSD_GROUPS * SSD_STATE
S5_GROUP = 16
S5_GROUPS = 16
S5_STATE = 64
S5_W = S5_GROUP * S5_GROUPS
CROSS_HEADS = 4
MOE_GROUPS = 4
EXPERTS_PER_GROUP = 8
N_EXPERTS = MOE_GROUPS * EXPERTS_PER_GROUP

LANES = 128
ROW_TILE = 512
RET_CHUNK = 256
SSD_CHUNK = 128
S5_CHUNK = 8
S5_LANE_GROUPS = LANES // S5_GROUP
S5_HALVES = S5_W // LANES
S5_BLOCK_STEPS = 64
MOE_BLOCK = 256
MOE_ROW_TILE = 256
CONV_PAD = 8
VMEM_LIMIT = 48 * 1024 * 1024


def _cparams(*sem):
    return pltpu.CompilerParams(dimension_semantics=sem, vmem_limit_bytes=VMEM_LIMIT)


def _rms(x, g):
    return x * lax.rsqrt(jnp.mean(x * x, axis=-1, keepdims=True) + EPS) * g


def _silu(x):
    return x * jax.nn.sigmoid(x)


def _dot(a, b):
    return jnp.dot(a, b, preferred_element_type=F32)


def _dot_nt(a, b):
    return lax.dot_general(a, b, (((1,), (1,)), ((), ())), preferred_element_type=F32)


def _dot_tn(a, b):
    return lax.dot_general(a, b, (((0,), (0,)), ((), ())), preferred_element_type=F32)


def _load_token_tiles(ref):
    return jnp.concatenate([ref[:, j, :] for j in range(ref.shape[1])], axis=-1)


def _store_token_tiles(ref, x):
    for j in range(ref.shape[1]):
        ref[:, j, :] = x[:, j * LANES:(j + 1) * LANES]


def _norm_matmul_kernel(x_ref, g_ref, w_ref, o_ref):
    xn = _rms(x_ref[...], g_ref[...]).astype(BF16)
    o_ref[...] = _dot(xn, w_ref[...]).astype(o_ref.dtype)


def _norm_matmul(x, g, w, out_dtype, tm, tn):
    m, d = x.shape
    n = w.shape[1]
    return pl.pallas_call(
        _norm_matmul_kernel,
        grid=(m // tm, n // tn),
        in_specs=[pl.BlockSpec((tm, d), lambda i, j: (i, 0)),
                  pl.BlockSpec((1, d), lambda i, j: (0, 0)),
                  pl.BlockSpec((d, tn), lambda i, j: (0, j))],
        out_specs=pl.BlockSpec((tm, tn), lambda i, j: (i, j)),
        out_shape=jax.ShapeDtypeStruct((m, n), out_dtype),
        compiler_params=_cparams("parallel", "parallel"),
        name="norm_matmul",
    )(x, g, w)


_QKVG_W = 4 * RET_W
_IN_SPLITS = (_QKVG_W, SSD_W, SSD_XBC_W, S5_W, LANES)


def _in_proj_kernel(h_ref, g_ref, w_ref, qkvg_ref, z_ref, xbc_ref, u_ref, dt_ref):
    xn = _rms(h_ref[...], g_ref[...]).astype(BF16)
    lo = 0
    for ref, width in zip((qkvg_ref, z_ref, xbc_ref, u_ref, dt_ref), _IN_SPLITS):
        ref[...] = _dot(xn, w_ref[:, lo:lo + width]).astype(ref.dtype)
        lo += width


def _in_proj(h, g, w_pack):
    t, d = h.shape
    tm = ROW_TILE
    n = w_pack.shape[1]
    dts = (BF16, BF16, BF16, F32, F32)
    return pl.pallas_call(
        _in_proj_kernel,
        grid=(t // tm,),
        in_specs=[pl.BlockSpec((tm, d), lambda i: (i, 0)),
                  pl.BlockSpec((1, d), lambda i: (0, 0)),
                  pl.BlockSpec((d, n), lambda i: (0, 0))],
        out_specs=[pl.BlockSpec((tm, w), lambda i: (i, 0)) for w in _IN_SPLITS],
        out_shape=[jax.ShapeDtypeStruct((t, w), dt) for w, dt in zip(_IN_SPLITS, dts)],
        compiler_params=_cparams("parallel"),
        name="in_proj",
    )(h, g, w_pack)


def _retention_kernel(qkvg_ref, cos_ref, sin_ref, decay_ref, qdec_ref, kdec_ref, cdec_ref, gn_ref,
                      out_ref, s_ref):
    @pl.when(pl.program_id(1) == 0)
    def _():
        s_ref[...] = jnp.zeros_like(s_ref)

    x = qkvg_ref[...]
    w = RET_W
    q = x[:, 0:w].astype(F32)
    k = x[:, w:2 * w].astype(F32)
    v = x[:, 2 * w:3 * w]
    g = x[:, 3 * w:4 * w].astype(F32)
    half = RET_HEAD_DIM // 2
    lane = lax.broadcasted_iota(jnp.int32, q.shape, 1)
    first_half = (lane % RET_HEAD_DIM) < half

    def rot(t):
        swapped = jnp.where(first_half, pltpu.roll(t, w - half, 1), pltpu.roll(t, half, 1))
        return t * cos_ref[...] + swapped * sin_ref[...]

    qr = rot(q)
    kr = rot(k) * (RET_HEAD_DIM ** -0.5)
    qb = qr.astype(BF16)
    kb = kr.astype(BF16)
    qd = (qr * qdec_ref[...]).astype(BF16)
    kd = (kr * kdec_ref[...]).astype(BF16)
    outs = []
    for h in range(RET_HEADS):
        sl = slice(h * RET_HEAD_DIM, (h + 1) * RET_HEAD_DIM)
        s = _dot_nt(qb[:, sl], kb[:, sl]) * decay_ref[h]
        state = s_ref[h]
        y = _dot(s.astype(BF16), v[:, sl]) + _dot(qd[:, sl], state.astype(BF16))
        s_ref[h] = state * cdec_ref[h] + _dot_tn(kd[:, sl], v[:, sl])
        outs.append(y * lax.rsqrt(jnp.mean(y * y, axis=-1, keepdims=True) + EPS))
    yr = jnp.concatenate(outs, axis=-1)
    out_ref[...] = (_silu(g) * (yr * gn_ref[...])).astype(out_ref.dtype)


def _retention_tables(seq):
    c = RET_CHUNK
    dh = RET_HEAD_DIM
    inv = ROPE_BASE ** (-jnp.arange(0, dh, 2, dtype=F32) / dh)
    ang = jnp.arange(seq, dtype=F32)[:, None] * inv[None, :]
    cos, sin = jnp.cos(ang), jnp.sin(ang)
    cos4 = jnp.tile(jnp.concatenate([cos, cos], axis=-1), (1, RET_HEADS))
    sin4 = jnp.tile(jnp.concatenate([-sin, sin], axis=-1), (1, RET_HEADS))
    lg = jnp.log1p(-(2.0 ** (-5.0 - jnp.arange(RET_HEADS, dtype=F32))))
    i = jnp.arange(c, dtype=F32)
    rel = i[:, None] - i[None, :]
    decay = jnp.where(rel[None] >= 0, jnp.exp(lg[:, None, None] * jnp.maximum(rel, 0.0)[None]), 0.0)
    per_head = lambda t: jnp.repeat(t.T, dh, axis=1)
    qdec = per_head(jnp.exp(lg[:, None] * (i + 1.0)[None]))
    kdec = per_head(jnp.exp(lg[:, None] * (c - 1.0 - i)[None]))
    cdec = jnp.broadcast_to(jnp.exp(lg * c)[:, None, None], (RET_HEADS, dh, dh))
    return cos4, sin4, decay, qdec, kdec, cdec


def _retention(qkvg, tables, gn, batch, seq):
    c = RET_CHUNK
    nc = seq // c
    cos4, sin4, decay, qdec, kdec, cdec = tables
    w = RET_W
    full = lambda shape: pl.BlockSpec(shape, lambda b, j: (0,) * len(shape))
    return pl.pallas_call(
        _retention_kernel,
        grid=(batch, nc),
        in_specs=[pl.BlockSpec((c, _QKVG_W), lambda b, j: (b * nc + j, 0)),
                  pl.BlockSpec((c, w), lambda b, j: (j, 0)),
                  pl.BlockSpec((c, w), lambda b, j: (j, 0)),
                  full((RET_HEADS, c, c)), full((c, w)), full((c, w)),
                  full((RET_HEADS, RET_HEAD_DIM, RET_HEAD_DIM)), full((1, w))],
        out_specs=pl.BlockSpec((c, w), lambda b, j: (b * nc + j, 0)),
        out_shape=jax.ShapeDtypeStruct((batch * seq, w), BF16),
        scratch_shapes=[pltpu.VMEM((RET_HEADS, RET_HEAD_DIM, RET_HEAD_DIM), F32)],
        compiler_params=_cparams("parallel", "arbitrary"),
        name="retention",
    )(qkvg, cos4, sin4, decay, qdec, kdec, cdec, gn)


def _ssd_kernel(z_ref, xbc_ref, dt_ref, cw_ref, cb_ref, dtb_ref, a_ref, d_ref, ng_ref,
                out_ref, xs_ref, s_ref):
    c = SSD_CHUNK
    p = SSD_HEAD_DIM
    n = SSD_STATE

    @pl.when(pl.program_id(1) == 0)
    def _():
        xs_ref[0:CONV_PAD, :] = jnp.zeros((CONV_PAD, SSD_XBC_W), F32)
        s_ref[...] = jnp.zeros_like(s_ref)

    xs_ref[CONV_PAD:CONV_PAD + c, :] = xbc_ref[...].astype(F32)
    conv = jnp.broadcast_to(cb_ref[...], (c, SSD_XBC_W))
    for j in range(SSD_CONV):
        off = CONV_PAD - (SSD_CONV - 1) + j
        conv = conv + xs_ref[off:off + c, :] * cw_ref[j:j + 1, :]
    xs_ref[0:CONV_PAD, :] = xs_ref[c:c + CONV_PAD, :]
    act = _silu(conv)
    xs = act[:, :SSD_W]
    bm = act[:, SSD_W:SSD_W + SSD_GROUPS * n]
    cm = act[:, SSD_W + SSD_GROUPS * n:]

    dt_in = dt_ref[...] + dtb_ref[...]
    dt = jnp.maximum(dt_in, 0.0) + jnp.log1p(jnp.exp(-jnp.abs(dt_in)))
    row = lax.broadcasted_iota(jnp.int32, (c, c), 0)
    col = lax.broadcasted_iota(jnp.int32, (c, c), 1)
    causal = row >= col
    a_cum = jnp.dot(causal.astype(F32), dt * a_ref[...], precision=HIGHEST,
                    preferred_element_type=F32)
    a_cum_t = a_cum.T
    a_last = a_cum[c - 1:c, :]
    dec_state = jnp.exp(a_last - a_cum)
    exp_a = jnp.exp(a_cum)
    chunk_dec = jnp.exp(a_last)

    heads_per_group = SSD_HEADS // SSD_GROUPS
    outs = []
    for g in range(SSD_GROUPS):
        bg = bm[:, g * n:(g + 1) * n].astype(BF16)
        cg = cm[:, g * n:(g + 1) * n].astype(BF16)
        cb = _dot_nt(cg, bg)
        for r in range(heads_per_group):
            h = g * heads_per_group + r
            lmat = jnp.exp(jnp.where(causal, a_cum[:, h:h + 1] - a_cum_t[h:h + 1, :], -jnp.inf))
            xh = xs[:, h * p:(h + 1) * p]
            xdt = xh * dt[:, h:h + 1]
            state = s_ref[h]
            y = _dot((cb * lmat).astype(BF16), xdt.astype(BF16))
            y = y + _dot(cg, state.astype(BF16)) * exp_a[:, h:h + 1]
            s_ref[h] = state * chunk_dec[:, h:h + 1] + _dot_tn(
                bg, (xdt * dec_state[:, h:h + 1]).astype(BF16))
            outs.append(y + xh * d_ref[:, h * p:(h + 1) * p])
    y = jnp.concatenate(outs, axis=-1)
    out_ref[...] = _rms(y * _silu(z_ref[...].astype(F32)), ng_ref[...]).astype(out_ref.dtype)


def _ssd(z, xbc, dt, conv_w, conv_b, dt_bias, a_log, d_skip, norm_g, batch, seq):
    c = SSD_CHUNK
    nc = seq // c
    pad = lambda v: jnp.pad(v, (0, LANES - v.shape[0]))[None, :]
    a_neg = pad(-jnp.exp(a_log))
    d_wide = jnp.repeat(d_skip, SSD_HEAD_DIM)[None, :]
    full = lambda shape: pl.BlockSpec(shape, lambda b, j: (0,) * len(shape))
    blk = lambda w: pl.BlockSpec((c, w), lambda b, j: (b * nc + j, 0))
    return pl.pallas_call(
        _ssd_kernel,
        grid=(batch, nc),
        in_specs=[blk(SSD_W), blk(SSD_XBC_W), blk(LANES),
                  full((SSD_CONV, SSD_XBC_W)), full((1, SSD_XBC_W)), full((1, LANES)), full((1, LANES)),
                  full((1, SSD_W)), full((1, SSD_W))],
        out_specs=blk(SSD_W),
        out_shape=jax.ShapeDtypeStruct((batch * seq, SSD_W), BF16),
        scratch_shapes=[pltpu.VMEM((c + CONV_PAD, SSD_XBC_W), F32),
                        pltpu.VMEM((SSD_HEADS, SSD_STATE, SSD_HEAD_DIM), F32)],
        compiler_params=_cparams("parallel", "arbitrary"),
        name="ssd",
    )(z, xbc, dt, conv_w, conv_b[None, :], pad(dt_bias), a_neg, d_wide, norm_g[None, :])


def _s5_kernel(u_ref, t1_ref, pre_ref, pim_ref, qre_ref, qim_ref, are_ref, aim_ref, y_ref,
               ere_ref, eim_ref, xre_ref, xim_ref, sre_ref, sim_ref):
    batch, tb, _ = u_ref.shape
    cs = S5_CHUNK
    ns = tb // cs

    @pl.when(pl.program_id(1) == 0)
    def _():
        sre_ref[...] = jnp.zeros_like(sre_ref)
        sim_ref[...] = jnp.zeros_like(sim_ref)

    u = jnp.concatenate(
        [jnp.concatenate([u_ref[b, pl.ds(s, ns, stride=cs), :] for s in range(cs)], axis=-1)
         for b in range(batch)], axis=0).astype(BF16)
    n_tiles = ere_ref.shape[0]
    lanes_of = lambda j: slice(j * LANES, (j + 1) * LANES)
    e_re = _dot(u, pre_ref[0])
    e_im = _dot(u, pim_ref[0])
    for j in range(n_tiles):
        ere_ref[j] = e_re[:, lanes_of(j)]
        eim_ref[j] = e_im[:, lanes_of(j)]
    shape = (batch, LANES)
    ar = [jnp.broadcast_to(are_ref[0, :, lanes_of(j)], shape) for j in range(n_tiles)]
    ai = [jnp.broadcast_to(aim_ref[0, :, lanes_of(j)], shape) for j in range(n_tiles)]

    def step(n, carry):
        rows = pl.ds(n, batch, stride=ns)
        out = []
        for j in range(n_tiles):
            xr, xi = carry[j]
            xre_ref[j, rows, :] = xr
            xim_ref[j, rows, :] = xi
            out.append((ar[j] * xr - ai[j] * xi + ere_ref[j, rows, :],
                        ar[j] * xi + ai[j] * xr + eim_ref[j, rows, :]))
        return tuple(out)

    init = tuple((sre_ref[j], sim_ref[j]) for j in range(n_tiles))
    final = lax.fori_loop(0, ns, step, init)
    for j in range(n_tiles):
        sre_ref[j], sim_ref[j] = final[j]
    x_re = jnp.concatenate([xre_ref[j] for j in range(n_tiles)], axis=-1).astype(BF16)
    x_im = jnp.concatenate([xim_ref[j] for j in range(n_tiles)], axis=-1).astype(BF16)
    y = _dot(u, t1_ref[0]) + _dot(x_re, qre_ref[0]) + _dot(x_im, qim_ref[0])
    for b in range(batch):
        for s in range(cs):
            y_ref[b, pl.ds(s, ns, stride=cs), :] = y[b * ns:(b + 1) * ns, s * LANES:(s + 1) * LANES]


def _s5_operators(a_re, a_im, b_re, b_im, c_re, c_im, log_step):
    cs = S5_CHUNK
    ein = functools.partial(jnp.einsum, precision=HIGHEST)
    delta = jnp.exp(log_step)[:, None]
    ar, ai = a_re, a_im
    mag = jnp.exp(ar * delta)
    ang = ai * delta
    lr, li = mag * jnp.cos(ang), mag * jnp.sin(ang)
    den = ar * ar + ai * ai
    nr, ni = lr - 1.0, li
    cr = (nr * ar + ni * ai) / den
    ci = (ni * ar - nr * ai) / den
    bbr = cr[..., None] * b_re - ci[..., None] * b_im
    bbi = cr[..., None] * b_im + ci[..., None] * b_re
    k = jnp.arange(cs + 1, dtype=F32)
    pmag = jnp.exp((ar * delta)[..., None] * k)
    pang = ang[..., None] * k
    pr, pi = pmag * jnp.cos(pang), pmag * jnp.sin(pang)
    clr = c_re[..., None] * pr[:, None] - c_im[..., None] * pi[:, None]
    cli = c_re[..., None] * pi[:, None] + c_im[..., None] * pr[:, None]
    kern = ein('gcpk,gpd->gkcd', clr, bbr) - ein('gcpk,gpd->gkcd', cli, bbi)
    kern = jnp.concatenate([kern[:, :cs], jnp.zeros_like(kern[:, :1])], axis=1)
    s = jnp.arange(cs)
    lag = jnp.where(s[None, :] >= s[:, None], s[None, :] - s[:, None], cs)
    nh, ng = S5_HALVES, S5_LANE_GROUPS
    eye = jnp.eye(ng, dtype=F32)
    halves = lambda t: t.reshape((nh, ng) + t.shape[1:])
    bd = halves(kern).transpose(0, 2, 1, 4, 3)
    bd = (bd[:, :, :, :, None, :] * eye[None, None, :, None, :, None]).reshape(nh, cs + 1, LANES, LANES)
    t1 = bd.astype(BF16)[:, lag].transpose(0, 1, 3, 2, 4).reshape(nh, cs * LANES, cs * LANES)
    rev = cs - 1 - s
    prr, pri = pr[..., rev], pi[..., rev]
    p_re = prr[..., None] * bbr[:, :, None] - pri[..., None] * bbi[:, :, None]
    p_im = prr[..., None] * bbi[:, :, None] + pri[..., None] * bbr[:, :, None]

    def flat_p(t):
        t = halves(t.transpose(0, 2, 3, 1)).transpose(0, 2, 1, 3, 4)
        t = t[:, :, :, :, None, :] * eye[None, None, :, None, :, None]
        return t.reshape(nh, cs * LANES, ng * S5_STATE).astype(BF16)

    def flat_q(t):
        t = halves(t.transpose(0, 2, 3, 1))
        t = t[:, :, :, :, None, :] * eye[None, :, None, None, :, None]
        return t.reshape(nh, ng * S5_STATE, cs * LANES).astype(BF16)

    a_chunk_re = pr[..., cs].reshape(nh, 1, ng * S5_STATE)
    a_chunk_im = pi[..., cs].reshape(nh, 1, ng * S5_STATE)
    return (t1, flat_p(p_re), flat_p(p_im), flat_q(clr[..., 1:]), flat_q(-cli[..., 1:]),
            a_chunk_re, a_chunk_im)


def _s5(u, ops, batch, seq):
    tb = S5_CHUNK * S5_BLOCK_STEPS
    rows = batch * S5_BLOCK_STEPS
    flat = S5_CHUNK * LANES
    nstate = S5_LANE_GROUPS * S5_STATE
    per_h = lambda a, b: pl.BlockSpec((1, a, b), lambda h, j: (h, 0, 0))
    seq_blk = pl.BlockSpec((batch, tb, LANES), lambda h, j: (0, j, h))
    y = pl.pallas_call(
        _s5_kernel,
        grid=(S5_HALVES, seq // tb),
        in_specs=[seq_blk, per_h(flat, flat), per_h(flat, nstate), per_h(flat, nstate),
                  per_h(nstate, flat), per_h(nstate, flat), per_h(1, nstate), per_h(1, nstate)],
        out_specs=seq_blk,
        out_shape=jax.ShapeDtypeStruct((batch, seq, S5_W), F32),
        scratch_shapes=([pltpu.VMEM((nstate // LANES, rows, LANES), F32)] * 4
                        + [pltpu.VMEM((nstate // LANES, batch, LANES), F32)] * 2),
        compiler_params=_cparams("parallel", "arbitrary"),
        name="s5",
    )(u.reshape(batch, seq, S5_W), *ops)
    return y.reshape(batch * seq, S5_W)


def _out_proj_kernel(h_ref, r_ref, m_ref, ys_ref, u_ref, d_ref, wg_ref, bg_ref, wo_ref, o_ref):
    y = ys_ref[...].astype(F32) + d_ref[...] * u_ref[...].astype(F32)
    g = jax.nn.gelu(y)
    s = g * jax.nn.sigmoid(_dot(g.astype(BF16), wg_ref[...]) + bg_ref[...])
    acc = _dot(r_ref[...], wo_ref[0:RET_W, :])
    acc = acc + _dot(m_ref[...], wo_ref[RET_W:RET_W + SSD_W, :])
    acc = acc + _dot(s.astype(BF16), wo_ref[RET_W + SSD_W:, :])
    o_ref[...] = h_ref[...] + acc


def _out_proj(h, out_r, out_m, y_s, u, d_s5, w_glu, b_glu, w_out):
    t, d = h.shape
    tm = ROW_TILE
    row = lambda w: pl.BlockSpec((tm, w), lambda i: (i, 0))
    full = lambda a, b: pl.BlockSpec((a, b), lambda i: (0, 0))
    return pl.pallas_call(
        _out_proj_kernel,
        grid=(t // tm,),
        in_specs=[row(d), row(RET_W), row(SSD_W), row(S5_W), row(S5_W),
                  full(1, S5_W), full(S5_W, S5_W), full(1, S5_W), full(d, d)],
        out_specs=row(d),
        out_shape=jax.ShapeDtypeStruct((t, d), F32),
        compiler_params=_cparams("parallel"),
        name="out_proj",
    )(h, out_r, out_m, y_s, u, d_s5, w_glu, b_glu, w_out)


def _cross_kernel(h_ref, g_ref, wq_ref, k_ref, v_ref, wo_ref, o_ref):
    h = h_ref[...]
    d = h.shape[-1]
    dh = d // CROSS_HEADS
    q = _dot(_rms(h, g_ref[...]).astype(BF16), wq_ref[...]).astype(BF16)
    outs = []
    for i in range(CROSS_HEADS):
        sl = slice(i * dh, (i + 1) * dh)
        s = _dot_nt(q[:, sl], k_ref[:, sl]) * (dh ** -0.5)
        p = jnp.exp(s - jnp.max(s, axis=-1, keepdims=True))
        o = _dot(p.astype(BF16), v_ref[:, sl])
        outs.append(o / jnp.sum(p, axis=-1, keepdims=True))
    o = jnp.concatenate(outs, axis=-1).astype(BF16)
    _store_token_tiles(o_ref, h + _dot(o, wo_ref[...]))


def _cross(h, g, wq, kv, layer, wo, seq, mem_len):
    t, d = h.shape
    tm = ROW_TILE
    tiles_per_seq = seq // tm
    full = lambda a, b: pl.BlockSpec((a, b), lambda i: (0, 0))
    return pl.pallas_call(
        _cross_kernel,
        grid=(t // tm,),
        in_specs=[pl.BlockSpec((tm, d), lambda i: (i, 0)), full(1, d), full(d, d),
                  pl.BlockSpec((mem_len, d), lambda i: (i // tiles_per_seq, 2 * layer)),
                  pl.BlockSpec((mem_len, d), lambda i: (i // tiles_per_seq, 2 * layer + 1)),
                  full(d, d)],
        out_specs=pl.BlockSpec((tm, d // LANES, LANES), lambda i: (i, 0, 0)),
        out_shape=jax.ShapeDtypeStruct((t, d // LANES, LANES), F32),
        compiler_params=_cparams("parallel"),
        name="cross_attn",
    )(h, g, wq, kv, kv, wo)


_GROUP_LANE0 = N_EXPERTS


def _router_kernel(h_ref, g_ref, w_ref, b_ref, info_ref, infot_ref, cnt_ref, carry_ref):
    @pl.when(pl.program_id(0) == 0)
    def _():
        carry_ref[...] = jnp.zeros_like(carry_ref)

    xn = _rms(_load_token_tiles(h_ref), g_ref[...])
    logits = jnp.dot(xn, w_ref[...], precision=HIGHEST, preferred_element_type=F32) + b_ref[...]
    tm = logits.shape[0]
    lane = lax.broadcasted_iota(jnp.int32, logits.shape, 1).astype(F32)
    neg = -jnp.inf

    def first_argmax(vals):
        m = jnp.max(vals, axis=-1, keepdims=True)
        return m, jnp.min(jnp.where(vals == m, lane, float(LANES)), axis=-1, keepdims=True)

    gl = jnp.where((lane >= _GROUP_LANE0) & (lane < _GROUP_LANE0 + MOE_GROUPS), logits, neg)
    gmax, glane = first_argmax(gl)
    pg = 1.0 / jnp.sum(jnp.exp(gl - gmax), axis=-1, keepdims=True)
    lo = (glane - _GROUP_LANE0) * EXPERTS_PER_GROUP
    el = jnp.where((lane >= lo) & (lane < lo + EXPERTS_PER_GROUP), logits, neg)
    m1, e1 = first_argmax(el)
    m2, e2 = first_argmax(jnp.where(lane == e1, neg, el))
    p2 = jnp.exp(m2 - m1)
    gate1 = pg / (1.0 + p2)
    gate2 = pg * p2 / (1.0 + p2)

    hot = jnp.where((lane == e1) | (lane == e2), 1.0, 0.0)
    row = lax.broadcasted_iota(jnp.int32, (tm, tm), 0)
    col = lax.broadcasted_iota(jnp.int32, (tm, tm), 1)
    before = jnp.where(row > col, 1.0, 0.0).astype(BF16)
    cum = _dot(before, hot.astype(BF16)) + carry_ref[...]
    rank1 = jnp.sum(jnp.where(lane == e1, cum, 0.0), axis=-1, keepdims=True)
    rank2 = jnp.sum(jnp.where(lane == e2, cum, 0.0), axis=-1, keepdims=True)
    carry_ref[...] = carry_ref[...] + jnp.sum(hot, axis=0, keepdims=True)
    cnt_ref[...] = carry_ref[...]

    info = jnp.zeros(logits.shape, F32)
    for i, val in enumerate((e1, e2, rank1, rank2, gate1, gate2)):
        info = jnp.where(lane == i, val, info)
    info_ref[...] = info
    infot_ref[...] = info.T[0:_INFO_ROWS, :]


_INFO_ROWS = 8


def _router(h3, g, w_r, b_r):
    t, nt, _ = h3.shape
    d = nt * LANES
    tm = ROW_TILE
    return pl.pallas_call(
        _router_kernel,
        grid=(t // tm,),
        in_specs=[pl.BlockSpec((tm, nt, LANES), lambda i: (i, 0, 0)),
                  pl.BlockSpec((1, d), lambda i: (0, 0)),
                  pl.BlockSpec((d, LANES), lambda i: (0, 0)),
                  pl.BlockSpec((1, LANES), lambda i: (0, 0))],
        out_specs=[pl.BlockSpec((tm, LANES), lambda i: (i, 0)),
                   pl.BlockSpec((_INFO_ROWS, tm), lambda i: (0, i)),
                   pl.BlockSpec((1, LANES), lambda i: (0, 0))],
        out_shape=[jax.ShapeDtypeStruct((t, LANES), F32), jax.ShapeDtypeStruct((_INFO_ROWS, t), F32),
                   jax.ShapeDtypeStruct((1, LANES), F32)],
        scratch_shapes=[pltpu.VMEM((1, LANES), F32)],
        compiler_params=_cparams("arbitrary"),
        name="moe_router",
    )(h3, g, w_r, b_r)


def _row_copy(src_hbm, dst_vmem, src_row, dst_row, sem):
    return pltpu.make_async_copy(src_hbm.at[pl.ds(src_row, 1)], dst_vmem.at[pl.ds(dst_row, 1)], sem)


_ISSUE_UNROLL = 8


def _start_row_gather(src_hbm, idx_ref, n_rows, dst, sem):
    def body(j, carry):
        for p in range(2):
            r = 2 * j + p
            _row_copy(src_hbm, dst, idx_ref[0, 0, r], r, sem).start(priority=p)
        return carry

    lax.fori_loop(0, n_rows // 2, body, 0, unroll=_ISSUE_UNROLL)


def _wait_row_gather(src_hbm, n_rows, dst, sem):
    pltpu.make_async_copy(src_hbm.at[pl.ds(0, n_rows)], dst, sem).wait()


def _expert_kernel(be_ref, nu_ref, cur_ref, nxt_ref, h_hbm, g_ref, wg_ref, wu_ref, wd_ref, y_ref,
                   xbuf, wg_s, wu_s, wd_s, sem):
    i = pl.program_id(0)
    n_used = nu_ref[0]
    slot = i % 2
    used = i < n_used

    @pl.when((i == 0) & used)
    def _():
        _start_row_gather(h_hbm, cur_ref, MOE_BLOCK, xbuf.at[0], sem.at[0])

    @pl.when(i + 1 < n_used)
    def _():
        _start_row_gather(h_hbm, nxt_ref, MOE_BLOCK, xbuf.at[1 - slot], sem.at[1 - slot])

    @pl.when(used & ((i == 0) | (be_ref[i] != be_ref[jnp.maximum(i - 1, 0)])))
    def _():
        wg_s[...] = wg_ref[0, 0].astype(BF16)
        wu_s[...] = wu_ref[0, 0].astype(BF16)
        wd_s[...] = wd_ref[0, 0].astype(BF16)

    @pl.when(used)
    def _():
        _wait_row_gather(h_hbm, MOE_BLOCK, xbuf.at[slot], sem.at[slot])
        xn = _rms(_load_token_tiles(xbuf.at[slot]), g_ref[...]).astype(BF16)
        hid = (_silu(_dot(xn, wg_s[...])) * _dot(xn, wu_s[...])).astype(BF16)
        _store_token_tiles(y_ref, _dot(hid, wd_s[...]))

    @pl.when(jnp.logical_not(used))
    def _():
        y_ref[...] = jnp.zeros_like(y_ref)


def _experts(h3, g, block_e, n_used, src_tok, w_gate, w_up, w_down, layer):
    t, nt, _ = h3.shape
    d = nt * LANES
    nb = block_e.shape[0]
    de = w_gate.shape[-1]
    idx_blk = lambda f: pl.BlockSpec((1, 1, MOE_BLOCK), lambda i, be, nu: (f(i), 0, 0),
                                     memory_space=pltpu.SMEM)
    grid_spec = pltpu.PrefetchScalarGridSpec(
        num_scalar_prefetch=2,
        grid=(nb,),
        in_specs=[idx_blk(lambda i: i), idx_blk(lambda i: jnp.minimum(i + 1, nb - 1)),
                  pl.BlockSpec(memory_space=pl.ANY),
                  pl.BlockSpec((1, d), lambda i, be, nu: (0, 0)),
                  pl.BlockSpec((1, 1, d, de), lambda i, be, nu: (layer, be[i], 0, 0)),
                  pl.BlockSpec((1, 1, d, de), lambda i, be, nu: (layer, be[i], 0, 0)),
                  pl.BlockSpec((1, 1, de, d), lambda i, be, nu: (layer, be[i], 0, 0))],
        out_specs=pl.BlockSpec((MOE_BLOCK, nt, LANES), lambda i, be, nu: (i, 0, 0)),
        scratch_shapes=[pltpu.VMEM((2, MOE_BLOCK, nt, LANES), F32), pltpu.VMEM((d, de), BF16),
                        pltpu.VMEM((d, de), BF16), pltpu.VMEM((de, d), BF16),
                        pltpu.SemaphoreType.DMA((2,))],
    )
    src3 = src_tok.reshape(nb, 1, MOE_BLOCK)
    return pl.pallas_call(
        _expert_kernel,
        grid_spec=grid_spec,
        out_shape=jax.ShapeDtypeStruct((nb * MOE_BLOCK, nt, LANES), F32),
        compiler_params=_cparams("arbitrary"),
        name="moe_experts",
    )(block_e, n_used, src3, src3, h3, g, w_gate, w_up, w_down)


def _combine_kernel(dest_ref, h_ref, info_ref, y_hbm, fg_ref, o_ref, ybuf, sem, *, final_norm):
    tm = h_ref.shape[0]

    def start(r, carry):
        for k in range(2):
            _row_copy(y_hbm, ybuf.at[k], dest_ref[0, 0, k * tm + r], r, sem.at[k]).start(priority=k)
        return carry

    lax.fori_loop(0, tm, start, 0, unroll=_ISSUE_UNROLL)
    for k in range(2):
        _wait_row_gather(y_hbm, tm, ybuf.at[k], sem.at[k])
    info = info_ref[...]
    out = _load_token_tiles(h_ref) + (info[:, 4:5] * _load_token_tiles(ybuf.at[0])
                                      + info[:, 5:6] * _load_token_tiles(ybuf.at[1]))
    if final_norm:
        out = _rms(out, fg_ref[...])
    o_ref[...] = out


def _combine(h3, info, dest, ybuf, final_g, final_norm):
    t, nt, _ = h3.shape
    d = nt * LANES
    tm = MOE_ROW_TILE
    return pl.pallas_call(
        functools.partial(_combine_kernel, final_norm=final_norm),
        grid=(t // tm,),
        in_specs=[pl.BlockSpec((1, 1, 2 * tm), lambda i: (i, 0, 0), memory_space=pltpu.SMEM),
                  pl.BlockSpec((tm, nt, LANES), lambda i: (i, 0, 0)),
                  pl.BlockSpec((tm, LANES), lambda i: (i, 0)),
                  pl.BlockSpec(memory_space=pl.ANY),
                  pl.BlockSpec((1, d), lambda i: (0, 0))],
        out_specs=pl.BlockSpec((tm, d), lambda i: (i, 0)),
        out_shape=jax.ShapeDtypeStruct((t, d), F32),
        scratch_shapes=[pltpu.VMEM((2, tm, nt, LANES), F32), pltpu.SemaphoreType.DMA((2,))],
        compiler_params=_cparams("arbitrary"),
        name="moe_combine",
    )(dest.reshape(2, t // tm, tm).transpose(1, 0, 2).reshape(t // tm, 1, 2 * tm), h3, info, ybuf, final_g)


def _moe(h3, g, w_rg, b_rg, w_re, b_re, w_gate, w_up, w_down, layer, final_g, final_norm):
    t = h3.shape[0]
    d = h3.shape[1] * LANES
    pad_cols = LANES - N_EXPERTS - MOE_GROUPS
    w_r = jnp.concatenate([w_re, w_rg, jnp.zeros((d, pad_cols), F32)], axis=1)
    b_r = jnp.concatenate([b_re, b_rg, jnp.zeros((pad_cols,), F32)])[None, :]
    info, infot, cnt = _router(h3, g, w_r, b_r)
    expert = infot[0:2].astype(jnp.int32)
    rank = infot[2:4].astype(jnp.int32)
    counts = cnt[0, :N_EXPERTS].astype(jnp.int32)
    padded = (counts + MOE_BLOCK - 1) // MOE_BLOCK * MOE_BLOCK
    pends = jnp.cumsum(padded)
    pstarts = pends - padded
    starts = jnp.cumsum(counts) - counts
    ids = jnp.arange(N_EXPERTS, dtype=jnp.int32)
    dest = jnp.sum(jnp.where(expert[..., None] == ids, pstarts, 0), axis=-1) + rank
    nb = (2 * t) // MOE_BLOCK + N_EXPERTS
    blk = jnp.arange(nb, dtype=jnp.int32)
    block_e = jnp.minimum(jnp.sum(pends[None, :] <= blk[:, None] * MOE_BLOCK, axis=1), N_EXPERTS - 1)
    block_e = block_e.astype(jnp.int32)
    n_used = (pends[-1:] // MOE_BLOCK).astype(jnp.int32)
    keys = expert * t + jnp.arange(t, dtype=jnp.int32)[None, :]
    sorted_tok = jnp.sort(keys.reshape(-1)) % t
    shift = jnp.sum(jnp.where(block_e[:, None] == ids, pstarts - starts, 0), axis=-1)
    pair = blk[:, None] * MOE_BLOCK + jnp.arange(MOE_BLOCK, dtype=jnp.int32)[None, :] - shift[:, None]
    src_tok = sorted_tok[jnp.clip(pair, 0, 2 * t - 1)]
    ybuf = _experts(h3, g, block_e, n_used, src_tok, w_gate, w_up, w_down, layer)
    return _combine(h3, info, dest, ybuf, final_g, final_norm)


def kernel(x, mem, norm_mix_g, w_in, ret_gn_g, ssd_conv_w, ssd_conv_b, ssd_dt_bias, ssd_A_log, ssd_D,
           ssd_norm_g, s5_A_re, s5_A_im, s5_B_re, s5_B_im, s5_C_re, s5_C_im, s5_log_step, s5_D, s5_w_glu,
           s5_b_glu, w_out, norm_cross_g, mem_norm_g, w_cq, w_ck, w_cv, w_co, norm_ffn_g, w_route_group,
           b_route_group, w_route_expert, b_route_expert, w_gate, w_up, w_down, norm_final_g):
    batch, seq, d = x.shape
    depth = w_in.shape[0]
    mem_len = mem.shape[1]
    t = batch * seq
    assert d == RET_W * 4 and t % ROW_TILE == 0 and seq % ROW_TILE == 0
    assert seq % RET_CHUNK == 0 and seq % SSD_CHUNK == 0 and seq % (S5_CHUNK * S5_BLOCK_STEPS) == 0
    assert t % MOE_BLOCK == 0 and t % MOE_ROW_TILE == 0
    row = lambda v: v[None, :]
    h = x.reshape(t, d)

    w_kv = jnp.concatenate([w for i in range(depth) for w in (w_ck[i], w_cv[i])], axis=1).astype(BF16)
    kv = _norm_matmul(mem.reshape(batch * mem_len, d), row(mem_norm_g), w_kv, BF16,
                      tm=mem_len, tn=d)
    ret_tables = _retention_tables(seq)
    c0 = _QKVG_W + SSD_W + SSD_XBC_W
    c1 = c0 + SSD_HEADS

    for i in range(depth):
        w_pack = jnp.concatenate(
            [w_in[i][:, :c0], w_in[i][:, c1:], w_in[i][:, c0:c1], jnp.zeros((d, LANES - SSD_HEADS), F32)],
            axis=1).astype(BF16)
        qkvg, z, xbc, u, dt = _in_proj(h, row(norm_mix_g[i]), w_pack)
        out_r = _retention(qkvg, ret_tables, row(ret_gn_g[i]), batch, seq)
        out_m = _ssd(z, xbc, dt, ssd_conv_w[i], ssd_conv_b[i], ssd_dt_bias[i], ssd_A_log[i], ssd_D[i],
                     ssd_norm_g[i], batch, seq)
        s5_ops = _s5_operators(s5_A_re[i], s5_A_im[i], s5_B_re[i], s5_B_im[i], s5_C_re[i], s5_C_im[i],
                               s5_log_step[i])
        y_s = _s5(u, s5_ops, batch, seq)
        h = _out_proj(h, out_r, out_m, y_s, u, row(s5_D[i]), s5_w_glu[i].astype(BF16), row(s5_b_glu[i]),
                      w_out[i].astype(BF16))
        h = _cross(h, row(norm_cross_g[i]), w_cq[i].astype(BF16), kv, i, w_co[i].astype(BF16), seq, mem_len)
        h = _moe(h, row(norm_ffn_g[i]), w_route_group[i], b_route_group[i], w_route_expert[i],
                 b_route_expert[i], w_gate, w_up, w_down, i, row(norm_final_g),
                 final_norm=(i == depth - 1))
    return h.reshape(batch, seq, d)
```

```python
import functools
import math

import jax
import jax.numpy as jnp
from jax import lax
from jax.experimental import pallas as pl
from jax.experimental.pallas import tpu as pltpu

F32 = jnp.float32
BF16 = jnp.bfloat16
HIGHEST = lax.Precision.HIGHEST

EPS = 1e-6
RET_HEADS = 4
RET_HEAD_DIM = 64
RET_W = RET_HEADS * RET_HEAD_DIM
ROPE_BASE = 10000.0
SSD_HEAD_DIM = 64
SSD_HEADS = 8
SSD_GROUPS = 2
SSD_STATE = 128
SSD_CONV = 4
SSD_W = SSD_HEADS * SSD_HEAD_DIM
SSD_XBC_W = SSD_W + 2 * SSD_GROUPS * SSD_STATE
S5_GROUP = 16
S5_GROUPS = 16
S5_STATE = 64
S5_W = S5_GROUP * S5_GROUPS
CROSS_HEADS = 4
MOE_GROUPS = 4
EXPERTS_PER_GROUP = 8
N_EXPERTS = MOE_GROUPS * EXPERTS_PER_GROUP

LANES = 128
ROW_TILE = 512
RET_CHUNK = 256
SSD_CHUNK = 128
S5_CHUNK = 8
S5_LANE_GROUPS = LANES // S5_GROUP
S5_HALVES = S5_W // LANES
S5_BLOCK_STEPS = 64
MOE_BLOCK = 256
MOE_ROW_TILE = 256
VMEM_LIMIT = 48 * 1024 * 1024


def _cparams(*sem):
    return pltpu.CompilerParams(dimension_semantics=sem, vmem_limit_bytes=VMEM_LIMIT)


def _rms(x, g):
    return x * lax.rsqrt(jnp.mean(x * x, axis=-1, keepdims=True) + EPS) * g


def _silu(x):
    return x * jax.nn.sigmoid(x)


def _dot(a, b):
    return jnp.dot(a, b, preferred_element_type=F32)


def _dot_nt(a, b):
    return lax.dot_general(a, b, (((1,), (1,)), ((), ())), preferred_element_type=F32)


def _dot_tn(a, b):
    return lax.dot_general(a, b, (((0,), (0,)), ((), ())), preferred_element_type=F32)


def _norm_matmul_kernel(x_ref, g_ref, w_ref, o_ref):
    xn = _rms(x_ref[...], g_ref[...]).astype(BF16)
    o_ref[...] = _dot(xn, w_ref[...]).astype(o_ref.dtype)


def _norm_matmul(x, g, w, out_dtype, tm, tn):
    m, d = x.shape
    n = w.shape[1]
    return pl.pallas_call(
        _norm_matmul_kernel,
        grid=(m // tm, n // tn),
        in_specs=[pl.BlockSpec((tm, d), lambda i, j: (i, 0)),
                  pl.BlockSpec((1, d), lambda i, j: (0, 0)),
                  pl.BlockSpec((d, tn), lambda i, j: (0, j))],
        out_specs=pl.BlockSpec((tm, tn), lambda i, j: (i, j)),
        out_shape=jax.ShapeDtypeStruct((m, n), out_dtype),
        compiler_params=_cparams("parallel", "parallel"),
        name="norm_matmul",
    )(x, g, w)


_QKVG_W = 4 * RET_W
_IN_SPLITS = (_QKVG_W, SSD_W, SSD_XBC_W, S5_W, LANES)


def _in_proj_kernel(h_ref, g_ref, w_ref, qkvg_ref, z_ref, xbc_ref, u_ref, dt_ref):
    xn = _rms(h_ref[...], g_ref[...]).astype(BF16)
    lo = 0
    for ref, width in zip((qkvg_ref, z_ref, xbc_ref, u_ref, dt_ref), _IN_SPLITS):
        ref[...] = _dot(xn, w_ref[:, lo:lo + width]).astype(ref.dtype)
        lo += width


def _in_proj(h, g, w_pack):
    t, d = h.shape
    tm = ROW_TILE
    n = w_pack.shape[1]
    dts = (BF16, BF16, BF16, F32, F32)
    return pl.pallas_call(
        _in_proj_kernel,
        grid=(t // tm,),
        in_specs=[pl.BlockSpec((tm, d), lambda i: (i, 0)),
                  pl.BlockSpec((1, d), lambda i: (0, 0)),
                  pl.BlockSpec((d, n), lambda i: (0, 0))],
        out_specs=[pl.BlockSpec((tm, w), lambda i: (i, 0)) for w in _IN_SPLITS],
        out_shape=[jax.ShapeDtypeStruct((t, w), dt) for w, dt in zip(_IN_SPLITS, dts)],
        compiler_params=_cparams("parallel"),
        name="in_proj",
    )(h, g, w_pack)


def _retention_kernel(qkvg_ref, cos_ref, sin_ref, decay_ref, qdec_ref, kdec_ref, cdec_ref, gn_ref,
                      out_ref, s_ref):
    @pl.when(pl.program_id(1) == 0)
    def _():
        s_ref[...] = jnp.zeros_like(s_ref)

    x = qkvg_ref[...]
    w = RET_W
    q = x[:, 0:w].astype(F32)
    k = x[:, w:2 * w].astype(F32)
    v = x[:, 2 * w:3 * w]
    g = x[:, 3 * w:4 * w].astype(F32)
    half = RET_HEAD_DIM // 2
    lane = lax.broadcasted_iota(jnp.int32, q.shape, 1)
    first_half = (lane % RET_HEAD_DIM) < half

    def rot(t):
        swapped = jnp.where(first_half, pltpu.roll(t, w - half, 1), pltpu.roll(t, half, 1))
        return t * cos_ref[...] + swapped * sin_ref[...]

    qr = rot(q)
    kr = rot(k) * (RET_HEAD_DIM ** -0.5)
    qb = qr.astype(BF16)
    kb = kr.astype(BF16)
    qd = (qr * qdec_ref[...]).astype(BF16)
    kd = (kr * kdec_ref[...]).astype(BF16)
    outs = []
    for h in range(RET_HEADS):
        sl = slice(h * RET_HEAD_DIM, (h + 1) * RET_HEAD_DIM)
        s = _dot_nt(qb[:, sl], kb[:, sl]) * decay_ref[h]
        state = s_ref[h]
        y = _dot(s.astype(BF16), v[:, sl]) + _dot(qd[:, sl], state.astype(BF16))
        s_ref[h] = state * cdec_ref[h] + _dot_tn(kd[:, sl], v[:, sl])
        outs.append(y * lax.rsqrt(jnp.mean(y * y, axis=-1, keepdims=True) + EPS))
    yr = jnp.concatenate(outs, axis=-1)
    out_ref[...] = (_silu(g) * (yr * gn_ref[...])).astype(out_ref.dtype)


def _retention_tables(seq):
    c = RET_CHUNK
    dh = RET_HEAD_DIM
    inv = ROPE_BASE ** (-jnp.arange(0, dh, 2, dtype=F32) / dh)
    ang = jnp.arange(seq, dtype=F32)[:, None] * inv[None, :]
    cos, sin = jnp.cos(ang), jnp.sin(ang)
    cos4 = jnp.tile(jnp.concatenate([cos, cos], axis=-1), (1, RET_HEADS))
    sin4 = jnp.tile(jnp.concatenate([-sin, sin], axis=-1), (1, RET_HEADS))
    lg = jnp.log1p(-(2.0 ** (-5.0 - jnp.arange(RET_HEADS, dtype=F32))))
    i = jnp.arange(c, dtype=F32)
    rel = i[:, None] - i[None, :]
    decay = jnp.where(rel[None] >= 0, jnp.exp(lg[:, None, None] * jnp.maximum(rel, 0.0)[None]), 0.0)
    per_head = lambda t: jnp.repeat(t.T, dh, axis=1)
    qdec = per_head(jnp.exp(lg[:, None] * (i + 1.0)[None]))
    kdec = per_head(jnp.exp(lg[:, None] * (c - 1.0 - i)[None]))
    cdec = jnp.broadcast_to(jnp.exp(lg * c)[:, None, None], (RET_HEADS, dh, dh))
    return cos4, sin4, decay, qdec, kdec, cdec


def _retention(qkvg, tables, gn, batch, seq):
    c = RET_CHUNK
    nc = seq // c
    cos4, sin4, decay, qdec, kdec, cdec = tables
    w = RET_W
    full = lambda shape: pl.BlockSpec(shape, lambda b, j: (0,) * len(shape))
    return pl.pallas_call(
        _retention_kernel,
        grid=(batch, nc),
        in_specs=[pl.BlockSpec((c, _QKVG_W), lambda b, j: (b * nc + j, 0)),
                  pl.BlockSpec((c, w), lambda b, j: (j, 0)),
                  pl.BlockSpec((c, w), lambda b, j: (j, 0)),
                  full((RET_HEADS, c, c)), full((c, w)), full((c, w)),
                  full((RET_HEADS, RET_HEAD_DIM, RET_HEAD_DIM)), full((1, w))],
        out_specs=pl.BlockSpec((c, w), lambda b, j: (b * nc + j, 0)),
        out_shape=jax.ShapeDtypeStruct((batch * seq, w), BF16),
        scratch_shapes=[pltpu.VMEM((RET_HEADS, RET_HEAD_DIM, RET_HEAD_DIM), F32)],
        compiler_params=_cparams("parallel", "arbitrary"),
        name="retention",
    )(qkvg, cos4, sin4, decay, qdec, kdec, cdec, gn)


def _split3(v):
    hi = v.astype(BF16)
    rest = v - hi.astype(F32)
    mid = rest.astype(BF16)
    return hi, mid, (rest - mid.astype(F32)).astype(BF16)


def _ssd_kernel(z_ref, xbc_ref, dt_ref, shift_ref, expand_ref, cw_ref, cb_ref, dtb_ref, a_ref, d_ref,
                ng_ref, out_ref, xcat_ref, s_ref):
    c = SSD_CHUNK
    p = SSD_HEAD_DIM
    n = SSD_STATE

    @pl.when(pl.program_id(1) == 0)
    def _():
        xcat_ref[0:c, :] = jnp.zeros((c, SSD_XBC_W), BF16)
        s_ref[...] = jnp.zeros_like(s_ref)

    x_cur = xbc_ref[...]
    xcat_ref[c:2 * c, :] = x_cur
    x_cat = xcat_ref[...]
    conv = cb_ref[...] + x_cur.astype(F32) * cw_ref[SSD_CONV - 1:SSD_CONV, :]
    for j in range(SSD_CONV - 1):
        conv = conv + _dot(shift_ref[j], x_cat) * cw_ref[j:j + 1, :]
    xcat_ref[0:c, :] = x_cur
    act = _silu(conv)
    xs = act[:, :SSD_W]
    bm = act[:, SSD_W:SSD_W + SSD_GROUPS * n]
    cm = act[:, SSD_W + SSD_GROUPS * n:]

    dt_in = dt_ref[...] + dtb_ref[...]
    dt = jnp.maximum(dt_in, 0.0) + jnp.log1p(jnp.exp(-jnp.abs(dt_in)))
    row = lax.broadcasted_iota(jnp.int32, (c, c), 0)
    col = lax.broadcasted_iota(jnp.int32, (c, c), 1)
    causal = row >= col
    a_cum = jnp.dot(causal.astype(F32), dt * a_ref[...], precision=HIGHEST,
                    preferred_element_type=F32)
    a_cum_t = a_cum.T

    expand = expand_ref[...]
    widen = lambda v: sum(_dot(term, expand) for term in _split3(v))
    dt_w = widen(dt)
    a_w = widen(a_cum)
    a_last_w = a_w[c - 1:c, :]
    chunk_dec_w = jnp.exp(a_last_w)
    xdt = xs * dt_w
    xdt_b = xdt.astype(BF16)
    xdec_b = (xdt * jnp.exp(a_last_w - a_w)).astype(BF16)

    heads_per_group = SSD_HEADS // SSD_GROUPS
    gw = heads_per_group * p
    y_diag, y_off = [], []
    for g in range(SSD_GROUPS):
        bg = bm[:, g * n:(g + 1) * n].astype(BF16)
        cg = cm[:, g * n:(g + 1) * n].astype(BF16)
        cb = _dot_nt(cg, bg)
        gl = slice(g * gw, (g + 1) * gw)
        states = s_ref[g]
        y_off.append(_dot(cg, states.astype(BF16)))
        s_ref[g] = states * chunk_dec_w[:, gl] + _dot_tn(bg, xdec_b[:, gl])
        for r in range(heads_per_group):
            h = g * heads_per_group + r
            lmat = jnp.exp(jnp.where(causal, a_cum[:, h:h + 1] - a_cum_t[h:h + 1, :], -jnp.inf))
            y_diag.append(_dot((cb * lmat).astype(BF16), xdt_b[:, h * p:(h + 1) * p]))
    y = (jnp.concatenate(y_diag, axis=-1) + jnp.concatenate(y_off, axis=-1) * jnp.exp(a_w)
         + xs * d_ref[...])
    out_ref[...] = _rms(y * _silu(z_ref[...].astype(F32)), ng_ref[...]).astype(out_ref.dtype)


def _ssd(z, xbc, dt, conv_w, conv_b, dt_bias, a_log, d_skip, norm_g, batch, seq):
    c = SSD_CHUNK
    nc = seq // c
    pad = lambda v: jnp.pad(v, (0, LANES - v.shape[0]))[None, :]
    a_neg = pad(-jnp.exp(a_log))
    d_wide = jnp.repeat(d_skip, SSD_HEAD_DIM)[None, :]
    t_idx = jnp.arange(c)[None, :, None]
    lag = (SSD_CONV - 1 - jnp.arange(SSD_CONV - 1))[:, None, None]
    shift = (jnp.arange(2 * c)[None, None, :] == c + t_idx - lag).astype(BF16)
    expand = (jnp.arange(LANES)[:, None] == jnp.arange(SSD_W)[None, :] // SSD_HEAD_DIM).astype(BF16)
    full = lambda shape: pl.BlockSpec(shape, lambda b, j: (0,) * len(shape))
    blk = lambda w: pl.BlockSpec((c, w), lambda b, j: (b * nc + j, 0))
    return pl.pallas_call(
        _ssd_kernel,
        grid=(batch, nc),
        in_specs=[blk(SSD_W), blk(SSD_XBC_W), blk(LANES), full((SSD_CONV - 1, c, 2 * c)),
                  full((LANES, SSD_W)), full((SSD_CONV, SSD_XBC_W)), full((1, SSD_XBC_W)),
                  full((1, LANES)), full((1, LANES)), full((1, SSD_W)), full((1, SSD_W))],
        out_specs=blk(SSD_W),
        out_shape=jax.ShapeDtypeStruct((batch * seq, SSD_W), BF16),
        scratch_shapes=[pltpu.VMEM((2 * c, SSD_XBC_W), BF16),
                        pltpu.VMEM((SSD_GROUPS, SSD_STATE, SSD_W // SSD_GROUPS), F32)],
        compiler_params=_cparams("parallel", "arbitrary"),
        name="ssd",
    )(z, xbc, dt, shift, expand, conv_w, conv_b[None, :], pad(dt_bias), a_neg, d_wide, norm_g[None, :])


def _s5_kernel(u_ref, t1_ref, pre_ref, pim_ref, qre_ref, qim_ref, are_ref, aim_ref, y_ref,
               ere_ref, eim_ref, xre_ref, xim_ref, sre_ref, sim_ref):
    batch, tb, _ = u_ref.shape
    cs = S5_CHUNK
    ns = tb // cs

    @pl.when(pl.program_id(1) == 0)
    def _():
        sre_ref[...] = jnp.zeros_like(sre_ref)
        sim_ref[...] = jnp.zeros_like(sim_ref)

    u = jnp.concatenate(
        [jnp.concatenate([u_ref[b, pl.ds(s, ns, stride=cs), :] for s in range(cs)], axis=-1)
         for b in range(batch)], axis=0).astype(BF16)
    n_tiles = ere_ref.shape[0]
    lanes_of = lambda j: slice(j * LANES, (j + 1) * LANES)
    e_re = _dot(u, pre_ref[0])
    e_im = _dot(u, pim_ref[0])
    for j in range(n_tiles):
        ere_ref[j] = e_re[:, lanes_of(j)]
        eim_ref[j] = e_im[:, lanes_of(j)]
    shape = (batch, LANES)
    ar = [jnp.broadcast_to(are_ref[0, :, lanes_of(j)], shape) for j in range(n_tiles)]
    ai = [jnp.broadcast_to(aim_ref[0, :, lanes_of(j)], shape) for j in range(n_tiles)]

    def step(n, carry):
        rows = pl.ds(n, batch, stride=ns)
        out = []
        for j in range(n_tiles):
            xr, xi = carry[j]
            xre_ref[j, rows, :] = xr
            xim_ref[j, rows, :] = xi
            out.append((ar[j] * xr - ai[j] * xi + ere_ref[j, rows, :],
                        ar[j] * xi + ai[j] * xr + eim_ref[j, rows, :]))
        return tuple(out)

    init = tuple((sre_ref[j], sim_ref[j]) for j in range(n_tiles))
    final = lax.fori_loop(0, ns, step, init)
    for j in range(n_tiles):
        sre_ref[j], sim_ref[j] = final[j]
    x_re = jnp.concatenate([xre_ref[j] for j in range(n_tiles)], axis=-1).astype(BF16)
    x_im = jnp.concatenate([xim_ref[j] for j in range(n_tiles)], axis=-1).astype(BF16)
    y = _dot(u, t1_ref[0]) + _dot(x_re, qre_ref[0]) + _dot(x_im, qim_ref[0])
    for b in range(batch):
        for s in range(cs):
            y_ref[b, pl.ds(s, ns, stride=cs), :] = y[b * ns:(b + 1) * ns, s * LANES:(s + 1) * LANES]


def _s5_operators(a_re, a_im, b_re, b_im, c_re, c_im, log_step):
    cs = S5_CHUNK
    ein = functools.partial(jnp.einsum, precision=HIGHEST)
    delta = jnp.exp(log_step)[:, None]
    ar, ai = a_re, a_im
    mag = jnp.exp(ar * delta)
    ang = ai * delta
    lr, li = mag * jnp.cos(ang), mag * jnp.sin(ang)
    den = ar * ar + ai * ai
    nr, ni = lr - 1.0, li
    cr = (nr * ar + ni * ai) / den
    ci = (ni * ar - nr * ai) / den
    bbr = cr[..., None] * b_re - ci[..., None] * b_im
    bbi = cr[..., None] * b_im + ci[..., None] * b_re
    k = jnp.arange(cs + 1, dtype=F32)
    pmag = jnp.exp((ar * delta)[..., None] * k)
    pang = ang[..., None] * k
    pr, pi = pmag * jnp.cos(pang), pmag * jnp.sin(pang)
    clr = c_re[..., None] * pr[:, None] - c_im[..., None] * pi[:, None]
    cli = c_re[..., None] * pi[:, None] + c_im[..., None] * pr[:, None]
    kern = ein('gcpk,gpd->gkcd', clr, bbr) - ein('gcpk,gpd->gkcd', cli, bbi)
    kern = jnp.concatenate([kern[:, :cs], jnp.zeros_like(kern[:, :1])], axis=1)
    s = jnp.arange(cs)
    lag = jnp.where(s[None, :] >= s[:, None], s[None, :] - s[:, None], cs)
    nh, ng = S5_HALVES, S5_LANE_GROUPS
    eye = jnp.eye(ng, dtype=F32)
    halves = lambda t: t.reshape((nh, ng) + t.shape[1:])
    bd = halves(kern).transpose(0, 2, 1, 4, 3)
    bd = (bd[:, :, :, :, None, :] * eye[None, None, :, None, :, None]).reshape(nh, cs + 1, LANES, LANES)
    t1 = bd.astype(BF16)[:, lag].transpose(0, 1, 3, 2, 4).reshape(nh, cs * LANES, cs * LANES)
    rev = cs - 1 - s
    prr, pri = pr[..., rev], pi[..., rev]
    p_re = prr[..., None] * bbr[:, :, None] - pri[..., None] * bbi[:, :, None]
    p_im = prr[..., None] * bbi[:, :, None] + pri[..., None] * bbr[:, :, None]

    def flat_p(t):
        t = halves(t.transpose(0, 2, 3, 1)).transpose(0, 2, 1, 3, 4)
        t = t[:, :, :, :, None, :] * eye[None, None, :, None, :, None]
        return t.reshape(nh, cs * LANES, ng * S5_STATE).astype(BF16)

    def flat_q(t):
        t = halves(t.transpose(0, 2, 3, 1))
        t = t[:, :, :, :, None, :] * eye[None, :, None, None, :, None]
        return t.reshape(nh, ng * S5_STATE, cs * LANES).astype(BF16)

    a_chunk_re = pr[..., cs].reshape(nh, 1, ng * S5_STATE)
    a_chunk_im = pi[..., cs].reshape(nh, 1, ng * S5_STATE)
    return (t1, flat_p(p_re), flat_p(p_im), flat_q(clr[..., 1:]), flat_q(-cli[..., 1:]),
            a_chunk_re, a_chunk_im)


def _s5(u, ops, batch, seq):
    tb = S5_CHUNK * S5_BLOCK_STEPS
    rows = batch * S5_BLOCK_STEPS
    flat = S5_CHUNK * LANES
    nstate = S5_LANE_GROUPS * S5_STATE
    per_h = lambda a, b: pl.BlockSpec((1, a, b), lambda h, j: (h, 0, 0))
    seq_blk = pl.BlockSpec((batch, tb, LANES), lambda h, j: (0, j, h))
    y = pl.pallas_call(
        _s5_kernel,
        grid=(S5_HALVES, seq // tb),
        in_specs=[seq_blk, per_h(flat, flat), per_h(flat, nstate), per_h(flat, nstate),
                  per_h(nstate, flat), per_h(nstate, flat), per_h(1, nstate), per_h(1, nstate)],
        out_specs=seq_blk,
        out_shape=jax.ShapeDtypeStruct((batch, seq, S5_W), F32),
        scratch_shapes=([pltpu.VMEM((nstate // LANES, rows, LANES), F32)] * 4
                        + [pltpu.VMEM((nstate // LANES, batch, LANES), F32)] * 2),
        compiler_params=_cparams("parallel", "arbitrary"),
        name="s5",
    )(u.reshape(batch, seq, S5_W), *ops)
    return y.reshape(batch * seq, S5_W)


def _out_proj_kernel(h_ref, r_ref, m_ref, ys_ref, u_ref, d_ref, wg_ref, bg_ref, wo_ref, o_ref):
    y = ys_ref[...].astype(F32) + d_ref[...] * u_ref[...].astype(F32)
    g = jax.nn.gelu(y)
    s = g * jax.nn.sigmoid(_dot(g.astype(BF16), wg_ref[...]) + bg_ref[...])
    acc = _dot(r_ref[...], wo_ref[0:RET_W, :])
    acc = acc + _dot(m_ref[...], wo_ref[RET_W:RET_W + SSD_W, :])
    acc = acc + _dot(s.astype(BF16), wo_ref[RET_W + SSD_W:, :])
    o_ref[...] = h_ref[...] + acc


def _out_proj(h, out_r, out_m, y_s, u, d_s5, w_glu, b_glu, w_out):
    t, d = h.shape
    tm = ROW_TILE
    row = lambda w: pl.BlockSpec((tm, w), lambda i: (i, 0))
    full = lambda a, b: pl.BlockSpec((a, b), lambda i: (0, 0))
    return pl.pallas_call(
        _out_proj_kernel,
        grid=(t // tm,),
        in_specs=[row(d), row(RET_W), row(SSD_W), row(S5_W), row(S5_W),
                  full(1, S5_W), full(S5_W, S5_W), full(1, S5_W), full(d, d)],
        out_specs=row(d),
        out_shape=jax.ShapeDtypeStruct((t, d), F32),
        compiler_params=_cparams("parallel"),
        name="out_proj",
    )(h, out_r, out_m, y_s, u, d_s5, w_glu, b_glu, w_out)


def _cross_kernel(h_ref, g_ref, wq_ref, k_ref, v_ref, wo_ref, o_ref):
    h = h_ref[...]
    d = h.shape[-1]
    dh = d // CROSS_HEADS
    q = _dot(_rms(h, g_ref[...]).astype(BF16), wq_ref[...]).astype(BF16)
    outs = []
    for i in range(CROSS_HEADS):
        sl = slice(i * dh, (i + 1) * dh)
        s = _dot_nt(q[:, sl], k_ref[:, sl]) * (dh ** -0.5)
        p = jnp.exp(s - jnp.max(s, axis=-1, keepdims=True))
        o = _dot(p.astype(BF16), v_ref[:, sl])
        outs.append(o / jnp.sum(p, axis=-1, keepdims=True))
    o = jnp.concatenate(outs, axis=-1).astype(BF16)
    o_ref[...] = h + _dot(o, wo_ref[...])


def _cross(h, g, wq, kv, layer, wo, seq, mem_len):
    t, d = h.shape
    tm = ROW_TILE
    tiles_per_seq = seq // tm
    full = lambda a, b: pl.BlockSpec((a, b), lambda i: (0, 0))
    return pl.pallas_call(
        _cross_kernel,
        grid=(t // tm,),
        in_specs=[pl.BlockSpec((tm, d), lambda i: (i, 0)), full(1, d), full(d, d),
                  pl.BlockSpec((mem_len, d), lambda i: (i // tiles_per_seq, 2 * layer)),
                  pl.BlockSpec((mem_len, d), lambda i: (i // tiles_per_seq, 2 * layer + 1)),
                  full(d, d)],
        out_specs=pl.BlockSpec((tm, d), lambda i: (i, 0)),
        out_shape=jax.ShapeDtypeStruct((t, d), F32),
        compiler_params=_cparams("parallel"),
        name="cross_attn",
    )(h, g, wq, kv, kv, wo)


_GROUP_LANE0 = N_EXPERTS


def _router_kernel(h_ref, g_ref, whi_ref, wlo_ref, b_ref, info_ref, infot_ref, cnt_ref, carry_ref):
    @pl.when(pl.program_id(0) == 0)
    def _():
        carry_ref[...] = jnp.zeros_like(carry_ref)

    xn = _rms(h_ref[...], g_ref[...])
    x_hi = xn.astype(BF16)
    x_lo = (xn - x_hi.astype(F32)).astype(BF16)
    logits = (_dot(x_hi, whi_ref[...]) + _dot(x_lo, whi_ref[...]) + _dot(x_hi, wlo_ref[...])
              + b_ref[...])
    tm = logits.shape[0]
    lane = lax.broadcasted_iota(jnp.int32, logits.shape, 1).astype(F32)
    neg = -jnp.inf

    def first_argmax(vals):
        m = jnp.max(vals, axis=-1, keepdims=True)
        return m, jnp.min(jnp.where(vals == m, lane, float(LANES)), axis=-1, keepdims=True)

    gl = jnp.where((lane >= _GROUP_LANE0) & (lane < _GROUP_LANE0 + MOE_GROUPS), logits, neg)
    gmax, glane = first_argmax(gl)
    pg = 1.0 / jnp.sum(jnp.exp(gl - gmax), axis=-1, keepdims=True)
    lo = (glane - _GROUP_LANE0) * EXPERTS_PER_GROUP
    el = jnp.where((lane >= lo) & (lane < lo + EXPERTS_PER_GROUP), logits, neg)
    m1, e1 = first_argmax(el)
    m2, e2 = first_argmax(jnp.where(lane == e1, neg, el))
    p2 = jnp.exp(m2 - m1)
    gate1 = pg / (1.0 + p2)
    gate2 = pg * p2 / (1.0 + p2)

    hot = jnp.where((lane == e1) | (lane == e2), 1.0, 0.0)
    row = lax.broadcasted_iota(jnp.int32, (tm, tm), 0)
    col = lax.broadcasted_iota(jnp.int32, (tm, tm), 1)
    before = jnp.where(row > col, 1.0, 0.0).astype(BF16)
    cum = _dot(before, hot.astype(BF16)) + carry_ref[...]
    rank1 = jnp.sum(jnp.where(lane == e1, cum, 0.0), axis=-1, keepdims=True)
    rank2 = jnp.sum(jnp.where(lane == e2, cum, 0.0), axis=-1, keepdims=True)
    carry_ref[...] = carry_ref[...] + jnp.sum(hot, axis=0, keepdims=True)
    cnt_ref[...] = carry_ref[...]

    info = jnp.zeros(logits.shape, F32)
    for i, val in enumerate((e1, e2, rank1, rank2, gate1, gate2)):
        info = jnp.where(lane == i, val, info)
    info_ref[...] = info
    infot_ref[...] = info.T[0:_INFO_ROWS, :]


_INFO_ROWS = 8


def _router(h, g, w_r, b_r):
    t, d = h.shape
    tm = ROW_TILE
    w_hi = w_r.astype(BF16)
    w_lo = (w_r - w_hi.astype(F32)).astype(BF16)
    return pl.pallas_call(
        _router_kernel,
        grid=(t // tm,),
        in_specs=[pl.BlockSpec((tm, d), lambda i: (i, 0)),
                  pl.BlockSpec((1, d), lambda i: (0, 0)),
                  pl.BlockSpec((d, LANES), lambda i: (0, 0)),
                  pl.BlockSpec((d, LANES), lambda i: (0, 0)),
                  pl.BlockSpec((1, LANES), lambda i: (0, 0))],
        out_specs=[pl.BlockSpec((tm, LANES), lambda i: (i, 0)),
                   pl.BlockSpec((_INFO_ROWS, tm), lambda i: (0, i)),
                   pl.BlockSpec((1, LANES), lambda i: (0, 0))],
        out_shape=[jax.ShapeDtypeStruct((t, LANES), F32), jax.ShapeDtypeStruct((_INFO_ROWS, t), F32),
                   jax.ShapeDtypeStruct((1, LANES), F32)],
        scratch_shapes=[pltpu.VMEM((1, LANES), F32)],
        compiler_params=_cparams("arbitrary"),
        name="moe_router",
    )(h, g, w_hi, w_lo, b_r)


def _row_copy(src_hbm, dst_vmem, src_row, dst_row, sem):
    return pltpu.make_async_copy(src_hbm.at[pl.ds(src_row, 1), :], dst_vmem.at[pl.ds(dst_row, 1), :], sem)


_ISSUE_UNROLL = 8


def _start_row_gather(src_hbm, idx_ref, n_rows, dst, sem):
    def body(j, carry):
        for p in range(2):
            r = 2 * j + p
            _row_copy(src_hbm, dst, idx_ref[0, 0, r], r, sem).start(priority=p)
        return carry

    lax.fori_loop(0, n_rows // 2, body, 0, unroll=_ISSUE_UNROLL)


def _wait_row_gather(src_hbm, n_rows, dst, sem):
    pltpu.make_async_copy(src_hbm.at[pl.ds(0, n_rows), :], dst, sem).wait()


def _expert_kernel(be_ref, nu_ref, cur_ref, nxt_ref, h_hbm, g_ref, wg_ref, wu_ref, wd_ref, y_ref,
                   xbuf, wg_s, wu_s, wd_s, sem):
    i = pl.program_id(0)
    n_used = nu_ref[0]
    slot = i % 2
    used = i < n_used

    @pl.when((i == 0) & used)
    def _():
        _start_row_gather(h_hbm, cur_ref, MOE_BLOCK, xbuf.at[0], sem.at[0])

    @pl.when(i + 1 < n_used)
    def _():
        _start_row_gather(h_hbm, nxt_ref, MOE_BLOCK, xbuf.at[1 - slot], sem.at[1 - slot])

    @pl.when(used & ((i == 0) | (be_ref[i] != be_ref[jnp.maximum(i - 1, 0)])))
    def _():
        wg_s[...] = wg_ref[0, 0].astype(BF16)
        wu_s[...] = wu_ref[0, 0].astype(BF16)
        wd_s[...] = wd_ref[0, 0].astype(BF16)

    @pl.when(used)
    def _():
        _wait_row_gather(h_hbm, MOE_BLOCK, xbuf.at[slot], sem.at[slot])
        xn = _rms(xbuf[slot], g_ref[...]).astype(BF16)
        hid = (_silu(_dot(xn, wg_s[...])) * _dot(xn, wu_s[...])).astype(BF16)
        y_ref[...] = _dot(hid, wd_s[...])

    @pl.when(jnp.logical_not(used))
    def _():
        y_ref[...] = jnp.zeros_like(y_ref)


def _experts(h, g, block_e, n_used, src_tok, w_gate, w_up, w_down, layer):
    t, d = h.shape
    nb = block_e.shape[0]
    de = w_gate.shape[-1]
    idx_blk = lambda f: pl.BlockSpec((1, 1, MOE_BLOCK), lambda i, be, nu: (f(i), 0, 0),
                                     memory_space=pltpu.SMEM)
    grid_spec = pltpu.PrefetchScalarGridSpec(
        num_scalar_prefetch=2,
        grid=(nb,),
        in_specs=[idx_blk(lambda i: i), idx_blk(lambda i: jnp.minimum(i + 1, nb - 1)),
                  pl.BlockSpec(memory_space=pl.ANY),
                  pl.BlockSpec((1, d), lambda i, be, nu: (0, 0)),
                  pl.BlockSpec((1, 1, d, de), lambda i, be, nu: (layer, be[i], 0, 0)),
                  pl.BlockSpec((1, 1, d, de), lambda i, be, nu: (layer, be[i], 0, 0)),
                  pl.BlockSpec((1, 1, de, d), lambda i, be, nu: (layer, be[i], 0, 0))],
        out_specs=pl.BlockSpec((MOE_BLOCK, d), lambda i, be, nu: (i, 0)),
        scratch_shapes=[pltpu.VMEM((2, MOE_BLOCK, d), F32), pltpu.VMEM((d, de), BF16),
                        pltpu.VMEM((d, de), BF16), pltpu.VMEM((de, d), BF16),
                        pltpu.SemaphoreType.DMA((2,))],
    )
    src3 = src_tok.reshape(nb, 1, MOE_BLOCK)
    return pl.pallas_call(
        _expert_kernel,
        grid_spec=grid_spec,
        out_shape=jax.ShapeDtypeStruct((nb * MOE_BLOCK, d), F32),
        compiler_params=_cparams("arbitrary"),
        name="moe_experts",
    )(block_e, n_used, src3, src3, h, g, w_gate, w_up, w_down)


def _combine_kernel(dest_ref, h_ref, info_ref, y_hbm, fg_ref, o_ref, ybuf, sem, *, final_norm):
    tm = h_ref.shape[0]

    def start(r, carry):
        for k in range(2):
            _row_copy(y_hbm, ybuf.at[k], dest_ref[0, 0, k * tm + r], r, sem.at[k]).start(priority=k)
        return carry

    lax.fori_loop(0, tm, start, 0, unroll=_ISSUE_UNROLL)
    for k in range(2):
        _wait_row_gather(y_hbm, tm, ybuf.at[k], sem.at[k])
    info = info_ref[...]
    out = h_ref[...] + (info[:, 4:5] * ybuf[0] + info[:, 5:6] * ybuf[1])
    if final_norm:
        out = _rms(out, fg_ref[...])
    o_ref[...] = out


def _combine(h, info, dest, ybuf, final_g, final_norm):
    t, d = h.shape
    tm = MOE_ROW_TILE
    return pl.pallas_call(
        functools.partial(_combine_kernel, final_norm=final_norm),
        grid=(t // tm,),
        in_specs=[pl.BlockSpec((1, 1, 2 * tm), lambda i: (i, 0, 0), memory_space=pltpu.SMEM),
                  pl.BlockSpec((tm, d), lambda i: (i, 0)),
                  pl.BlockSpec((tm, LANES), lambda i: (i, 0)),
                  pl.BlockSpec(memory_space=pl.ANY),
                  pl.BlockSpec((1, d), lambda i: (0, 0))],
        out_specs=pl.BlockSpec((tm, d), lambda i: (i, 0)),
        out_shape=jax.ShapeDtypeStruct((t, d), F32),
        scratch_shapes=[pltpu.VMEM((2, tm, d), F32), pltpu.SemaphoreType.DMA((2,))],
        compiler_params=_cparams("arbitrary"),
        name="moe_combine",
    )(dest.reshape(2, t // tm, tm).transpose(1, 0, 2).reshape(t // tm, 1, 2 * tm), h, info, ybuf, final_g)


def _moe(h, g, w_rg, b_rg, w_re, b_re, w_gate, w_up, w_down, layer, final_g, final_norm):
    t, d = h.shape
    pad_cols = LANES - N_EXPERTS - MOE_GROUPS
    w_r = jnp.concatenate([w_re, w_rg, jnp.zeros((d, pad_cols), F32)], axis=1)
    b_r = jnp.concatenate([b_re, b_rg, jnp.zeros((pad_cols,), F32)])[None, :]
    info, infot, cnt = _router(h, g, w_r, b_r)
    expert = infot[0:2].astype(jnp.int32)
    rank = infot[2:4].astype(jnp.int32)
    counts = cnt[0, :N_EXPERTS].astype(jnp.int32)
    padded = (counts + MOE_BLOCK - 1) // MOE_BLOCK * MOE_BLOCK
    pends = jnp.cumsum(padded)
    pstarts = pends - padded
    starts = jnp.cumsum(counts) - counts
    ids = jnp.arange(N_EXPERTS, dtype=jnp.int32)
    dest = jnp.sum(jnp.where(expert[..., None] == ids, pstarts, 0), axis=-1) + rank
    nb = (2 * t) // MOE_BLOCK + N_EXPERTS
    blk = jnp.arange(nb, dtype=jnp.int32)
    block_e = jnp.minimum(jnp.sum(pends[None, :] <= blk[:, None] * MOE_BLOCK, axis=1), N_EXPERTS - 1)
    block_e = block_e.astype(jnp.int32)
    n_used = (pends[-1:] // MOE_BLOCK).astype(jnp.int32)
    keys = expert * t + jnp.arange(t, dtype=jnp.int32)[None, :]
    sorted_tok = jnp.sort(keys.reshape(-1)) % t
    shift = jnp.sum(jnp.where(block_e[:, None] == ids, pstarts - starts, 0), axis=-1)
    pair = blk[:, None] * MOE_BLOCK + jnp.arange(MOE_BLOCK, dtype=jnp.int32)[None, :] - shift[:, None]
    src_tok = sorted_tok[jnp.clip(pair, 0, 2 * t - 1)]
    ybuf = _experts(h, g, block_e, n_used, src_tok, w_gate, w_up, w_down, layer)
    return _combine(h, info, dest, ybuf, final_g, final_norm)


def kernel(x, mem, norm_mix_g, w_in, ret_gn_g, ssd_conv_w, ssd_conv_b, ssd_dt_bias, ssd_A_log, ssd_D,
           ssd_norm_g, s5_A_re, s5_A_im, s5_B_re, s5_B_im, s5_C_re, s5_C_im, s5_log_step, s5_D, s5_w_glu,
           s5_b_glu, w_out, norm_cross_g, mem_norm_g, w_cq, w_ck, w_cv, w_co, norm_ffn_g, w_route_group,
           b_route_group, w_route_expert, b_route_expert, w_gate, w_up, w_down, norm_final_g):
    batch, seq, d = x.shape
    depth = w_in.shape[0]
    mem_len = mem.shape[1]
    t = batch * seq
    assert d == RET_W * 4 and t % ROW_TILE == 0 and seq % ROW_TILE == 0
    assert seq % RET_CHUNK == 0 and seq % SSD_CHUNK == 0 and seq % (S5_CHUNK * S5_BLOCK_STEPS) == 0
    assert t % MOE_BLOCK == 0 and t % MOE_ROW_TILE == 0
    row = lambda v: v[None, :]
    h = x.reshape(t, d)

    w_kv = jnp.concatenate([w for i in range(depth) for w in (w_ck[i], w_cv[i])], axis=1).astype(BF16)
    kv = _norm_matmul(mem.reshape(batch * mem_len, d), row(mem_norm_g), w_kv, BF16,
                      tm=mem_len, tn=d)
    ret_tables = _retention_tables(seq)
    c0 = _QKVG_W + SSD_W + SSD_XBC_W
    c1 = c0 + SSD_HEADS

    for i in range(depth):
        w_pack = jnp.concatenate(
            [w_in[i][:, :c0], w_in[i][:, c1:], w_in[i][:, c0:c1], jnp.zeros((d, LANES - SSD_HEADS), F32)],
            axis=1).astype(BF16)
        qkvg, z, xbc, u, dt = _in_proj(h, row(norm_mix_g[i]), w_pack)
        out_r = _retention(qkvg, ret_tables, row(ret_gn_g[i]), batch, seq)
        out_m = _ssd(z, xbc, dt, ssd_conv_w[i], ssd_conv_b[i], ssd_dt_bias[i], ssd_A_log[i], ssd_D[i],
                     ssd_norm_g[i], batch, seq)
        s5_ops = _s5_operators(s5_A_re[i], s5_A_im[i], s5_B_re[i], s5_B_im[i], s5_C_re[i], s5_C_im[i],
                               s5_log_step[i])
        y_s = _s5(u, s5_ops, batch, seq)
        h = _out_proj(h, out_r, out_m, y_s, u, row(s5_D[i]), s5_w_glu[i].astype(BF16), row(s5_b_glu[i]),
                      w_out[i].astype(BF16))
        h = _cross(h, row(norm_cross_g[i]), w_cq[i].astype(BF16), kv, i, w_co[i].astype(BF16), seq, mem_len)
        h = _moe(h, row(norm_ffn_g[i]), w_route_group[i], b_route_group[i], w_route_expert[i],
                 b_route_expert[i], w_gate, w_up, w_down, i, row(norm_final_g),
                 final_norm=(i == depth - 1))
    return h.reshape(batch, seq, d)
```

```python
import functools
import math

import jax
import jax.numpy as jnp
from jax import lax
from jax.experimental import pallas as pl
from jax.experimental.pallas import tpu as pltpu

F32 = jnp.float32
BF16 = jnp.bfloat16
HIGHEST = lax.Precision.HIGHEST

EPS = 1e-6
RET_HEADS = 4
RET_HEAD_DIM = 64
RET_W = RET_HEADS * RET_HEAD_DIM
ROPE_BASE = 10000.0
SSD_HEAD_DIM = 64
SSD_HEADS = 8
SSD_GROUPS = 2
SSD_STATE = 128
SSD_CONV = 4
SSD_W = SSD_HEADS * SSD_HEAD_DIM
SSD_XBC_W = SSD_W + 2 * SSD_GROUPS * SSD_STATE
S5_GROUP = 16
S5_GROUPS = 16
S5_STATE = 64
S5_W = S5_GROUP * S5_GROUPS
CROSS_HEADS = 4
MOE_GROUPS = 4
EXPERTS_PER_GROUP = 8
N_EXPERTS = MOE_GROUPS * EXPERTS_PER_GROUP

LANES = 128
ROW_TILE = 512
RET_CHUNK = 256
SSD_CHUNK = 128
S5_CHUNK = 8
S5_LANE_GROUPS = LANES // S5_GROUP
S5_HALVES = S5_W // LANES
S5_BLOCK_STEPS = 64
MOE_BLOCK = 256
VMEM_LIMIT = 48 * 1024 * 1024


def _cparams(*sem):
    return pltpu.CompilerParams(dimension_semantics=sem, vmem_limit_bytes=VMEM_LIMIT)


def _rms(x, g):
    return x * lax.rsqrt(jnp.mean(x * x, axis=-1, keepdims=True) + EPS) * g


def _silu(x):
    return x * jax.nn.sigmoid(x)


def _dot(a, b):
    return jnp.dot(a, b, preferred_element_type=F32)


def _dot_nt(a, b):
    return lax.dot_general(a, b, (((1,), (1,)), ((), ())), preferred_element_type=F32)


def _dot_tn(a, b):
    return lax.dot_general(a, b, (((0,), (0,)), ((), ())), preferred_element_type=F32)


def _norm_matmul_kernel(x_ref, g_ref, w_ref, o_ref):
    xn = _rms(x_ref[...], g_ref[...]).astype(BF16)
    o_ref[...] = _dot(xn, w_ref[...]).astype(o_ref.dtype)


def _norm_matmul(x, g, w, out_dtype, tm, tn):
    m, d = x.shape
    n = w.shape[1]
    return pl.pallas_call(
        _norm_matmul_kernel,
        grid=(m // tm, n // tn),
        in_specs=[pl.BlockSpec((tm, d), lambda i, j: (i, 0)),
                  pl.BlockSpec((1, d), lambda i, j: (0, 0)),
                  pl.BlockSpec((d, tn), lambda i, j: (0, j))],
        out_specs=pl.BlockSpec((tm, tn), lambda i, j: (i, j)),
        out_shape=jax.ShapeDtypeStruct((m, n), out_dtype),
        compiler_params=_cparams("parallel", "parallel"),
        name="norm_matmul",
    )(x, g, w)


_QKVG_W = 4 * RET_W
_IN_SPLITS = (_QKVG_W, SSD_W, SSD_XBC_W, S5_W, LANES)


def _in_proj_kernel(h_ref, g_ref, w_ref, qkvg_ref, z_ref, xbc_ref, u_ref, dt_ref):
    xn = _rms(h_ref[...], g_ref[...]).astype(BF16)
    lo = 0
    for ref, width in zip((qkvg_ref, z_ref, xbc_ref, u_ref, dt_ref), _IN_SPLITS):
        ref[...] = _dot(xn, w_ref[:, lo:lo + width]).astype(ref.dtype)
        lo += width


def _in_proj(h, g, w_pack):
    t, d = h.shape
    tm = ROW_TILE
    n = w_pack.shape[1]
    dts = (BF16, BF16, BF16, F32, F32)
    return pl.pallas_call(
        _in_proj_kernel,
        grid=(t // tm,),
        in_specs=[pl.BlockSpec((tm, d), lambda i: (i, 0)),
                  pl.BlockSpec((1, d), lambda i: (0, 0)),
                  pl.BlockSpec((d, n), lambda i: (0, 0))],
        out_specs=[pl.BlockSpec((tm, w), lambda i: (i, 0)) for w in _IN_SPLITS],
        out_shape=[jax.ShapeDtypeStruct((t, w), dt) for w, dt in zip(_IN_SPLITS, dts)],
        compiler_params=_cparams("parallel"),
        name="in_proj",
    )(h, g, w_pack)


def _retention_kernel(qkvg_ref, cos_ref, sin_ref, decay_ref, qdec_ref, kdec_ref, cdec_ref, gn_ref,
                      out_ref, s_ref):
    @pl.when(pl.program_id(1) == 0)
    def _():
        s_ref[...] = jnp.zeros_like(s_ref)

    x = qkvg_ref[...]
    w = RET_W
    q = x[:, 0:w].astype(F32)
    k = x[:, w:2 * w].astype(F32)
    v = x[:, 2 * w:3 * w]
    g = x[:, 3 * w:4 * w].astype(F32)
    half = RET_HEAD_DIM // 2
    lane = lax.broadcasted_iota(jnp.int32, q.shape, 1)
    first_half = (lane % RET_HEAD_DIM) < half

    def rot(t):
        swapped = jnp.where(first_half, pltpu.roll(t, w - half, 1), pltpu.roll(t, half, 1))
        return t * cos_ref[...] + swapped * sin_ref[...]

    qr = rot(q)
    kr = rot(k) * (RET_HEAD_DIM ** -0.5)
    qb = qr.astype(BF16)
    kb = kr.astype(BF16)
    qd = (qr * qdec_ref[...]).astype(BF16)
    kd = (kr * kdec_ref[...]).astype(BF16)
    outs = []
    for h in range(RET_HEADS):
        sl = slice(h * RET_HEAD_DIM, (h + 1) * RET_HEAD_DIM)
        s = _dot_nt(qb[:, sl], kb[:, sl]) * decay_ref[h]
        state = s_ref[h]
        y = _dot(s.astype(BF16), v[:, sl]) + _dot(qd[:, sl], state.astype(BF16))
        s_ref[h] = state * cdec_ref[h] + _dot_tn(kd[:, sl], v[:, sl])
        outs.append(y * lax.rsqrt(jnp.mean(y * y, axis=-1, keepdims=True) + EPS))
    yr = jnp.concatenate(outs, axis=-1)
    out_ref[...] = (_silu(g) * (yr * gn_ref[...])).astype(out_ref.dtype)


def _retention_tables(seq):
    c = RET_CHUNK
    dh = RET_HEAD_DIM
    inv = ROPE_BASE ** (-jnp.arange(0, dh, 2, dtype=F32) / dh)
    ang = jnp.arange(seq, dtype=F32)[:, None] * inv[None, :]
    cos, sin = jnp.cos(ang), jnp.sin(ang)
    cos4 = jnp.tile(jnp.concatenate([cos, cos], axis=-1), (1, RET_HEADS))
    sin4 = jnp.tile(jnp.concatenate([-sin, sin], axis=-1), (1, RET_HEADS))
    lg = jnp.log1p(-(2.0 ** (-5.0 - jnp.arange(RET_HEADS, dtype=F32))))
    i = jnp.arange(c, dtype=F32)
    rel = i[:, None] - i[None, :]
    decay = jnp.where(rel[None] >= 0, jnp.exp(lg[:, None, None] * jnp.maximum(rel, 0.0)[None]), 0.0)
    per_head = lambda t: jnp.repeat(t.T, dh, axis=1)
    qdec = per_head(jnp.exp(lg[:, None] * (i + 1.0)[None]))
    kdec = per_head(jnp.exp(lg[:, None] * (c - 1.0 - i)[None]))
    cdec = jnp.broadcast_to(jnp.exp(lg * c)[:, None, None], (RET_HEADS, dh, dh))
    return cos4, sin4, decay, qdec, kdec, cdec


def _retention(qkvg, tables, gn, batch, seq):
    c = RET_CHUNK
    nc = seq // c
    cos4, sin4, decay, qdec, kdec, cdec = tables
    w = RET_W
    full = lambda shape: pl.BlockSpec(shape, lambda b, j: (0,) * len(shape))
    return pl.pallas_call(
        _retention_kernel,
        grid=(batch, nc),
        in_specs=[pl.BlockSpec((c, _QKVG_W), lambda b, j: (b * nc + j, 0)),
                  pl.BlockSpec((c, w), lambda b, j: (j, 0)),
                  pl.BlockSpec((c, w), lambda b, j: (j, 0)),
                  full((RET_HEADS, c, c)), full((c, w)), full((c, w)),
                  full((RET_HEADS, RET_HEAD_DIM, RET_HEAD_DIM)), full((1, w))],
        out_specs=pl.BlockSpec((c, w), lambda b, j: (b * nc + j, 0)),
        out_shape=jax.ShapeDtypeStruct((batch * seq, w), BF16),
        scratch_shapes=[pltpu.VMEM((RET_HEADS, RET_HEAD_DIM, RET_HEAD_DIM), F32)],
        compiler_params=_cparams("parallel", "arbitrary"),
        name="retention",
    )(qkvg, cos4, sin4, decay, qdec, kdec, cdec, gn)


def _split3(v):
    hi = v.astype(BF16)
    rest = v - hi.astype(F32)
    mid = rest.astype(BF16)
    return hi, mid, (rest - mid.astype(F32)).astype(BF16)


def _ssd_kernel(z_ref, xbc_ref, dt_ref, shift_ref, expand_ref, cw_ref, cb_ref, dtb_ref, a_ref, d_ref,
                ng_ref, out_ref, xcat_ref, s_ref):
    c = SSD_CHUNK
    p = SSD_HEAD_DIM
    n = SSD_STATE

    @pl.when(pl.program_id(1) == 0)
    def _():
        xcat_ref[0:c, :] = jnp.zeros((c, SSD_XBC_W), BF16)
        s_ref[...] = jnp.zeros_like(s_ref)

    x_cur = xbc_ref[...]
    xcat_ref[c:2 * c, :] = x_cur
    x_cat = xcat_ref[...]
    conv = cb_ref[...] + x_cur.astype(F32) * cw_ref[SSD_CONV - 1:SSD_CONV, :]
    for j in range(SSD_CONV - 1):
        conv = conv + _dot(shift_ref[j], x_cat) * cw_ref[j:j + 1, :]
    xcat_ref[0:c, :] = x_cur
    act = _silu(conv)
    xs = act[:, :SSD_W]
    bm = act[:, SSD_W:SSD_W + SSD_GROUPS * n]
    cm = act[:, SSD_W + SSD_GROUPS * n:]

    dt_in = dt_ref[...] + dtb_ref[...]
    dt = jnp.maximum(dt_in, 0.0) + jnp.log1p(jnp.exp(-jnp.abs(dt_in)))
    row = lax.broadcasted_iota(jnp.int32, (c, c), 0)
    col = lax.broadcasted_iota(jnp.int32, (c, c), 1)
    causal = row >= col
    a_cum = jnp.dot(causal.astype(F32), dt * a_ref[...], precision=HIGHEST,
                    preferred_element_type=F32)
    a_cum_t = a_cum.T

    expand = expand_ref[...]
    widen = lambda v: sum(_dot(term, expand) for term in _split3(v))
    dt_w = widen(dt)
    a_w = widen(a_cum)
    a_last_w = a_w[c - 1:c, :]
    chunk_dec_w = jnp.exp(a_last_w)
    xdt = xs * dt_w
    xdt_b = xdt.astype(BF16)
    xdec_b = (xdt * jnp.exp(a_last_w - a_w)).astype(BF16)

    heads_per_group = SSD_HEADS // SSD_GROUPS
    gw = heads_per_group * p
    y_diag, y_off = [], []
    for g in range(SSD_GROUPS):
        bg = bm[:, g * n:(g + 1) * n].astype(BF16)
        cg = cm[:, g * n:(g + 1) * n].astype(BF16)
        cb = _dot_nt(cg, bg)
        gl = slice(g * gw, (g + 1) * gw)
        states = s_ref[g]
        y_off.append(_dot(cg, states.astype(BF16)))
        s_ref[g] = states * chunk_dec_w[:, gl] + _dot_tn(bg, xdec_b[:, gl])
        for r in range(heads_per_group):
            h = g * heads_per_group + r
            lmat = jnp.exp(jnp.where(causal, a_cum[:, h:h + 1] - a_cum_t[h:h + 1, :], -jnp.inf))
            y_diag.append(_dot((cb * lmat).astype(BF16), xdt_b[:, h * p:(h + 1) * p]))
    y = (jnp.concatenate(y_diag, axis=-1) + jnp.concatenate(y_off, axis=-1) * jnp.exp(a_w)
         + xs * d_ref[...])
    out_ref[...] = _rms(y * _silu(z_ref[...].astype(F32)), ng_ref[...]).astype(out_ref.dtype)


def _ssd(z, xbc, dt, conv_w, conv_b, dt_bias, a_log, d_skip, norm_g, batch, seq):
    c = SSD_CHUNK
    nc = seq // c
    pad = lambda v: jnp.pad(v, (0, LANES - v.shape[0]))[None, :]
    a_neg = pad(-jnp.exp(a_log))
    d_wide = jnp.repeat(d_skip, SSD_HEAD_DIM)[None, :]
    t_idx = jnp.arange(c)[None, :, None]
    lag = (SSD_CONV - 1 - jnp.arange(SSD_CONV - 1))[:, None, None]
    shift = (jnp.arange(2 * c)[None, None, :] == c + t_idx - lag).astype(BF16)
    expand = (jnp.arange(LANES)[:, None] == jnp.arange(SSD_W)[None, :] // SSD_HEAD_DIM).astype(BF16)
    full = lambda shape: pl.BlockSpec(shape, lambda b, j: (0,) * len(shape))
    blk = lambda w: pl.BlockSpec((c, w), lambda b, j: (b * nc + j, 0))
    return pl.pallas_call(
        _ssd_kernel,
        grid=(batch, nc),
        in_specs=[blk(SSD_W), blk(SSD_XBC_W), blk(LANES), full((SSD_CONV - 1, c, 2 * c)),
                  full((LANES, SSD_W)), full((SSD_CONV, SSD_XBC_W)), full((1, SSD_XBC_W)),
                  full((1, LANES)), full((1, LANES)), full((1, SSD_W)), full((1, SSD_W))],
        out_specs=blk(SSD_W),
        out_shape=jax.ShapeDtypeStruct((batch * seq, SSD_W), BF16),
        scratch_shapes=[pltpu.VMEM((2 * c, SSD_XBC_W), BF16),
                        pltpu.VMEM((SSD_GROUPS, SSD_STATE, SSD_W // SSD_GROUPS), F32)],
        compiler_params=_cparams("parallel", "arbitrary"),
        name="ssd",
    )(z, xbc, dt, shift, expand, conv_w, conv_b[None, :], pad(dt_bias), a_neg, d_wide, norm_g[None, :])


def _s5_kernel(u_ref, t1_ref, pre_ref, pim_ref, qre_ref, qim_ref, are_ref, aim_ref, y_ref,
               ere_ref, eim_ref, xre_ref, xim_ref, sre_ref, sim_ref):
    batch, tb, _ = u_ref.shape
    cs = S5_CHUNK
    ns = tb // cs

    @pl.when(pl.program_id(1) == 0)
    def _():
        sre_ref[...] = jnp.zeros_like(sre_ref)
        sim_ref[...] = jnp.zeros_like(sim_ref)

    u = jnp.concatenate(
        [jnp.concatenate([u_ref[b, pl.ds(s, ns, stride=cs), :] for s in range(cs)], axis=-1)
         for b in range(batch)], axis=0).astype(BF16)
    n_tiles = ere_ref.shape[0]
    lanes_of = lambda j: slice(j * LANES, (j + 1) * LANES)
    e_re = _dot(u, pre_ref[0])
    e_im = _dot(u, pim_ref[0])
    for j in range(n_tiles):
        ere_ref[j] = e_re[:, lanes_of(j)]
        eim_ref[j] = e_im[:, lanes_of(j)]
    shape = (batch, LANES)
    ar = [jnp.broadcast_to(are_ref[0, :, lanes_of(j)], shape) for j in range(n_tiles)]
    ai = [jnp.broadcast_to(aim_ref[0, :, lanes_of(j)], shape) for j in range(n_tiles)]

    def step(n, carry):
        rows = pl.ds(n, batch, stride=ns)
        out = []
        for j in range(n_tiles):
            xr, xi = carry[j]
            xre_ref[j, rows, :] = xr
            xim_ref[j, rows, :] = xi
            out.append((ar[j] * xr - ai[j] * xi + ere_ref[j, rows, :],
                        ar[j] * xi + ai[j] * xr + eim_ref[j, rows, :]))
        return tuple(out)

    init = tuple((sre_ref[j], sim_ref[j]) for j in range(n_tiles))
    final = lax.fori_loop(0, ns, step, init)
    for j in range(n_tiles):
        sre_ref[j], sim_ref[j] = final[j]
    x_re = jnp.concatenate([xre_ref[j] for j in range(n_tiles)], axis=-1).astype(BF16)
    x_im = jnp.concatenate([xim_ref[j] for j in range(n_tiles)], axis=-1).astype(BF16)
    y = _dot(u, t1_ref[0]) + _dot(x_re, qre_ref[0]) + _dot(x_im, qim_ref[0])
    for b in range(batch):
        for s in range(cs):
            y_ref[b, pl.ds(s, ns, stride=cs), :] = y[b * ns:(b + 1) * ns, s * LANES:(s + 1) * LANES]


def _s5_operators(a_re, a_im, b_re, b_im, c_re, c_im, log_step):
    cs = S5_CHUNK
    ein = functools.partial(jnp.einsum, precision=HIGHEST)
    delta = jnp.exp(log_step)[:, None]
    ar, ai = a_re, a_im
    mag = jnp.exp(ar * delta)
    ang = ai * delta
    lr, li = mag * jnp.cos(ang), mag * jnp.sin(ang)
    den = ar * ar + ai * ai
    nr, ni = lr - 1.0, li
    cr = (nr * ar + ni * ai) / den
    ci = (ni * ar - nr * ai) / den
    bbr = cr[..., None] * b_re - ci[..., None] * b_im
    bbi = cr[..., None] * b_im + ci[..., None] * b_re
    k = jnp.arange(cs + 1, dtype=F32)
    pmag = jnp.exp((ar * delta)[..., None] * k)
    pang = ang[..., None] * k
    pr, pi = pmag * jnp.cos(pang), pmag * jnp.sin(pang)
    clr = c_re[..., None] * pr[:, None] - c_im[..., None] * pi[:, None]
    cli = c_re[..., None] * pi[:, None] + c_im[..., None] * pr[:, None]
    kern = ein('gcpk,gpd->gkcd', clr, bbr) - ein('gcpk,gpd->gkcd', cli, bbi)
    kern = jnp.concatenate([kern[:, :cs], jnp.zeros_like(kern[:, :1])], axis=1)
    s = jnp.arange(cs)
    lag = jnp.where(s[None, :] >= s[:, None], s[None, :] - s[:, None], cs)
    nh, ng = S5_HALVES, S5_LANE_GROUPS
    eye = jnp.eye(ng, dtype=F32)
    halves = lambda t: t.reshape((nh, ng) + t.shape[1:])
    bd = halves(kern).transpose(0, 2, 1, 4, 3)
    bd = (bd[:, :, :, :, None, :] * eye[None, None, :, None, :, None]).reshape(nh, cs + 1, LANES, LANES)
    t1 = bd.astype(BF16)[:, lag].transpose(0, 1, 3, 2, 4).reshape(nh, cs * LANES, cs * LANES)
    rev = cs - 1 - s
    prr, pri = pr[..., rev], pi[..., rev]
    p_re = prr[..., None] * bbr[:, :, None] - pri[..., None] * bbi[:, :, None]
    p_im = prr[..., None] * bbi[:, :, None] + pri[..., None] * bbr[:, :, None]

    def flat_p(t):
        t = halves(t.transpose(0, 2, 3, 1)).transpose(0, 2, 1, 3, 4)
        t = t[:, :, :, :, None, :] * eye[None, None, :, None, :, None]
        return t.reshape(nh, cs * LANES, ng * S5_STATE).astype(BF16)

    def flat_q(t):
        t = halves(t.transpose(0, 2, 3, 1))
        t = t[:, :, :, :, None, :] * eye[None, :, None, None, :, None]
        return t.reshape(nh, ng * S5_STATE, cs * LANES).astype(BF16)

    a_chunk_re = pr[..., cs].reshape(nh, 1, ng * S5_STATE)
    a_chunk_im = pi[..., cs].reshape(nh, 1, ng * S5_STATE)
    return (t1, flat_p(p_re), flat_p(p_im), flat_q(clr[..., 1:]), flat_q(-cli[..., 1:]),
            a_chunk_re, a_chunk_im)


def _s5(u, ops, batch, seq):
    tb = S5_CHUNK * S5_BLOCK_STEPS
    rows = batch * S5_BLOCK_STEPS
    flat = S5_CHUNK * LANES
    nstate = S5_LANE_GROUPS * S5_STATE
    per_h = lambda a, b: pl.BlockSpec((1, a, b), lambda h, j: (h, 0, 0))
    seq_blk = pl.BlockSpec((batch, tb, LANES), lambda h, j: (0, j, h))
    y = pl.pallas_call(
        _s5_kernel,
        grid=(S5_HALVES, seq // tb),
        in_specs=[seq_blk, per_h(flat, flat), per_h(flat, nstate), per_h(flat, nstate),
                  per_h(nstate, flat), per_h(nstate, flat), per_h(1, nstate), per_h(1, nstate)],
        out_specs=seq_blk,
        out_shape=jax.ShapeDtypeStruct((batch, seq, S5_W), F32),
        scratch_shapes=([pltpu.VMEM((nstate // LANES, rows, LANES), F32)] * 4
                        + [pltpu.VMEM((nstate // LANES, batch, LANES), F32)] * 2),
        compiler_params=_cparams("parallel", "arbitrary"),
        name="s5",
    )(u.reshape(batch, seq, S5_W), *ops)
    return y.reshape(batch * seq, S5_W)


def _out_proj_kernel(h_ref, r_ref, m_ref, ys_ref, u_ref, d_ref, wg_ref, bg_ref, wo_ref, o_ref):
    y = ys_ref[...].astype(F32) + d_ref[...] * u_ref[...].astype(F32)
    g = jax.nn.gelu(y)
    s = g * jax.nn.sigmoid(_dot(g.astype(BF16), wg_ref[...]) + bg_ref[...])
    acc = _dot(r_ref[...], wo_ref[0:RET_W, :])
    acc = acc + _dot(m_ref[...], wo_ref[RET_W:RET_W + SSD_W, :])
    acc = acc + _dot(s.astype(BF16), wo_ref[RET_W + SSD_W:, :])
    o_ref[...] = h_ref[...] + acc


def _out_proj(h, out_r, out_m, y_s, u, d_s5, w_glu, b_glu, w_out):
    t, d = h.shape
    tm = ROW_TILE
    row = lambda w: pl.BlockSpec((tm, w), lambda i: (i, 0))
    full = lambda a, b: pl.BlockSpec((a, b), lambda i: (0, 0))
    return pl.pallas_call(
        _out_proj_kernel,
        grid=(t // tm,),
        in_specs=[row(d), row(RET_W), row(SSD_W), row(S5_W), row(S5_W),
                  full(1, S5_W), full(S5_W, S5_W), full(1, S5_W), full(d, d)],
        out_specs=row(d),
        out_shape=jax.ShapeDtypeStruct((t, d), F32),
        compiler_params=_cparams("parallel"),
        name="out_proj",
    )(h, out_r, out_m, y_s, u, d_s5, w_glu, b_glu, w_out)


def _cross_kernel(h_ref, g_ref, wq_ref, k_ref, v_ref, wo_ref, o_ref):
    h = h_ref[...]
    d = h.shape[-1]
    dh = d // CROSS_HEADS
    q = _dot(_rms(h, g_ref[...]).astype(BF16), wq_ref[...]).astype(BF16)
    outs = []
    for i in range(CROSS_HEADS):
        sl = slice(i * dh, (i + 1) * dh)
        s = _dot_nt(q[:, sl], k_ref[:, sl]) * (dh ** -0.5)
        p = jnp.exp(s - jnp.max(s, axis=-1, keepdims=True))
        o = _dot(p.astype(BF16), v_ref[:, sl])
        outs.append(o / jnp.sum(p, axis=-1, keepdims=True))
    o = jnp.concatenate(outs, axis=-1).astype(BF16)
    o_ref[...] = h + _dot(o, wo_ref[...])


def _cross(h, g, wq, kv, layer, wo, seq, mem_len):
    t, d = h.shape
    tm = ROW_TILE
    tiles_per_seq = seq // tm
    full = lambda a, b: pl.BlockSpec((a, b), lambda i: (0, 0))
    return pl.pallas_call(
        _cross_kernel,
        grid=(t // tm,),
        in_specs=[pl.BlockSpec((tm, d), lambda i: (i, 0)), full(1, d), full(d, d),
                  pl.BlockSpec((mem_len, d), lambda i: (i // tiles_per_seq, 2 * layer)),
                  pl.BlockSpec((mem_len, d), lambda i: (i // tiles_per_seq, 2 * layer + 1)),
                  full(d, d)],
        out_specs=pl.BlockSpec((tm, d), lambda i: (i, 0)),
        out_shape=jax.ShapeDtypeStruct((t, d), F32),
        compiler_params=_cparams("parallel"),
        name="cross_attn",
    )(h, g, wq, kv, kv, wo)


_GROUP_LANE0 = N_EXPERTS


def _router_kernel(h_ref, g_ref, whi_ref, wlo_ref, b_ref, info_ref, infot_ref, cnt_ref, carry_ref):
    @pl.when(pl.program_id(0) == 0)
    def _():
        carry_ref[...] = jnp.zeros_like(carry_ref)

    xn = _rms(h_ref[...], g_ref[...])
    x_hi = xn.astype(BF16)
    x_lo = (xn - x_hi.astype(F32)).astype(BF16)
    logits = (_dot(x_hi, whi_ref[...]) + _dot(x_lo, whi_ref[...]) + _dot(x_hi, wlo_ref[...])
              + b_ref[...])
    lane = lax.broadcasted_iota(jnp.int32, logits.shape, 1).astype(F32)
    neg = -jnp.inf

    def first_argmax(vals):
        m = jnp.max(vals, axis=-1, keepdims=True)
        return m, jnp.min(jnp.where(vals == m, lane, float(LANES)), axis=-1, keepdims=True)

    gl = jnp.where((lane >= _GROUP_LANE0) & (lane < _GROUP_LANE0 + MOE_GROUPS), logits, neg)
    gmax, glane = first_argmax(gl)
    pg = 1.0 / jnp.sum(jnp.exp(gl - gmax), axis=-1, keepdims=True)
    lo = (glane - _GROUP_LANE0) * EXPERTS_PER_GROUP
    el = jnp.where((lane >= lo) & (lane < lo + EXPERTS_PER_GROUP), logits, neg)
    m1, e1 = first_argmax(el)
    m2, e2 = first_argmax(jnp.where(lane == e1, neg, el))
    p2 = jnp.exp(m2 - m1)
    gate1 = pg / (1.0 + p2)
    gate2 = pg * p2 / (1.0 + p2)

    hot = jnp.where((lane == e1) | (lane == e2), 1.0, 0.0)
    carry_ref[...] = carry_ref[...] + jnp.sum(hot, axis=0, keepdims=True)
    cnt_ref[...] = carry_ref[...]

    info = jnp.zeros(logits.shape, F32)
    for i, val in enumerate((e1, e2, gate1, gate2)):
        info = jnp.where(lane == i, val, info)
    info_ref[...] = info
    infot_ref[...] = info.T[0:_INFO_ROWS, :]


_INFO_ROWS = 8


def _router(h, g, w_r, b_r):
    t, d = h.shape
    tm = ROW_TILE
    w_hi = w_r.astype(BF16)
    w_lo = (w_r - w_hi.astype(F32)).astype(BF16)
    return pl.pallas_call(
        _router_kernel,
        grid=(t // tm,),
        in_specs=[pl.BlockSpec((tm, d), lambda i: (i, 0)),
                  pl.BlockSpec((1, d), lambda i: (0, 0)),
                  pl.BlockSpec((d, LANES), lambda i: (0, 0)),
                  pl.BlockSpec((d, LANES), lambda i: (0, 0)),
                  pl.BlockSpec((1, LANES), lambda i: (0, 0))],
        out_specs=[pl.BlockSpec((tm, LANES), lambda i: (i, 0)),
                   pl.BlockSpec((_INFO_ROWS, tm), lambda i: (0, i)),
                   pl.BlockSpec((1, LANES), lambda i: (0, 0))],
        out_shape=[jax.ShapeDtypeStruct((t, LANES), F32), jax.ShapeDtypeStruct((_INFO_ROWS, t), F32),
                   jax.ShapeDtypeStruct((1, LANES), F32)],
        scratch_shapes=[pltpu.VMEM((1, LANES), F32)],
        compiler_params=_cparams("arbitrary"),
        name="moe_router",
    )(h, g, w_hi, w_lo, b_r)


def _row_copy(src, dst, src_row, dst_row, sem):
    return pltpu.make_async_copy(src.at[pl.ds(src_row, 1), :], dst.at[pl.ds(dst_row, 1), :], sem)


_ISSUE_UNROLL = 8


def _start_row_gather(src_hbm, idx_ref, n_rows, dst, sem):
    def body(j, carry):
        for p in range(2):
            r = 2 * j + p
            _row_copy(src_hbm, dst, idx_ref[0, 0, r], r, sem).start(priority=p)
        return carry

    lax.fori_loop(0, n_rows // 2, body, 0, unroll=_ISSUE_UNROLL)


def _start_row_scatter(src, idx_ref, n_rows, dst_hbm, sem):
    def body(j, carry):
        for p in range(2):
            r = 2 * j + p
            _row_copy(src, dst_hbm, r, idx_ref[0, 0, r], sem).start(priority=p)
        return carry

    lax.fori_loop(0, n_rows // 2, body, 0, unroll=_ISSUE_UNROLL)


def _wait_rows(hbm, n_rows, vmem, sem, to_vmem):
    window = hbm.at[pl.ds(0, n_rows), :]
    src, dst = (window, vmem) if to_vmem else (vmem, window)
    pltpu.make_async_copy(src, dst, sem).wait()


def _expert_kernel(be_ref, nu_ref, cur_ref, nxt_ref, dst_ref, h_hbm, g_ref, wg_ref, wu_ref, wd_ref, y_hbm,
                   xbuf, ybuf, wg_s, wu_s, wd_s, xsem, ysem):
    i = pl.program_id(0)
    n_used = nu_ref[0]
    slot = i % 2
    used = i < n_used

    @pl.when(i == 0)
    def _():
        ybuf[0] = jnp.zeros(ybuf.shape[1:], F32)
        spare0 = y_hbm.shape[0] - 2 * MOE_BLOCK
        for half in range(2):
            fill = pltpu.make_async_copy(
                ybuf.at[0], y_hbm.at[pl.ds(spare0 + half * MOE_BLOCK, MOE_BLOCK), :], ysem.at[0])
            fill.start()
            fill.wait()

    @pl.when((i == 0) & used)
    def _():
        _start_row_gather(h_hbm, cur_ref, MOE_BLOCK, xbuf.at[0], xsem.at[0])

    @pl.when(i + 1 < n_used)
    def _():
        _start_row_gather(h_hbm, nxt_ref, MOE_BLOCK, xbuf.at[1 - slot], xsem.at[1 - slot])

    @pl.when(used & ((i == 0) | (be_ref[i] != be_ref[jnp.maximum(i - 1, 0)])))
    def _():
        wg_s[...] = wg_ref[0, 0].astype(BF16)
        wu_s[...] = wu_ref[0, 0].astype(BF16)
        wd_s[...] = wd_ref[0, 0].astype(BF16)

    @pl.when(used & (i >= 2))
    def _():
        _wait_rows(y_hbm, MOE_BLOCK, ybuf.at[slot], ysem.at[slot], to_vmem=False)

    @pl.when(used)
    def _():
        _wait_rows(h_hbm, MOE_BLOCK, xbuf.at[slot], xsem.at[slot], to_vmem=True)
        xn = _rms(xbuf[slot], g_ref[...]).astype(BF16)
        hid = (_silu(_dot(xn, wg_s[...])) * _dot(xn, wu_s[...])).astype(BF16)
        ybuf[slot] = _dot(hid, wd_s[...])
        _start_row_scatter(ybuf.at[slot], dst_ref, MOE_BLOCK, y_hbm, ysem.at[slot])

    @pl.when(i == n_used - 1)
    def _():
        _wait_rows(y_hbm, MOE_BLOCK, ybuf.at[slot], ysem.at[slot], to_vmem=False)

        @pl.when(i >= 1)
        def _():
            _wait_rows(y_hbm, MOE_BLOCK, ybuf.at[1 - slot], ysem.at[1 - slot], to_vmem=False)


def _experts(h, g, block_e, n_used, src_tok, dst_row, n_out_rows, w_gate, w_up, w_down, layer):
    t, d = h.shape
    nb = block_e.shape[0]
    de = w_gate.shape[-1]
    idx_blk = lambda f: pl.BlockSpec((1, 1, MOE_BLOCK), lambda i, be, nu: (f(i), 0, 0),
                                     memory_space=pltpu.SMEM)
    grid_spec = pltpu.PrefetchScalarGridSpec(
        num_scalar_prefetch=2,
        grid=(nb,),
        in_specs=[idx_blk(lambda i: i), idx_blk(lambda i: jnp.minimum(i + 1, nb - 1)), idx_blk(lambda i: i),
                  pl.BlockSpec(memory_space=pl.ANY),
                  pl.BlockSpec((1, d), lambda i, be, nu: (0, 0)),
                  pl.BlockSpec((1, 1, d, de), lambda i, be, nu: (layer, be[i], 0, 0)),
                  pl.BlockSpec((1, 1, d, de), lambda i, be, nu: (layer, be[i], 0, 0)),
                  pl.BlockSpec((1, 1, de, d), lambda i, be, nu: (layer, be[i], 0, 0))],
        out_specs=pl.BlockSpec(memory_space=pl.ANY),
        scratch_shapes=[pltpu.VMEM((2, MOE_BLOCK, d), F32), pltpu.VMEM((2, MOE_BLOCK, d), F32),
                        pltpu.VMEM((d, de), BF16), pltpu.VMEM((d, de), BF16), pltpu.VMEM((de, d), BF16),
                        pltpu.SemaphoreType.DMA((2,)), pltpu.SemaphoreType.DMA((2,))],
    )
    src3 = src_tok.reshape(nb, 1, MOE_BLOCK)
    return pl.pallas_call(
        _expert_kernel,
        grid_spec=grid_spec,
        out_shape=jax.ShapeDtypeStruct((n_out_rows, d), F32),
        compiler_params=_cparams("arbitrary"),
        name="moe_experts",
    )(block_e, n_used, src3, src3, dst_row.reshape(nb, 1, MOE_BLOCK), h, g, w_gate, w_up, w_down)


def _combine_kernel(h_ref, info_ref, y1_ref, y2_ref, fg_ref, o_ref, *, final_norm):
    info = info_ref[...]
    out = h_ref[...] + (info[:, 2:3] * y1_ref[...] + info[:, 3:4] * y2_ref[...])
    if final_norm:
        out = _rms(out, fg_ref[...])
    o_ref[...] = out


def _combine(h, info, y_pairs, final_g, final_norm):
    t, d = h.shape
    tm = ROW_TILE
    nt = t // tm
    return pl.pallas_call(
        functools.partial(_combine_kernel, final_norm=final_norm),
        grid=(nt,),
        in_specs=[pl.BlockSpec((tm, d), lambda i: (i, 0)),
                  pl.BlockSpec((tm, LANES), lambda i: (i, 0)),
                  pl.BlockSpec((tm, d), lambda i: (i, 0)),
                  pl.BlockSpec((tm, d), lambda i: (i + nt, 0)),
                  pl.BlockSpec((1, d), lambda i: (0, 0))],
        out_specs=pl.BlockSpec((tm, d), lambda i: (i, 0)),
        out_shape=jax.ShapeDtypeStruct((t, d), F32),
        compiler_params=_cparams("parallel"),
        name="moe_combine",
    )(h, info, y_pairs, y_pairs, final_g)


def _moe(h, g, w_rg, b_rg, w_re, b_re, w_gate, w_up, w_down, layer, final_g, final_norm):
    t, d = h.shape
    pad_cols = LANES - N_EXPERTS - MOE_GROUPS
    w_r = jnp.concatenate([w_re, w_rg, jnp.zeros((d, pad_cols), F32)], axis=1)
    b_r = jnp.concatenate([b_re, b_rg, jnp.zeros((pad_cols,), F32)])[None, :]
    info, infot, cnt = _router(h, g, w_r, b_r)
    expert = infot[0:2].astype(jnp.int32)
    counts = cnt[0, :N_EXPERTS].astype(jnp.int32)
    padded = (counts + MOE_BLOCK - 1) // MOE_BLOCK * MOE_BLOCK
    pends = jnp.cumsum(padded)
    pstarts = pends - padded
    starts = jnp.cumsum(counts) - counts
    ids = jnp.arange(N_EXPERTS, dtype=jnp.int32)
    nb = (2 * t) // MOE_BLOCK + N_EXPERTS
    blk = jnp.arange(nb, dtype=jnp.int32)
    block_e = jnp.minimum(jnp.sum(pends[None, :] <= blk[:, None] * MOE_BLOCK, axis=1), N_EXPERTS - 1)
    block_e = block_e.astype(jnp.int32)
    n_used = (pends[-1:] // MOE_BLOCK).astype(jnp.int32)
    tok = jnp.arange(t, dtype=jnp.int32)[None, :]
    slot = jnp.arange(2, dtype=jnp.int32)[:, None]
    keys = jnp.sort(((expert * t + tok) * 2 + slot).reshape(-1))
    sorted_tok = (keys // 2) % t
    sorted_slot = keys % 2
    pick = lambda table: jnp.sum(jnp.where(block_e[:, None] == ids, table, 0), axis=-1)[:, None]
    row_in_blk = jnp.arange(MOE_BLOCK, dtype=jnp.int32)[None, :]
    rows = blk[:, None] * MOE_BLOCK + row_in_blk
    valid = rows - pick(pstarts) < pick(counts)
    pair = jnp.clip(rows - pick(pstarts - starts), 0, 2 * t - 1)
    src_tok = sorted_tok[pair]
    spare = 2 * t + (blk[:, None] % 2) * MOE_BLOCK + row_in_blk
    dst_row = jnp.where(valid, sorted_slot[pair] * t + sorted_tok[pair], spare)
    y_pairs = _experts(h, g, block_e, n_used, src_tok, dst_row, 2 * t + 2 * MOE_BLOCK, w_gate, w_up, w_down,
                       layer)
    return _combine(h, info, y_pairs, final_g, final_norm)


def kernel(x, mem, norm_mix_g, w_in, ret_gn_g, ssd_conv_w, ssd_conv_b, ssd_dt_bias, ssd_A_log, ssd_D,
           ssd_norm_g, s5_A_re, s5_A_im, s5_B_re, s5_B_im, s5_C_re, s5_C_im, s5_log_step, s5_D, s5_w_glu,
           s5_b_glu, w_out, norm_cross_g, mem_norm_g, w_cq, w_ck, w_cv, w_co, norm_ffn_g, w_route_group,
           b_route_group, w_route_expert, b_route_expert, w_gate, w_up, w_down, norm_final_g):
    batch, seq, d = x.shape
    depth = w_in.shape[0]
    mem_len = mem.shape[1]
    t = batch * seq
    assert d == RET_W * 4 and t % ROW_TILE == 0 and seq % ROW_TILE == 0
    assert seq % RET_CHUNK == 0 and seq % SSD_CHUNK == 0 and seq % (S5_CHUNK * S5_BLOCK_STEPS) == 0
    assert t % MOE_BLOCK == 0
    row = lambda v: v[None, :]
    h = x.reshape(t, d)

    w_kv = jnp.concatenate([w for i in range(depth) for w in (w_ck[i], w_cv[i])], axis=1).astype(BF16)
    kv = _norm_matmul(mem.reshape(batch * mem_len, d), row(mem_norm_g), w_kv, BF16,
                      tm=mem_len, tn=d)
    ret_tables = _retention_tables(seq)
    c0 = _QKVG_W + SSD_W + SSD_XBC_W
    c1 = c0 + SSD_HEADS

    for i in range(depth):
        w_pack = jnp.concatenate(
            [w_in[i][:, :c0], w_in[i][:, c1:], w_in[i][:, c0:c1], jnp.zeros((d, LANES - SSD_HEADS), F32)],
            axis=1).astype(BF16)
        qkvg, z, xbc, u, dt = _in_proj(h, row(norm_mix_g[i]), w_pack)
        out_r = _retention(qkvg, ret_tables, row(ret_gn_g[i]), batch, seq)
        out_m = _ssd(z, xbc, dt, ssd_conv_w[i], ssd_conv_b[i], ssd_dt_bias[i], ssd_A_log[i], ssd_D[i],
                     ssd_norm_g[i], batch, seq)
        s5_ops = _s5_operators(s5_A_re[i], s5_A_im[i], s5_B_re[i], s5_B_im[i], s5_C_re[i], s5_C_im[i],
                               s5_log_step[i])
        y_s = _s5(u, s5_ops, batch, seq)
        h = _out_proj(h, out_r, out_m, y_s, u, row(s5_D[i]), s5_w_glu[i].astype(BF16), row(s5_b_glu[i]),
                      w_out[i].astype(BF16))
        h = _cross(h, row(norm_cross_g[i]), w_cq[i].astype(BF16), kv, i, w_co[i].astype(BF16), seq, mem_len)
        h = _moe(h, row(norm_ffn_g[i]), w_route_group[i], b_route_group[i], w_route_expert[i],
                 b_route_expert[i], w_gate, w_up, w_down, i, row(norm_final_g),
                 final_norm=(i == depth - 1))
    return h.reshape(batch, seq, d)
```

```python
import functools
import math

import jax
import jax.numpy as jnp
from jax import lax
from jax.experimental import pallas as pl
from jax.experimental.pallas import tpu as pltpu

F32 = jnp.float32
BF16 = jnp.bfloat16
HIGHEST = lax.Precision.HIGHEST

EPS = 1e-6
RET_HEADS = 4
RET_HEAD_DIM = 64
RET_W = RET_HEADS * RET_HEAD_DIM
ROPE_BASE = 10000.0
SSD_HEAD_DIM = 64
SSD_HEADS = 8
SSD_GROUPS = 2
SSD_STATE = 128
SSD_CONV = 4
SSD_W = SSD_HEADS * SSD_HEAD_DIM
SSD_XBC_W = SSD_W + 2 * SSD_GROUPS * SSD_STATE
S5_GROUP = 16
S5_GROUPS = 16
S5_STATE = 64
S5_W = S5_GROUP * S5_GROUPS
CROSS_HEADS = 4
MOE_GROUPS = 4
EXPERTS_PER_GROUP = 8
N_EXPERTS = MOE_GROUPS * EXPERTS_PER_GROUP

LANES = 128
ROW_TILE = 512
RET_CHUNK = 256
SSD_CHUNK = 128
S5_CHUNK = 8
S5_LANE_GROUPS = LANES // S5_GROUP
S5_HALVES = S5_W // LANES
S5_BLOCK_STEPS = 64
MOE_BLOCK = 256
MOE_ROW_TILE = 256
VMEM_LIMIT = 48 * 1024 * 1024


def _cparams(*sem):
    return pltpu.CompilerParams(dimension_semantics=sem, vmem_limit_bytes=VMEM_LIMIT)


def _rms(x, g):
    return x * lax.rsqrt(jnp.mean(x * x, axis=-1, keepdims=True) + EPS) * g


def _silu(x):
    return x * jax.nn.sigmoid(x)


def _dot(a, b):
    return jnp.dot(a, b, preferred_element_type=F32)


def _dot_nt(a, b):
    return lax.dot_general(a, b, (((1,), (1,)), ((), ())), preferred_element_type=F32)


def _dot_tn(a, b):
    return lax.dot_general(a, b, (((0,), (0,)), ((), ())), preferred_element_type=F32)


def _load_token_tiles(ref, n_rows):
    nt = ref.shape[0] // n_rows
    return jnp.concatenate([ref[pl.ds(j, n_rows, stride=nt), :] for j in range(nt)], axis=-1)


def _store_token_tiles(ref, x):
    n_rows = x.shape[0]
    nt = ref.shape[0] // n_rows
    for j in range(nt):
        ref[pl.ds(j, n_rows, stride=nt), :] = x[:, j * LANES:(j + 1) * LANES]


def _norm_matmul_kernel(x_ref, g_ref, w_ref, o_ref):
    xn = _rms(x_ref[...], g_ref[...]).astype(BF16)
    o_ref[...] = _dot(xn, w_ref[...]).astype(o_ref.dtype)


def _norm_matmul(x, g, w, out_dtype, tm, tn):
    m, d = x.shape
    n = w.shape[1]
    return pl.pallas_call(
        _norm_matmul_kernel,
        grid=(m // tm, n // tn),
        in_specs=[pl.BlockSpec((tm, d), lambda i, j: (i, 0)),
                  pl.BlockSpec((1, d), lambda i, j: (0, 0)),
                  pl.BlockSpec((d, tn), lambda i, j: (0, j))],
        out_specs=pl.BlockSpec((tm, tn), lambda i, j: (i, j)),
        out_shape=jax.ShapeDtypeStruct((m, n), out_dtype),
        compiler_params=_cparams("parallel", "parallel"),
        name="norm_matmul",
    )(x, g, w)


_QKVG_W = 4 * RET_W
_IN_SPLITS = (_QKVG_W, SSD_W, SSD_XBC_W, S5_W, LANES)


def _in_proj_kernel(h_ref, g_ref, w_ref, qkvg_ref, z_ref, xbc_ref, u_ref, dt_ref):
    xn = _rms(h_ref[...], g_ref[...]).astype(BF16)
    lo = 0
    for ref, width in zip((qkvg_ref, z_ref, xbc_ref, u_ref, dt_ref), _IN_SPLITS):
        ref[...] = _dot(xn, w_ref[:, lo:lo + width]).astype(ref.dtype)
        lo += width


def _in_proj(h, g, w_pack):
    t, d = h.shape
    tm = ROW_TILE
    n = w_pack.shape[1]
    dts = (BF16, BF16, BF16, F32, F32)
    return pl.pallas_call(
        _in_proj_kernel,
        grid=(t // tm,),
        in_specs=[pl.BlockSpec((tm, d), lambda i: (i, 0)),
                  pl.BlockSpec((1, d), lambda i: (0, 0)),
                  pl.BlockSpec((d, n), lambda i: (0, 0))],
        out_specs=[pl.BlockSpec((tm, w), lambda i: (i, 0)) for w in _IN_SPLITS],
        out_shape=[jax.ShapeDtypeStruct((t, w), dt) for w, dt in zip(_IN_SPLITS, dts)],
        compiler_params=_cparams("parallel"),
        name="in_proj",
    )(h, g, w_pack)


def _retention_kernel(qkvg_ref, cos_ref, sin_ref, decay_ref, qdec_ref, kdec_ref, cdec_ref, gn_ref,
                      out_ref, s_ref):
    @pl.when(pl.program_id(1) == 0)
    def _():
        s_ref[...] = jnp.zeros_like(s_ref)

    x = qkvg_ref[...]
    w = RET_W
    q = x[:, 0:w].astype(F32)
    k = x[:, w:2 * w].astype(F32)
    v = x[:, 2 * w:3 * w]
    g = x[:, 3 * w:4 * w].astype(F32)
    half = RET_HEAD_DIM // 2
    lane = lax.broadcasted_iota(jnp.int32, q.shape, 1)
    first_half = (lane % RET_HEAD_DIM) < half

    def rot(t):
        swapped = jnp.where(first_half, pltpu.roll(t, w - half, 1), pltpu.roll(t, half, 1))
        return t * cos_ref[...] + swapped * sin_ref[...]

    qr = rot(q)
    kr = rot(k) * (RET_HEAD_DIM ** -0.5)
    qb = qr.astype(BF16)
    kb = kr.astype(BF16)
    qd = (qr * qdec_ref[...]).astype(BF16)
    kd = (kr * kdec_ref[...]).astype(BF16)
    outs = []
    for h in range(RET_HEADS):
        sl = slice(h * RET_HEAD_DIM, (h + 1) * RET_HEAD_DIM)
        s = _dot_nt(qb[:, sl], kb[:, sl]) * decay_ref[h]
        state = s_ref[h]
        y = _dot(s.astype(BF16), v[:, sl]) + _dot(qd[:, sl], state.astype(BF16))
        s_ref[h] = state * cdec_ref[h] + _dot_tn(kd[:, sl], v[:, sl])
        outs.append(y * lax.rsqrt(jnp.mean(y * y, axis=-1, keepdims=True) + EPS))
    yr = jnp.concatenate(outs, axis=-1)
    out_ref[...] = (_silu(g) * (yr * gn_ref[...])).astype(out_ref.dtype)


def _retention_tables(seq):
    c = RET_CHUNK
    dh = RET_HEAD_DIM
    inv = ROPE_BASE ** (-jnp.arange(0, dh, 2, dtype=F32) / dh)
    ang = jnp.arange(seq, dtype=F32)[:, None] * inv[None, :]
    cos, sin = jnp.cos(ang), jnp.sin(ang)
    cos4 = jnp.tile(jnp.concatenate([cos, cos], axis=-1), (1, RET_HEADS))
    sin4 = jnp.tile(jnp.concatenate([-sin, sin], axis=-1), (1, RET_HEADS))
    lg = jnp.log1p(-(2.0 ** (-5.0 - jnp.arange(RET_HEADS, dtype=F32))))
    i = jnp.arange(c, dtype=F32)
    rel = i[:, None] - i[None, :]
    decay = jnp.where(rel[None] >= 0, jnp.exp(lg[:, None, None] * jnp.maximum(rel, 0.0)[None]), 0.0)
    per_head = lambda t: jnp.repeat(t.T, dh, axis=1)
    qdec = per_head(jnp.exp(lg[:, None] * (i + 1.0)[None]))
    kdec = per_head(jnp.exp(lg[:, None] * (c - 1.0 - i)[None]))
    cdec = jnp.broadcast_to(jnp.exp(lg * c)[:, None, None], (RET_HEADS, dh, dh))
    return cos4, sin4, decay, qdec, kdec, cdec


def _retention(qkvg, tables, gn, batch, seq):
    c = RET_CHUNK
    nc = seq // c
    cos4, sin4, decay, qdec, kdec, cdec = tables
    w = RET_W
    full = lambda shape: pl.BlockSpec(shape, lambda b, j: (0,) * len(shape))
    return pl.pallas_call(
        _retention_kernel,
        grid=(batch, nc),
        in_specs=[pl.BlockSpec((c, _QKVG_W), lambda b, j: (b * nc + j, 0)),
                  pl.BlockSpec((c, w), lambda b, j: (j, 0)),
                  pl.BlockSpec((c, w), lambda b, j: (j, 0)),
                  full((RET_HEADS, c, c)), full((c, w)), full((c, w)),
                  full((RET_HEADS, RET_HEAD_DIM, RET_HEAD_DIM)), full((1, w))],
        out_specs=pl.BlockSpec((c, w), lambda b, j: (b * nc + j, 0)),
        out_shape=jax.ShapeDtypeStruct((batch * seq, w), BF16),
        scratch_shapes=[pltpu.VMEM((RET_HEADS, RET_HEAD_DIM, RET_HEAD_DIM), F32)],
        compiler_params=_cparams("parallel", "arbitrary"),
        name="retention",
    )(qkvg, cos4, sin4, decay, qdec, kdec, cdec, gn)


def _split3(v):
    hi = v.astype(BF16)
    rest = v - hi.astype(F32)
    mid = rest.astype(BF16)
    return hi, mid, (rest - mid.astype(F32)).astype(BF16)


def _ssd_kernel(z_ref, xbc_ref, dt_ref, shift_ref, expand_ref, cw_ref, cb_ref, dtb_ref, a_ref, d_ref,
                ng_ref, out_ref, xcat_ref, s_ref):
    c = SSD_CHUNK
    p = SSD_HEAD_DIM
    n = SSD_STATE

    @pl.when(pl.program_id(1) == 0)
    def _():
        xcat_ref[0:c, :] = jnp.zeros((c, SSD_XBC_W), BF16)
        s_ref[...] = jnp.zeros_like(s_ref)

    x_cur = xbc_ref[...]
    xcat_ref[c:2 * c, :] = x_cur
    x_cat = xcat_ref[...]
    conv = cb_ref[...] + x_cur.astype(F32) * cw_ref[SSD_CONV - 1:SSD_CONV, :]
    for j in range(SSD_CONV - 1):
        conv = conv + _dot(shift_ref[j], x_cat) * cw_ref[j:j + 1, :]
    xcat_ref[0:c, :] = x_cur
    act = _silu(conv)
    xs = act[:, :SSD_W]
    bm = act[:, SSD_W:SSD_W + SSD_GROUPS * n]
    cm = act[:, SSD_W + SSD_GROUPS * n:]

    dt_in = dt_ref[...] + dtb_ref[...]
    dt = jnp.maximum(dt_in, 0.0) + jnp.log1p(jnp.exp(-jnp.abs(dt_in)))
    row = lax.broadcasted_iota(jnp.int32, (c, c), 0)
    col = lax.broadcasted_iota(jnp.int32, (c, c), 1)
    causal = row >= col
    a_cum = jnp.dot(causal.astype(F32), dt * a_ref[...], precision=HIGHEST,
                    preferred_element_type=F32)
    a_cum_t = a_cum.T

    expand = expand_ref[...]
    widen = lambda v: sum(_dot(term, expand) for term in _split3(v))
    dt_w = widen(dt)
    a_w = widen(a_cum)
    a_last_w = a_w[c - 1:c, :]
    chunk_dec_w = jnp.exp(a_last_w)
    xdt = xs * dt_w
    xdt_b = xdt.astype(BF16)
    xdec_b = (xdt * jnp.exp(a_last_w - a_w)).astype(BF16)

    heads_per_group = SSD_HEADS // SSD_GROUPS
    gw = heads_per_group * p
    y_diag, y_off = [], []
    for g in range(SSD_GROUPS):
        bg = bm[:, g * n:(g + 1) * n].astype(BF16)
        cg = cm[:, g * n:(g + 1) * n].astype(BF16)
        cb = _dot_nt(cg, bg)
        gl = slice(g * gw, (g + 1) * gw)
        states = s_ref[g]
        y_off.append(_dot(cg, states.astype(BF16)))
        s_ref[g] = states * chunk_dec_w[:, gl] + _dot_tn(bg, xdec_b[:, gl])
        for r in range(heads_per_group):
            h = g * heads_per_group + r
            lmat = jnp.exp(jnp.where(causal, a_cum[:, h:h + 1] - a_cum_t[h:h + 1, :], -jnp.inf))
            y_diag.append(_dot((cb * lmat).astype(BF16), xdt_b[:, h * p:(h + 1) * p]))
    y = (jnp.concatenate(y_diag, axis=-1) + jnp.concatenate(y_off, axis=-1) * jnp.exp(a_w)
         + xs * d_ref[...])
    out_ref[...] = _rms(y * _silu(z_ref[...].astype(F32)), ng_ref[...]).astype(out_ref.dtype)


def _ssd(z, xbc, dt, conv_w, conv_b, dt_bias, a_log, d_skip, norm_g, batch, seq):
    c = SSD_CHUNK
    nc = seq // c
    pad = lambda v: jnp.pad(v, (0, LANES - v.shape[0]))[None, :]
    a_neg = pad(-jnp.exp(a_log))
    d_wide = jnp.repeat(d_skip, SSD_HEAD_DIM)[None, :]
    t_idx = jnp.arange(c)[None, :, None]
    lag = (SSD_CONV - 1 - jnp.arange(SSD_CONV - 1))[:, None, None]
    shift = (jnp.arange(2 * c)[None, None, :] == c + t_idx - lag).astype(BF16)
    expand = (jnp.arange(LANES)[:, None] == jnp.arange(SSD_W)[None, :] // SSD_HEAD_DIM).astype(BF16)
    full = lambda shape: pl.BlockSpec(shape, lambda b, j: (0,) * len(shape))
    blk = lambda w: pl.BlockSpec((c, w), lambda b, j: (b * nc + j, 0))
    return pl.pallas_call(
        _ssd_kernel,
        grid=(batch, nc),
        in_specs=[blk(SSD_W), blk(SSD_XBC_W), blk(LANES), full((SSD_CONV - 1, c, 2 * c)),
                  full((LANES, SSD_W)), full((SSD_CONV, SSD_XBC_W)), full((1, SSD_XBC_W)),
                  full((1, LANES)), full((1, LANES)), full((1, SSD_W)), full((1, SSD_W))],
        out_specs=blk(SSD_W),
        out_shape=jax.ShapeDtypeStruct((batch * seq, SSD_W), BF16),
        scratch_shapes=[pltpu.VMEM((2 * c, SSD_XBC_W), BF16),
                        pltpu.VMEM((SSD_GROUPS, SSD_STATE, SSD_W // SSD_GROUPS), F32)],
        compiler_params=_cparams("parallel", "arbitrary"),
        name="ssd",
    )(z, xbc, dt, shift, expand, conv_w, conv_b[None, :], pad(dt_bias), a_neg, d_wide, norm_g[None, :])


def _s5_kernel(u_ref, t1_ref, pre_ref, pim_ref, qre_ref, qim_ref, are_ref, aim_ref, y_ref,
               ere_ref, eim_ref, xre_ref, xim_ref, sre_ref, sim_ref):
    batch, tb, _ = u_ref.shape
    cs = S5_CHUNK
    ns = tb // cs

    @pl.when(pl.program_id(1) == 0)
    def _():
        sre_ref[...] = jnp.zeros_like(sre_ref)
        sim_ref[...] = jnp.zeros_like(sim_ref)

    u = jnp.concatenate(
        [jnp.concatenate([u_ref[b, pl.ds(s, ns, stride=cs), :] for s in range(cs)], axis=-1)
         for b in range(batch)], axis=0).astype(BF16)
    n_tiles = ere_ref.shape[0]
    lanes_of = lambda j: slice(j * LANES, (j + 1) * LANES)
    e_re = _dot(u, pre_ref[0])
    e_im = _dot(u, pim_ref[0])
    for j in range(n_tiles):
        ere_ref[j] = e_re[:, lanes_of(j)]
        eim_ref[j] = e_im[:, lanes_of(j)]
    shape = (batch, LANES)
    ar = [jnp.broadcast_to(are_ref[0, :, lanes_of(j)], shape) for j in range(n_tiles)]
    ai = [jnp.broadcast_to(aim_ref[0, :, lanes_of(j)], shape) for j in range(n_tiles)]

    def step(n, carry):
        rows = pl.ds(n, batch, stride=ns)
        out = []
        for j in range(n_tiles):
            xr, xi = carry[j]
            xre_ref[j, rows, :] = xr
            xim_ref[j, rows, :] = xi
            out.append((ar[j] * xr - ai[j] * xi + ere_ref[j, rows, :],
                        ar[j] * xi + ai[j] * xr + eim_ref[j, rows, :]))
        return tuple(out)

    init = tuple((sre_ref[j], sim_ref[j]) for j in range(n_tiles))
    final = lax.fori_loop(0, ns, step, init)
    for j in range(n_tiles):
        sre_ref[j], sim_ref[j] = final[j]
    x_re = jnp.concatenate([xre_ref[j] for j in range(n_tiles)], axis=-1).astype(BF16)
    x_im = jnp.concatenate([xim_ref[j] for j in range(n_tiles)], axis=-1).astype(BF16)
    y = _dot(u, t1_ref[0]) + _dot(x_re, qre_ref[0]) + _dot(x_im, qim_ref[0])
    for b in range(batch):
        for s in range(cs):
            y_ref[b, pl.ds(s, ns, stride=cs), :] = y[b * ns:(b + 1) * ns, s * LANES:(s + 1) * LANES]


def _s5_operators(a_re, a_im, b_re, b_im, c_re, c_im, log_step):
    cs = S5_CHUNK
    ein = functools.partial(jnp.einsum, precision=HIGHEST)
    delta = jnp.exp(log_step)[:, None]
    ar, ai = a_re, a_im
    mag = jnp.exp(ar * delta)
    ang = ai * delta
    lr, li = mag * jnp.cos(ang), mag * jnp.sin(ang)
    den = ar * ar + ai * ai
    nr, ni = lr - 1.0, li
    cr = (nr * ar + ni * ai) / den
    ci = (ni * ar - nr * ai) / den
    bbr = cr[..., None] * b_re - ci[..., None] * b_im
    bbi = cr[..., None] * b_im + ci[..., None] * b_re
    k = jnp.arange(cs + 1, dtype=F32)
    pmag = jnp.exp((ar * delta)[..., None] * k)
    pang = ang[..., None] * k
    pr, pi = pmag * jnp.cos(pang), pmag * jnp.sin(pang)
    clr = c_re[..., None] * pr[:, None] - c_im[..., None] * pi[:, None]
    cli = c_re[..., None] * pi[:, None] + c_im[..., None] * pr[:, None]
    kern = ein('gcpk,gpd->gkcd', clr, bbr) - ein('gcpk,gpd->gkcd', cli, bbi)
    kern = jnp.concatenate([kern[:, :cs], jnp.zeros_like(kern[:, :1])], axis=1)
    s = jnp.arange(cs)
    lag = jnp.where(s[None, :] >= s[:, None], s[None, :] - s[:, None], cs)
    nh, ng = S5_HALVES, S5_LANE_GROUPS
    eye = jnp.eye(ng, dtype=F32)
    halves = lambda t: t.reshape((nh, ng) + t.shape[1:])
    bd = halves(kern).transpose(0, 2, 1, 4, 3)
    bd = (bd[:, :, :, :, None, :] * eye[None, None, :, None, :, None]).reshape(nh, cs + 1, LANES, LANES)
    t1 = bd.astype(BF16)[:, lag].transpose(0, 1, 3, 2, 4).reshape(nh, cs * LANES, cs * LANES)
    rev = cs - 1 - s
    prr, pri = pr[..., rev], pi[..., rev]
    p_re = prr[..., None] * bbr[:, :, None] - pri[..., None] * bbi[:, :, None]
    p_im = prr[..., None] * bbi[:, :, None] + pri[..., None] * bbr[:, :, None]

    def flat_p(t):
        t = halves(t.transpose(0, 2, 3, 1)).transpose(0, 2, 1, 3, 4)
        t = t[:, :, :, :, None, :] * eye[None, None, :, None, :, None]
        return t.reshape(nh, cs * LANES, ng * S5_STATE).astype(BF16)

    def flat_q(t):
        t = halves(t.transpose(0, 2, 3, 1))
        t = t[:, :, :, :, None, :] * eye[None, :, None, None, :, None]
        return t.reshape(nh, ng * S5_STATE, cs * LANES).astype(BF16)

    a_chunk_re = pr[..., cs].reshape(nh, 1, ng * S5_STATE)
    a_chunk_im = pi[..., cs].reshape(nh, 1, ng * S5_STATE)
    return (t1, flat_p(p_re), flat_p(p_im), flat_q(clr[..., 1:]), flat_q(-cli[..., 1:]),
            a_chunk_re, a_chunk_im)


def _s5(u, ops, batch, seq):
    tb = S5_CHUNK * S5_BLOCK_STEPS
    rows = batch * S5_BLOCK_STEPS
    flat = S5_CHUNK * LANES
    nstate = S5_LANE_GROUPS * S5_STATE
    per_h = lambda a, b: pl.BlockSpec((1, a, b), lambda h, j: (h, 0, 0))
    seq_blk = pl.BlockSpec((batch, tb, LANES), lambda h, j: (0, j, h))
    y = pl.pallas_call(
        _s5_kernel,
        grid=(S5_HALVES, seq // tb),
        in_specs=[seq_blk, per_h(flat, flat), per_h(flat, nstate), per_h(flat, nstate),
                  per_h(nstate, flat), per_h(nstate, flat), per_h(1, nstate), per_h(1, nstate)],
        out_specs=seq_blk,
        out_shape=jax.ShapeDtypeStruct((batch, seq, S5_W), F32),
        scratch_shapes=([pltpu.VMEM((nstate // LANES, rows, LANES), F32)] * 4
                        + [pltpu.VMEM((nstate // LANES, batch, LANES), F32)] * 2),
        compiler_params=_cparams("parallel", "arbitrary"),
        name="s5",
    )(u.reshape(batch, seq, S5_W), *ops)
    return y.reshape(batch * seq, S5_W)


def _out_proj_kernel(h_ref, r_ref, m_ref, ys_ref, u_ref, d_ref, wg_ref, bg_ref, wo_ref, o_ref):
    y = ys_ref[...].astype(F32) + d_ref[...] * u_ref[...].astype(F32)
    g = jax.nn.gelu(y)
    s = g * jax.nn.sigmoid(_dot(g.astype(BF16), wg_ref[...]) + bg_ref[...])
    acc = _dot(r_ref[...], wo_ref[0:RET_W, :])
    acc = acc + _dot(m_ref[...], wo_ref[RET_W:RET_W + SSD_W, :])
    acc = acc + _dot(s.astype(BF16), wo_ref[RET_W + SSD_W:, :])
    o_ref[...] = h_ref[...] + acc


def _out_proj(h, out_r, out_m, y_s, u, d_s5, w_glu, b_glu, w_out):
    t, d = h.shape
    tm = ROW_TILE
    row = lambda w: pl.BlockSpec((tm, w), lambda i: (i, 0))
    full = lambda a, b: pl.BlockSpec((a, b), lambda i: (0, 0))
    return pl.pallas_call(
        _out_proj_kernel,
        grid=(t // tm,),
        in_specs=[row(d), row(RET_W), row(SSD_W), row(S5_W), row(S5_W),
                  full(1, S5_W), full(S5_W, S5_W), full(1, S5_W), full(d, d)],
        out_specs=row(d),
        out_shape=jax.ShapeDtypeStruct((t, d), F32),
        compiler_params=_cparams("parallel"),
        name="out_proj",
    )(h, out_r, out_m, y_s, u, d_s5, w_glu, b_glu, w_out)


def _cross_kernel(h_ref, g_ref, wq_ref, k_ref, v_ref, wo_ref, o_ref):
    h = h_ref[...]
    d = h.shape[-1]
    dh = d // CROSS_HEADS
    q = _dot(_rms(h, g_ref[...]).astype(BF16), wq_ref[...]).astype(BF16)
    outs = []
    for i in range(CROSS_HEADS):
        sl = slice(i * dh, (i + 1) * dh)
        s = _dot_nt(q[:, sl], k_ref[:, sl]) * (dh ** -0.5)
        p = jnp.exp(s - jnp.max(s, axis=-1, keepdims=True))
        o = _dot(p.astype(BF16), v_ref[:, sl])
        outs.append(o / jnp.sum(p, axis=-1, keepdims=True))
    o = jnp.concatenate(outs, axis=-1).astype(BF16)
    _store_token_tiles(o_ref, h + _dot(o, wo_ref[...]))


def _cross(h, g, wq, kv, layer, wo, seq, mem_len):
    t, d = h.shape
    nt = d // LANES
    tm = ROW_TILE
    tiles_per_seq = seq // tm
    full = lambda a, b: pl.BlockSpec((a, b), lambda i: (0, 0))
    return pl.pallas_call(
        _cross_kernel,
        grid=(t // tm,),
        in_specs=[pl.BlockSpec((tm, d), lambda i: (i, 0)), full(1, d), full(d, d),
                  pl.BlockSpec((mem_len, d), lambda i: (i // tiles_per_seq, 2 * layer)),
                  pl.BlockSpec((mem_len, d), lambda i: (i // tiles_per_seq, 2 * layer + 1)),
                  full(d, d)],
        out_specs=pl.BlockSpec((tm * nt, LANES), lambda i: (i, 0)),
        out_shape=jax.ShapeDtypeStruct((t * nt, LANES), F32),
        compiler_params=_cparams("parallel"),
        name="cross_attn",
    )(h, g, wq, kv, kv, wo)


_GROUP_LANE0 = N_EXPERTS


def _router_kernel(h_ref, g_ref, whi_ref, wlo_ref, b_ref, info_ref, infot_ref, cnt_ref, carry_ref):
    @pl.when(pl.program_id(0) == 0)
    def _():
        carry_ref[...] = jnp.zeros_like(carry_ref)

    xn = _rms(_load_token_tiles(h_ref, info_ref.shape[0]), g_ref[...])
    x_hi = xn.astype(BF16)
    x_lo = (xn - x_hi.astype(F32)).astype(BF16)
    logits = (_dot(x_hi, whi_ref[...]) + _dot(x_lo, whi_ref[...]) + _dot(x_hi, wlo_ref[...])
              + b_ref[...])
    tm = logits.shape[0]
    lane = lax.broadcasted_iota(jnp.int32, logits.shape, 1).astype(F32)
    neg = -jnp.inf

    def first_argmax(vals):
        m = jnp.max(vals, axis=-1, keepdims=True)
        return m, jnp.min(jnp.where(vals == m, lane, float(LANES)), axis=-1, keepdims=True)

    gl = jnp.where((lane >= _GROUP_LANE0) & (lane < _GROUP_LANE0 + MOE_GROUPS), logits, neg)
    gmax, glane = first_argmax(gl)
    pg = 1.0 / jnp.sum(jnp.exp(gl - gmax), axis=-1, keepdims=True)
    lo = (glane - _GROUP_LANE0) * EXPERTS_PER_GROUP
    el = jnp.where((lane >= lo) & (lane < lo + EXPERTS_PER_GROUP), logits, neg)
    m1, e1 = first_argmax(el)
    m2, e2 = first_argmax(jnp.where(lane == e1, neg, el))
    p2 = jnp.exp(m2 - m1)
    gate1 = pg / (1.0 + p2)
    gate2 = pg * p2 / (1.0 + p2)

    hot = jnp.where((lane == e1) | (lane == e2), 1.0, 0.0)
    row = lax.broadcasted_iota(jnp.int32, (tm, tm), 0)
    col = lax.broadcasted_iota(jnp.int32, (tm, tm), 1)
    before = jnp.where(row > col, 1.0, 0.0).astype(BF16)
    cum = _dot(before, hot.astype(BF16)) + carry_ref[...]
    rank1 = jnp.sum(jnp.where(lane == e1, cum, 0.0), axis=-1, keepdims=True)
    rank2 = jnp.sum(jnp.where(lane == e2, cum, 0.0), axis=-1, keepdims=True)
    carry_ref[...] = carry_ref[...] + jnp.sum(hot, axis=0, keepdims=True)
    cnt_ref[...] = carry_ref[...]

    info = jnp.zeros(logits.shape, F32)
    for i, val in enumerate((e1, e2, rank1, rank2, gate1, gate2)):
        info = jnp.where(lane == i, val, info)
    info_ref[...] = info
    infot_ref[...] = info.T[0:_INFO_ROWS, :]


_INFO_ROWS = 8


def _router(h_tiles, g, w_r, b_r, d):
    nt = d // LANES
    t = h_tiles.shape[0] // nt
    tm = ROW_TILE
    w_hi = w_r.astype(BF16)
    w_lo = (w_r - w_hi.astype(F32)).astype(BF16)
    return pl.pallas_call(
        _router_kernel,
        grid=(t // tm,),
        in_specs=[pl.BlockSpec((tm * nt, LANES), lambda i: (i, 0)),
                  pl.BlockSpec((1, d), lambda i: (0, 0)),
                  pl.BlockSpec((d, LANES), lambda i: (0, 0)),
                  pl.BlockSpec((d, LANES), lambda i: (0, 0)),
                  pl.BlockSpec((1, LANES), lambda i: (0, 0))],
        out_specs=[pl.BlockSpec((tm, LANES), lambda i: (i, 0)),
                   pl.BlockSpec((_INFO_ROWS, tm), lambda i: (0, i)),
                   pl.BlockSpec((1, LANES), lambda i: (0, 0))],
        out_shape=[jax.ShapeDtypeStruct((t, LANES), F32), jax.ShapeDtypeStruct((_INFO_ROWS, t), F32),
                   jax.ShapeDtypeStruct((1, LANES), F32)],
        scratch_shapes=[pltpu.VMEM((1, LANES), F32)],
        compiler_params=_cparams("arbitrary"),
        name="moe_router",
    )(h_tiles, g, w_hi, w_lo, b_r)


def _token_copy(src_hbm, dst_vmem, src_row, dst_token, nt, sem):
    return pltpu.make_async_copy(src_hbm.at[pl.ds(pl.multiple_of(src_row, nt), nt), :],
                                 dst_vmem.at[pl.ds(dst_token * nt, nt), :], sem)


_ISSUE_UNROLL = 8


def _start_token_gather(src_hbm, idx_ref, n_tokens, dst, sem):
    nt = dst.shape[0] // n_tokens

    def body(j, carry):
        for p in range(2):
            r = 2 * j + p
            _token_copy(src_hbm, dst, idx_ref[0, 0, r], r, nt, sem).start(priority=p)
        return carry

    lax.fori_loop(0, n_tokens // 2, body, 0, unroll=_ISSUE_UNROLL)


def _wait_token_gather(src_hbm, dst, sem):
    pltpu.make_async_copy(src_hbm.at[pl.ds(0, dst.shape[0]), :], dst, sem).wait()


def _expert_kernel(be_ref, nu_ref, cur_ref, nxt_ref, h_hbm, g_ref, wg_ref, wu_ref, wd_ref, y_ref,
                   xbuf, wg_s, wu_s, wd_s, sem):
    i = pl.program_id(0)
    n_used = nu_ref[0]
    slot = i % 2
    used = i < n_used

    @pl.when((i == 0) & used)
    def _():
        _start_token_gather(h_hbm, cur_ref, MOE_BLOCK, xbuf.at[0], sem.at[0])

    @pl.when(i + 1 < n_used)
    def _():
        _start_token_gather(h_hbm, nxt_ref, MOE_BLOCK, xbuf.at[1 - slot], sem.at[1 - slot])

    @pl.when(used & ((i == 0) | (be_ref[i] != be_ref[jnp.maximum(i - 1, 0)])))
    def _():
        wg_s[...] = wg_ref[0, 0].astype(BF16)
        wu_s[...] = wu_ref[0, 0].astype(BF16)
        wd_s[...] = wd_ref[0, 0].astype(BF16)

    @pl.when(used)
    def _():
        _wait_token_gather(h_hbm, xbuf.at[slot], sem.at[slot])
        xn = _rms(_load_token_tiles(xbuf.at[slot], MOE_BLOCK), g_ref[...]).astype(BF16)
        hid = (_silu(_dot(xn, wg_s[...])) * _dot(xn, wu_s[...])).astype(BF16)
        _store_token_tiles(y_ref, _dot(hid, wd_s[...]))

    @pl.when(jnp.logical_not(used))
    def _():
        y_ref[...] = jnp.zeros_like(y_ref)


def _experts(h_tiles, g, block_e, n_used, src_row, w_gate, w_up, w_down, layer):
    d = w_gate.shape[-2]
    nt = d // LANES
    nb = block_e.shape[0]
    de = w_gate.shape[-1]
    idx_blk = lambda f: pl.BlockSpec((1, 1, MOE_BLOCK), lambda i, be, nu: (f(i), 0, 0),
                                     memory_space=pltpu.SMEM)
    grid_spec = pltpu.PrefetchScalarGridSpec(
        num_scalar_prefetch=2,
        grid=(nb,),
        in_specs=[idx_blk(lambda i: i), idx_blk(lambda i: jnp.minimum(i + 1, nb - 1)),
                  pl.BlockSpec(memory_space=pl.ANY),
                  pl.BlockSpec((1, d), lambda i, be, nu: (0, 0)),
                  pl.BlockSpec((1, 1, d, de), lambda i, be, nu: (layer, be[i], 0, 0)),
                  pl.BlockSpec((1, 1, d, de), lambda i, be, nu: (layer, be[i], 0, 0)),
                  pl.BlockSpec((1, 1, de, d), lambda i, be, nu: (layer, be[i], 0, 0))],
        out_specs=pl.BlockSpec((MOE_BLOCK * nt, LANES), lambda i, be, nu: (i, 0)),
        scratch_shapes=[pltpu.VMEM((2, MOE_BLOCK * nt, LANES), F32), pltpu.VMEM((d, de), BF16),
                        pltpu.VMEM((d, de), BF16), pltpu.VMEM((de, d), BF16),
                        pltpu.SemaphoreType.DMA((2,))],
    )
    src3 = src_row.reshape(nb, 1, MOE_BLOCK)
    return pl.pallas_call(
        _expert_kernel,
        grid_spec=grid_spec,
        out_shape=jax.ShapeDtypeStruct((nb * MOE_BLOCK * nt, LANES), F32),
        compiler_params=_cparams("arbitrary"),
        name="moe_experts",
    )(block_e, n_used, src3, src3, h_tiles, g, w_gate, w_up, w_down)


def _combine_kernel(dest_ref, h_ref, info_ref, y_hbm, fg_ref, o_ref, ybuf, sem, *, final_norm):
    tm = info_ref.shape[0]
    nt = h_ref.shape[0] // tm

    def start(r, carry):
        for k in range(2):
            _token_copy(y_hbm, ybuf.at[k], dest_ref[0, 0, k * tm + r], r, nt, sem.at[k]).start(priority=k)
        return carry

    lax.fori_loop(0, tm, start, 0, unroll=_ISSUE_UNROLL)
    for k in range(2):
        _wait_token_gather(y_hbm, ybuf.at[k], sem.at[k])
    info = info_ref[...]
    out = _load_token_tiles(h_ref, tm) + (info[:, 4:5] * _load_token_tiles(ybuf.at[0], tm)
                                          + info[:, 5:6] * _load_token_tiles(ybuf.at[1], tm))
    if final_norm:
        out = _rms(out, fg_ref[...])
    o_ref[...] = out


def _combine(h_tiles, info, dest, ybuf, final_g, final_norm):
    t = info.shape[0]
    d = final_g.shape[-1]
    nt = d // LANES
    tm = MOE_ROW_TILE
    return pl.pallas_call(
        functools.partial(_combine_kernel, final_norm=final_norm),
        grid=(t // tm,),
        in_specs=[pl.BlockSpec((1, 1, 2 * tm), lambda i: (i, 0, 0), memory_space=pltpu.SMEM),
                  pl.BlockSpec((tm * nt, LANES), lambda i: (i, 0)),
                  pl.BlockSpec((tm, LANES), lambda i: (i, 0)),
                  pl.BlockSpec(memory_space=pl.ANY),
                  pl.BlockSpec((1, d), lambda i: (0, 0))],
        out_specs=pl.BlockSpec((tm, d), lambda i: (i, 0)),
        out_shape=jax.ShapeDtypeStruct((t, d), F32),
        scratch_shapes=[pltpu.VMEM((2, tm * nt, LANES), F32), pltpu.SemaphoreType.DMA((2,))],
        compiler_params=_cparams("arbitrary"),
        name="moe_combine",
    )(dest.reshape(2, t // tm, tm).transpose(1, 0, 2).reshape(t // tm, 1, 2 * tm), h_tiles, info, ybuf, final_g)


def _moe(h_tiles, g, w_rg, b_rg, w_re, b_re, w_gate, w_up, w_down, layer, final_g, final_norm):
    d = g.shape[-1]
    nt = d // LANES
    t = h_tiles.shape[0] // nt
    pad_cols = LANES - N_EXPERTS - MOE_GROUPS
    w_r = jnp.concatenate([w_re, w_rg, jnp.zeros((d, pad_cols), F32)], axis=1)
    b_r = jnp.concatenate([b_re, b_rg, jnp.zeros((pad_cols,), F32)])[None, :]
    info, infot, cnt = _router(h_tiles, g, w_r, b_r, d)
    expert = infot[0:2].astype(jnp.int32)
    rank = infot[2:4].astype(jnp.int32)
    counts = cnt[0, :N_EXPERTS].astype(jnp.int32)
    padded = (counts + MOE_BLOCK - 1) // MOE_BLOCK * MOE_BLOCK
    pends = jnp.cumsum(padded)
    pstarts = pends - padded
    starts = jnp.cumsum(counts) - counts
    ids = jnp.arange(N_EXPERTS, dtype=jnp.int32)
    dest = jnp.sum(jnp.where(expert[..., None] == ids, pstarts, 0), axis=-1) + rank
    nb = (2 * t) // MOE_BLOCK + N_EXPERTS
    blk = jnp.arange(nb, dtype=jnp.int32)
    block_e = jnp.minimum(jnp.sum(pends[None, :] <= blk[:, None] * MOE_BLOCK, axis=1), N_EXPERTS - 1)
    block_e = block_e.astype(jnp.int32)
    n_used = (pends[-1:] // MOE_BLOCK).astype(jnp.int32)
    keys = expert * t + jnp.arange(t, dtype=jnp.int32)[None, :]
    sorted_tok = jnp.sort(keys.reshape(-1)) % t
    shift = jnp.sum(jnp.where(block_e[:, None] == ids, pstarts - starts, 0), axis=-1)
    pair = blk[:, None] * MOE_BLOCK + jnp.arange(MOE_BLOCK, dtype=jnp.int32)[None, :] - shift[:, None]
    src_tok = sorted_tok[jnp.clip(pair, 0, 2 * t - 1)]
    ybuf = _experts(h_tiles, g, block_e, n_used, src_tok * nt, w_gate, w_up, w_down, layer)
    return _combine(h_tiles, info, dest * nt, ybuf, final_g, final_norm)


def kernel(x, mem, norm_mix_g, w_in, ret_gn_g, ssd_conv_w, ssd_conv_b, ssd_dt_bias, ssd_A_log, ssd_D,
           ssd_norm_g, s5_A_re, s5_A_im, s5_B_re, s5_B_im, s5_C_re, s5_C_im, s5_log_step, s5_D, s5_w_glu,
           s5_b_glu, w_out, norm_cross_g, mem_norm_g, w_cq, w_ck, w_cv, w_co, norm_ffn_g, w_route_group,
           b_route_group, w_route_expert, b_route_expert, w_gate, w_up, w_down, norm_final_g):
    batch, seq, d = x.shape
    depth = w_in.shape[0]
    mem_len = mem.shape[1]
    t = batch * seq
    assert d == RET_W * 4 and t % ROW_TILE == 0 and seq % ROW_TILE == 0
    assert seq % RET_CHUNK == 0 and seq % SSD_CHUNK == 0 and seq % (S5_CHUNK * S5_BLOCK_STEPS) == 0
    assert t % MOE_BLOCK == 0 and t % MOE_ROW_TILE == 0
    row = lambda v: v[None, :]
    h = x.reshape(t, d)

    w_kv = jnp.concatenate([w for i in range(depth) for w in (w_ck[i], w_cv[i])], axis=1).astype(BF16)
    kv = _norm_matmul(mem.reshape(batch * mem_len, d), row(mem_norm_g), w_kv, BF16,
                      tm=mem_len, tn=d)
    ret_tables = _retention_tables(seq)
    c0 = _QKVG_W + SSD_W + SSD_XBC_W
    c1 = c0 + SSD_HEADS

    for i in range(depth):
        w_pack = jnp.concatenate(
            [w_in[i][:, :c0], w_in[i][:, c1:], w_in[i][:, c0:c1], jnp.zeros((d, LANES - SSD_HEADS), F32)],
            axis=1).astype(BF16)
        qkvg, z, xbc, u, dt = _in_proj(h, row(norm_mix_g[i]), w_pack)
        out_r = _retention(qkvg, ret_tables, row(ret_gn_g[i]), batch, seq)
        out_m = _ssd(z, xbc, dt, ssd_conv_w[i], ssd_conv_b[i], ssd_dt_bias[i], ssd_A_log[i], ssd_D[i],
                     ssd_norm_g[i], batch, seq)
        s5_ops = _s5_operators(s5_A_re[i], s5_A_im[i], s5_B_re[i], s5_B_im[i], s5_C_re[i], s5_C_im[i],
                               s5_log_step[i])
        y_s = _s5(u, s5_ops, batch, seq)
        h = _out_proj(h, out_r, out_m, y_s, u, row(s5_D[i]), s5_w_glu[i].astype(BF16), row(s5_b_glu[i]),
                      w_out[i].astype(BF16))
        h = _cross(h, row(norm_cross_g[i]), w_cq[i].astype(BF16), kv, i, w_co[i].astype(BF16), seq, mem_len)
        h = _moe(h, row(norm_ffn_g[i]), w_route_group[i], b_route_group[i], w_route_expert[i],
                 b_route_expert[i], w_gate, w_up, w_down, i, row(norm_final_g),
                 final_norm=(i == depth - 1))
    return h.reshape(batch, seq, d)
```

```python
import functools
import math

import jax
import jax.numpy as jnp
from jax import lax
from jax.experimental import pallas as pl
from jax.experimental.pallas import tpu as pltpu

F32 = jnp.float32
BF16 = jnp.bfloat16
HIGHEST = lax.Precision.HIGHEST

EPS = 1e-6
RET_HEADS = 4
RET_HEAD_DIM = 64
RET_W = RET_HEADS * RET_HEAD_DIM
ROPE_BASE = 10000.0
SSD_HEAD_DIM = 64
SSD_HEADS = 8
SSD_GROUPS = 2
SSD_STATE = 128
SSD_CONV = 4
SSD_W = SSD_HEADS * SSD_HEAD_DIM
SSD_XBC_W = SSD_W + 2 * SSD_GROUPS * SSD_STATE
S5_GROUP = 16
S5_GROUPS = 16
S5_STATE = 64
S5_W = S5_GROUP * S5_GROUPS
CROSS_HEADS = 4
MOE_GROUPS = 4
EXPERTS_PER_GROUP = 8
N_EXPERTS = MOE_GROUPS * EXPERTS_PER_GROUP

LANES = 128
ROW_TILE = 512
RET_CHUNK = 256
SSD_CHUNK = 128
S5_CHUNK = 8
S5_LANE_GROUPS = LANES // S5_GROUP
S5_HALVES = S5_W // LANES
S5_BLOCK_STEPS = 64
MOE_BLOCK = 256
MOE_ROW_TILE = 256
VMEM_LIMIT = 48 * 1024 * 1024


def _cparams(*sem):
    return pltpu.CompilerParams(dimension_semantics=sem, vmem_limit_bytes=VMEM_LIMIT)


def _rms(x, g):
    return x * lax.rsqrt(jnp.mean(x * x, axis=-1, keepdims=True) + EPS) * g


def _silu(x):
    return x * jax.nn.sigmoid(x)


def _dot(a, b):
    return jnp.dot(a, b, preferred_element_type=F32)


def _dot_nt(a, b):
    return lax.dot_general(a, b, (((1,), (1,)), ((), ())), preferred_element_type=F32)


def _dot_tn(a, b):
    return lax.dot_general(a, b, (((0,), (0,)), ((), ())), preferred_element_type=F32)


def _load_token_tiles(ref, n_rows):
    nt = ref.shape[0] // n_rows
    return jnp.concatenate([ref[pl.ds(j, n_rows, stride=nt), :] for j in range(nt)], axis=-1)


def _store_token_tiles(ref, x):
    n_rows = x.shape[0]
    nt = ref.shape[0] // n_rows
    for j in range(nt):
        ref[pl.ds(j, n_rows, stride=nt), :] = x[:, j * LANES:(j + 1) * LANES]


def _norm_matmul_kernel(x_ref, g_ref, w_ref, o_ref):
    xn = _rms(x_ref[...], g_ref[...]).astype(BF16)
    o_ref[...] = _dot(xn, w_ref[...]).astype(o_ref.dtype)


def _norm_matmul(x, g, w, out_dtype, tm, tn):
    m, d = x.shape
    n = w.shape[1]
    return pl.pallas_call(
        _norm_matmul_kernel,
        grid=(m // tm, n // tn),
        in_specs=[pl.BlockSpec((tm, d), lambda i, j: (i, 0)),
                  pl.BlockSpec((1, d), lambda i, j: (0, 0)),
                  pl.BlockSpec((d, tn), lambda i, j: (0, j))],
        out_specs=pl.BlockSpec((tm, tn), lambda i, j: (i, j)),
        out_shape=jax.ShapeDtypeStruct((m, n), out_dtype),
        compiler_params=_cparams("parallel", "parallel"),
        name="norm_matmul",
    )(x, g, w)


_QKVG_W = 4 * RET_W
_IN_SPLITS = (_QKVG_W, SSD_W, SSD_XBC_W, S5_W, LANES)


def _in_proj_kernel(h_ref, g_ref, w_ref, qkvg_ref, z_ref, xbc_ref, u_ref, dt_ref):
    xn = _rms(h_ref[...], g_ref[...]).astype(BF16)
    lo = 0
    for ref, width in zip((qkvg_ref, z_ref, xbc_ref, u_ref, dt_ref), _IN_SPLITS):
        ref[...] = _dot(xn, w_ref[:, lo:lo + width]).astype(ref.dtype)
        lo += width


def _in_proj(h, g, w_pack):
    t, d = h.shape
    tm = ROW_TILE
    n = w_pack.shape[1]
    dts = (BF16, BF16, BF16, F32, F32)
    return pl.pallas_call(
        _in_proj_kernel,
        grid=(t // tm,),
        in_specs=[pl.BlockSpec((tm, d), lambda i: (i, 0)),
                  pl.BlockSpec((1, d), lambda i: (0, 0)),
                  pl.BlockSpec((d, n), lambda i: (0, 0))],
        out_specs=[pl.BlockSpec((tm, w), lambda i: (i, 0)) for w in _IN_SPLITS],
        out_shape=[jax.ShapeDtypeStruct((t, w), dt) for w, dt in zip(_IN_SPLITS, dts)],
        compiler_params=_cparams("parallel"),
        name="in_proj",
    )(h, g, w_pack)


def _retention_kernel(qkvg_ref, cos_ref, sin_ref, decay_ref, qdec_ref, kdec_ref, cdec_ref, gn_ref,
                      out_ref, s_ref):
    @pl.when(pl.program_id(1) == 0)
    def _():
        s_ref[...] = jnp.zeros_like(s_ref)

    x = qkvg_ref[...]
    w = RET_W
    q = x[:, 0:w].astype(F32)
    k = x[:, w:2 * w].astype(F32)
    v = x[:, 2 * w:3 * w]
    g = x[:, 3 * w:4 * w].astype(F32)
    half = RET_HEAD_DIM // 2
    lane = lax.broadcasted_iota(jnp.int32, q.shape, 1)
    first_half = (lane % RET_HEAD_DIM) < half

    def rot(t):
        swapped = jnp.where(first_half, pltpu.roll(t, w - half, 1), pltpu.roll(t, half, 1))
        return t * cos_ref[...] + swapped * sin_ref[...]

    qr = rot(q)
    kr = rot(k) * (RET_HEAD_DIM ** -0.5)
    qb = qr.astype(BF16)
    kb = kr.astype(BF16)
    qd = (qr * qdec_ref[...]).astype(BF16)
    kd = (kr * kdec_ref[...]).astype(BF16)
    outs = []
    for h in range(RET_HEADS):
        sl = slice(h * RET_HEAD_DIM, (h + 1) * RET_HEAD_DIM)
        s = _dot_nt(qb[:, sl], kb[:, sl]) * decay_ref[h]
        state = s_ref[h]
        y = _dot(s.astype(BF16), v[:, sl]) + _dot(qd[:, sl], state.astype(BF16))
        s_ref[h] = state * cdec_ref[h] + _dot_tn(kd[:, sl], v[:, sl])
        outs.append(y * lax.rsqrt(jnp.mean(y * y, axis=-1, keepdims=True) + EPS))
    yr = jnp.concatenate(outs, axis=-1)
    out_ref[...] = (_silu(g) * (yr * gn_ref[...])).astype(out_ref.dtype)


def _retention_tables(seq):
    c = RET_CHUNK
    dh = RET_HEAD_DIM
    inv = ROPE_BASE ** (-jnp.arange(0, dh, 2, dtype=F32) / dh)
    ang = jnp.arange(seq, dtype=F32)[:, None] * inv[None, :]
    cos, sin = jnp.cos(ang), jnp.sin(ang)
    cos4 = jnp.tile(jnp.concatenate([cos, cos], axis=-1), (1, RET_HEADS))
    sin4 = jnp.tile(jnp.concatenate([-sin, sin], axis=-1), (1, RET_HEADS))
    lg = jnp.log1p(-(2.0 ** (-5.0 - jnp.arange(RET_HEADS, dtype=F32))))
    i = jnp.arange(c, dtype=F32)
    rel = i[:, None] - i[None, :]
    decay = jnp.where(rel[None] >= 0, jnp.exp(lg[:, None, None] * jnp.maximum(rel, 0.0)[None]), 0.0)
    per_head = lambda t: jnp.repeat(t.T, dh, axis=1)
    qdec = per_head(jnp.exp(lg[:, None] * (i + 1.0)[None]))
    kdec = per_head(jnp.exp(lg[:, None] * (c - 1.0 - i)[None]))
    cdec = jnp.broadcast_to(jnp.exp(lg * c)[:, None, None], (RET_HEADS, dh, dh))
    return cos4, sin4, decay, qdec, kdec, cdec


def _retention(qkvg, tables, gn, batch, seq):
    c = RET_CHUNK
    nc = seq // c
    cos4, sin4, decay, qdec, kdec, cdec = tables
    w = RET_W
    full = lambda shape: pl.BlockSpec(shape, lambda b, j: (0,) * len(shape))
    return pl.pallas_call(
        _retention_kernel,
        grid=(batch, nc),
        in_specs=[pl.BlockSpec((c, _QKVG_W), lambda b, j: (b * nc + j, 0)),
                  pl.BlockSpec((c, w), lambda b, j: (j, 0)),
                  pl.BlockSpec((c, w), lambda b, j: (j, 0)),
                  full((RET_HEADS, c, c)), full((c, w)), full((c, w)),
                  full((RET_HEADS, RET_HEAD_DIM, RET_HEAD_DIM)), full((1, w))],
        out_specs=pl.BlockSpec((c, w), lambda b, j: (b * nc + j, 0)),
        out_shape=jax.ShapeDtypeStruct((batch * seq, w), BF16),
        scratch_shapes=[pltpu.VMEM((RET_HEADS, RET_HEAD_DIM, RET_HEAD_DIM), F32)],
        compiler_params=_cparams("parallel", "arbitrary"),
        name="retention",
    )(qkvg, cos4, sin4, decay, qdec, kdec, cdec, gn)


def _split3(v):
    hi = v.astype(BF16)
    rest = v - hi.astype(F32)
    mid = rest.astype(BF16)
    return hi, mid, (rest - mid.astype(F32)).astype(BF16)


def _ssd_kernel(z_ref, xbc_ref, dt_ref, shift_ref, expand_ref, cw_ref, cb_ref, dtb_ref, a_ref, d_ref,
                ng_ref, out_ref, xcat_ref, s_ref):
    c = SSD_CHUNK
    p = SSD_HEAD_DIM
    n = SSD_STATE

    @pl.when(pl.program_id(1) == 0)
    def _():
        xcat_ref[0:c, :] = jnp.zeros((c, SSD_XBC_W), BF16)
        s_ref[...] = jnp.zeros_like(s_ref)

    x_cur = xbc_ref[...]
    xcat_ref[c:2 * c, :] = x_cur
    x_cat = xcat_ref[...]
    conv = cb_ref[...] + x_cur.astype(F32) * cw_ref[SSD_CONV - 1:SSD_CONV, :]
    for j in range(SSD_CONV - 1):
        conv = conv + _dot(shift_ref[j], x_cat) * cw_ref[j:j + 1, :]
    xcat_ref[0:c, :] = x_cur
    act = _silu(conv)
    xs = act[:, :SSD_W]
    bm = act[:, SSD_W:SSD_W + SSD_GROUPS * n]
    cm = act[:, SSD_W + SSD_GROUPS * n:]

    dt_in = dt_ref[...] + dtb_ref[...]
    dt = jnp.maximum(dt_in, 0.0) + jnp.log1p(jnp.exp(-jnp.abs(dt_in)))
    row = lax.broadcasted_iota(jnp.int32, (c, c), 0)
    col = lax.broadcasted_iota(jnp.int32, (c, c), 1)
    causal = row >= col
    a_cum = jnp.dot(causal.astype(F32), dt * a_ref[...], precision=HIGHEST,
                    preferred_element_type=F32)
    a_cum_t = a_cum.T

    expand = expand_ref[...]
    widen = lambda v: sum(_dot(term, expand) for term in _split3(v))
    dt_w = widen(dt)
    a_w = widen(a_cum)
    a_last_w = a_w[c - 1:c, :]
    chunk_dec_w = jnp.exp(a_last_w)
    xdt = xs * dt_w
    xdt_b = xdt.astype(BF16)
    xdec_b = (xdt * jnp.exp(a_last_w - a_w)).astype(BF16)

    heads_per_group = SSD_HEADS // SSD_GROUPS
    gw = heads_per_group * p
    y_diag, y_off = [], []
    for g in range(SSD_GROUPS):
        bg = bm[:, g * n:(g + 1) * n].astype(BF16)
        cg = cm[:, g * n:(g + 1) * n].astype(BF16)
        cb = _dot_nt(cg, bg)
        gl = slice(g * gw, (g + 1) * gw)
        states = s_ref[g]
        y_off.append(_dot(cg, states.astype(BF16)))
        s_ref[g] = states * chunk_dec_w[:, gl] + _dot_tn(bg, xdec_b[:, gl])
        for r in range(heads_per_group):
            h = g * heads_per_group + r
            lmat = jnp.exp(jnp.where(causal, a_cum[:, h:h + 1] - a_cum_t[h:h + 1, :], -jnp.inf))
            y_diag.append(_dot((cb * lmat).astype(BF16), xdt_b[:, h * p:(h + 1) * p]))
    y = (jnp.concatenate(y_diag, axis=-1) + jnp.concatenate(y_off, axis=-1) * jnp.exp(a_w)
         + xs * d_ref[...])
    out_ref[...] = _rms(y * _silu(z_ref[...].astype(F32)), ng_ref[...]).astype(out_ref.dtype)


def _ssd(z, xbc, dt, conv_w, conv_b, dt_bias, a_log, d_skip, norm_g, batch, seq):
    c = SSD_CHUNK
    nc = seq // c
    pad = lambda v: jnp.pad(v, (0, LANES - v.shape[0]))[None, :]
    a_neg = pad(-jnp.exp(a_log))
    d_wide = jnp.repeat(d_skip, SSD_HEAD_DIM)[None, :]
    t_idx = jnp.arange(c)[None, :, None]
    lag = (SSD_CONV - 1 - jnp.arange(SSD_CONV - 1))[:, None, None]
    shift = (jnp.arange(2 * c)[None, None, :] == c + t_idx - lag).astype(BF16)
    expand = (jnp.arange(LANES)[:, None] == jnp.arange(SSD_W)[None, :] // SSD_HEAD_DIM).astype(BF16)
    full = lambda shape: pl.BlockSpec(shape, lambda b, j: (0,) * len(shape))
    blk = lambda w: pl.BlockSpec((c, w), lambda b, j: (b * nc + j, 0))
    return pl.pallas_call(
        _ssd_kernel,
        grid=(batch, nc),
        in_specs=[blk(SSD_W), blk(SSD_XBC_W), blk(LANES), full((SSD_CONV - 1, c, 2 * c)),
                  full((LANES, SSD_W)), full((SSD_CONV, SSD_XBC_W)), full((1, SSD_XBC_W)),
                  full((1, LANES)), full((1, LANES)), full((1, SSD_W)), full((1, SSD_W))],
        out_specs=blk(SSD_W),
        out_shape=jax.ShapeDtypeStruct((batch * seq, SSD_W), BF16),
        scratch_shapes=[pltpu.VMEM((2 * c, SSD_XBC_W), BF16),
                        pltpu.VMEM((SSD_GROUPS, SSD_STATE, SSD_W // SSD_GROUPS), F32)],
        compiler_params=_cparams("parallel", "arbitrary"),
        name="ssd",
    )(z, xbc, dt, shift, expand, conv_w, conv_b[None, :], pad(dt_bias), a_neg, d_wide, norm_g[None, :])


def _s5_kernel(u_ref, t1_ref, pre_ref, pim_ref, qre_ref, qim_ref, are_ref, aim_ref, y_ref,
               ere_ref, eim_ref, xre_ref, xim_ref, sre_ref, sim_ref):
    batch, tb, _ = u_ref.shape
    cs = S5_CHUNK
    ns = tb // cs

    @pl.when(pl.program_id(1) == 0)
    def _():
        sre_ref[...] = jnp.zeros_like(sre_ref)
        sim_ref[...] = jnp.zeros_like(sim_ref)

    u = jnp.concatenate(
        [jnp.concatenate([u_ref[b, pl.ds(s, ns, stride=cs), :] for s in range(cs)], axis=-1)
         for b in range(batch)], axis=0).astype(BF16)
    n_tiles = ere_ref.shape[0]
    lanes_of = lambda j: slice(j * LANES, (j + 1) * LANES)
    e_re = _dot(u, pre_ref[0])
    e_im = _dot(u, pim_ref[0])
    for j in range(n_tiles):
        ere_ref[j] = e_re[:, lanes_of(j)]
        eim_ref[j] = e_im[:, lanes_of(j)]
    shape = (batch, LANES)
    ar = [jnp.broadcast_to(are_ref[0, :, lanes_of(j)], shape) for j in range(n_tiles)]
    ai = [jnp.broadcast_to(aim_ref[0, :, lanes_of(j)], shape) for j in range(n_tiles)]

    def step(n, carry):
        rows = pl.ds(n, batch, stride=ns)
        out = []
        for j in range(n_tiles):
            xr, xi = carry[j]
            xre_ref[j, rows, :] = xr
            xim_ref[j, rows, :] = xi
            out.append((ar[j] * xr - ai[j] * xi + ere_ref[j, rows, :],
                        ar[j] * xi + ai[j] * xr + eim_ref[j, rows, :]))
        return tuple(out)

    init = tuple((sre_ref[j], sim_ref[j]) for j in range(n_tiles))
    final = lax.fori_loop(0, ns, step, init)
    for j in range(n_tiles):
        sre_ref[j], sim_ref[j] = final[j]
    x_re = jnp.concatenate([xre_ref[j] for j in range(n_tiles)], axis=-1).astype(BF16)
    x_im = jnp.concatenate([xim_ref[j] for j in range(n_tiles)], axis=-1).astype(BF16)
    y = _dot(u, t1_ref[0]) + _dot(x_re, qre_ref[0]) + _dot(x_im, qim_ref[0])
    for b in range(batch):
        for s in range(cs):
            y_ref[b, pl.ds(s, ns, stride=cs), :] = y[b * ns:(b + 1) * ns, s * LANES:(s + 1) * LANES]


def _s5_operators(a_re, a_im, b_re, b_im, c_re, c_im, log_step):
    cs = S5_CHUNK
    ein = functools.partial(jnp.einsum, precision=HIGHEST)
    delta = jnp.exp(log_step)[:, None]
    ar, ai = a_re, a_im
    mag = jnp.exp(ar * delta)
    ang = ai * delta
    lr, li = mag * jnp.cos(ang), mag * jnp.sin(ang)
    den = ar * ar + ai * ai
    nr, ni = lr - 1.0, li
    cr = (nr * ar + ni * ai) / den
    ci = (ni * ar - nr * ai) / den
    bbr = cr[..., None] * b_re - ci[..., None] * b_im
    bbi = cr[..., None] * b_im + ci[..., None] * b_re
    k = jnp.arange(cs + 1, dtype=F32)
    pmag = jnp.exp((ar * delta)[..., None] * k)
    pang = ang[..., None] * k
    pr, pi = pmag * jnp.cos(pang), pmag * jnp.sin(pang)
    clr = c_re[..., None] * pr[:, None] - c_im[..., None] * pi[:, None]
    cli = c_re[..., None] * pi[:, None] + c_im[..., None] * pr[:, None]
    kern = ein('gcpk,gpd->gkcd', clr, bbr) - ein('gcpk,gpd->gkcd', cli, bbi)
    kern = jnp.concatenate([kern[:, :cs], jnp.zeros_like(kern[:, :1])], axis=1)
    s = jnp.arange(cs)
    lag = jnp.where(s[None, :] >= s[:, None], s[None, :] - s[:, None], cs)
    nh, ng = S5_HALVES, S5_LANE_GROUPS
    eye = jnp.eye(ng, dtype=F32)
    halves = lambda t: t.reshape((nh, ng) + t.shape[1:])
    bd = halves(kern).transpose(0, 2, 1, 4, 3)
    bd = (bd[:, :, :, :, None, :] * eye[None, None, :, None, :, None]).reshape(nh, cs + 1, LANES, LANES)
    t1 = bd.astype(BF16)[:, lag].transpose(0, 1, 3, 2, 4).reshape(nh, cs * LANES, cs * LANES)
    rev = cs - 1 - s
    prr, pri = pr[..., rev], pi[..., rev]
    p_re = prr[..., None] * bbr[:, :, None] - pri[..., None] * bbi[:, :, None]
    p_im = prr[..., None] * bbi[:, :, None] + pri[..., None] * bbr[:, :, None]

    def flat_p(t):
        t = halves(t.transpose(0, 2, 3, 1)).transpose(0, 2, 1, 3, 4)
        t = t[:, :, :, :, None, :] * eye[None, None, :, None, :, None]
        return t.reshape(nh, cs * LANES, ng * S5_STATE).astype(BF16)

    def flat_q(t):
        t = halves(t.transpose(0, 2, 3, 1))
        t = t[:, :, :, :, None, :] * eye[None, :, None, None, :, None]
        return t.reshape(nh, ng * S5_STATE, cs * LANES).astype(BF16)

    a_chunk_re = pr[..., cs].reshape(nh, 1, ng * S5_STATE)
    a_chunk_im = pi[..., cs].reshape(nh, 1, ng * S5_STATE)
    return (t1, flat_p(p_re), flat_p(p_im), flat_q(clr[..., 1:]), flat_q(-cli[..., 1:]),
            a_chunk_re, a_chunk_im)


def _s5(u, ops, batch, seq):
    tb = S5_CHUNK * S5_BLOCK_STEPS
    rows = batch * S5_BLOCK_STEPS
    flat = S5_CHUNK * LANES
    nstate = S5_LANE_GROUPS * S5_STATE
    per_h = lambda a, b: pl.BlockSpec((1, a, b), lambda h, j: (h, 0, 0))
    seq_blk = pl.BlockSpec((batch, tb, LANES), lambda h, j: (0, j, h))
    y = pl.pallas_call(
        _s5_kernel,
        grid=(S5_HALVES, seq // tb),
        in_specs=[seq_blk, per_h(flat, flat), per_h(flat, nstate), per_h(flat, nstate),
                  per_h(nstate, flat), per_h(nstate, flat), per_h(1, nstate), per_h(1, nstate)],
        out_specs=seq_blk,
        out_shape=jax.ShapeDtypeStruct((batch, seq, S5_W), F32),
        scratch_shapes=([pltpu.VMEM((nstate // LANES, rows, LANES), F32)] * 4
                        + [pltpu.VMEM((nstate // LANES, batch, LANES), F32)] * 2),
        compiler_params=_cparams("parallel", "arbitrary"),
        name="s5",
    )(u.reshape(batch, seq, S5_W), *ops)
    return y.reshape(batch * seq, S5_W)


_GROUP_LANE0 = N_EXPERTS
_INFO_ROWS = 8


def _mixer_out_proj(h, out_r, out_m, y_s, u, d_s5, w_glu, b_glu, w_out):
    y = y_s + d_s5 * u
    g = jax.nn.gelu(y)
    s = g * jax.nn.sigmoid(_dot(g.astype(BF16), w_glu[...]) + b_glu)
    acc = _dot(out_r, w_out[0:RET_W, :])
    acc = acc + _dot(out_m, w_out[RET_W:RET_W + SSD_W, :])
    acc = acc + _dot(s.astype(BF16), w_out[RET_W + SSD_W:, :])
    return h + acc


def _cross_attention(h, g, wq_ref, k_ref, v_ref, wo_ref):
    d = h.shape[-1]
    dh = d // CROSS_HEADS
    q = _dot(_rms(h, g).astype(BF16), wq_ref[...]).astype(BF16)
    outs = []
    for i in range(CROSS_HEADS):
        sl = slice(i * dh, (i + 1) * dh)
        s = _dot_nt(q[:, sl], k_ref[:, sl]) * (dh ** -0.5)
        p = jnp.exp(s - jnp.max(s, axis=-1, keepdims=True))
        o = _dot(p.astype(BF16), v_ref[:, sl])
        outs.append(o / jnp.sum(p, axis=-1, keepdims=True))
    o = jnp.concatenate(outs, axis=-1).astype(BF16)
    return h + _dot(o, wo_ref[...])


def _route(xn, whi_ref, wlo_ref, bias, carry):
    x_hi = xn.astype(BF16)
    x_lo = (xn - x_hi.astype(F32)).astype(BF16)
    logits = _dot(x_hi, whi_ref[...]) + _dot(x_lo, whi_ref[...]) + _dot(x_hi, wlo_ref[...]) + bias
    tm = logits.shape[0]
    lane = lax.broadcasted_iota(jnp.int32, logits.shape, 1).astype(F32)
    neg = -jnp.inf

    def first_argmax(vals):
        m = jnp.max(vals, axis=-1, keepdims=True)
        return m, jnp.min(jnp.where(vals == m, lane, float(LANES)), axis=-1, keepdims=True)

    gl = jnp.where((lane >= _GROUP_LANE0) & (lane < _GROUP_LANE0 + MOE_GROUPS), logits, neg)
    gmax, glane = first_argmax(gl)
    pg = 1.0 / jnp.sum(jnp.exp(gl - gmax), axis=-1, keepdims=True)
    lo = (glane - _GROUP_LANE0) * EXPERTS_PER_GROUP
    el = jnp.where((lane >= lo) & (lane < lo + EXPERTS_PER_GROUP), logits, neg)
    m1, e1 = first_argmax(el)
    m2, e2 = first_argmax(jnp.where(lane == e1, neg, el))
    p2 = jnp.exp(m2 - m1)
    gate1 = pg / (1.0 + p2)
    gate2 = pg * p2 / (1.0 + p2)

    hot = jnp.where((lane == e1) | (lane == e2), 1.0, 0.0)
    row = lax.broadcasted_iota(jnp.int32, (tm, tm), 0)
    col = lax.broadcasted_iota(jnp.int32, (tm, tm), 1)
    before = jnp.where(row > col, 1.0, 0.0).astype(BF16)
    cum = _dot(before, hot.astype(BF16)) + carry
    rank1 = jnp.sum(jnp.where(lane == e1, cum, 0.0), axis=-1, keepdims=True)
    rank2 = jnp.sum(jnp.where(lane == e2, cum, 0.0), axis=-1, keepdims=True)

    info = jnp.zeros(logits.shape, F32)
    for i, val in enumerate((e1, e2, rank1, rank2, gate1, gate2)):
        info = jnp.where(lane == i, val, info)
    return info, carry + jnp.sum(hot, axis=0, keepdims=True)


def _post_mixer_kernel(h_ref, r_ref, m_ref, ys_ref, u_ref, d_ref, wg_ref, bg_ref, wo_ref,
                       gc_ref, wq_ref, k_ref, v_ref, wco_ref, gf_ref, whi_ref, wlo_ref, br_ref,
                       o_ref, info_ref, infot_ref, cnt_ref, carry_ref):
    @pl.when(pl.program_id(0) == 0)
    def _():
        carry_ref[...] = jnp.zeros_like(carry_ref)

    h = _mixer_out_proj(h_ref[...], r_ref[...], m_ref[...], ys_ref[...], u_ref[...], d_ref[...], wg_ref,
                        bg_ref[...], wo_ref)
    h = _cross_attention(h, gc_ref[...], wq_ref, k_ref, v_ref, wco_ref)
    _store_token_tiles(o_ref, h)
    info, carry = _route(_rms(h, gf_ref[...]), whi_ref, wlo_ref, br_ref[...], carry_ref[...])
    carry_ref[...] = carry
    cnt_ref[...] = carry
    info_ref[...] = info
    infot_ref[...] = info.T[0:_INFO_ROWS, :]


def _post_mixer(h, out_r, out_m, y_s, u, d_s5, w_glu, b_glu, w_out, g_cross, wq, kv, layer, w_co,
                g_ffn, w_r, b_r, seq, mem_len):
    t, d = h.shape
    nt = d // LANES
    tm = ROW_TILE
    tiles_per_seq = seq // tm
    w_hi = w_r.astype(BF16)
    w_lo = (w_r - w_hi.astype(F32)).astype(BF16)
    row = lambda w: pl.BlockSpec((tm, w), lambda i: (i, 0))
    full = lambda a, b: pl.BlockSpec((a, b), lambda i: (0, 0))
    mem_blk = lambda col: pl.BlockSpec((mem_len, d), lambda i: (i // tiles_per_seq, col))
    return pl.pallas_call(
        _post_mixer_kernel,
        grid=(t // tm,),
        in_specs=[row(d), row(RET_W), row(SSD_W), row(S5_W), row(S5_W),
                  full(1, S5_W), full(S5_W, S5_W), full(1, S5_W), full(d, d),
                  full(1, d), full(d, d), mem_blk(2 * layer), mem_blk(2 * layer + 1), full(d, d),
                  full(1, d), full(d, LANES), full(d, LANES), full(1, LANES)],
        out_specs=[pl.BlockSpec((tm * nt, LANES), lambda i: (i, 0)),
                   pl.BlockSpec((tm, LANES), lambda i: (i, 0)),
                   pl.BlockSpec((_INFO_ROWS, tm), lambda i: (0, i)),
                   pl.BlockSpec((1, LANES), lambda i: (0, 0))],
        out_shape=[jax.ShapeDtypeStruct((t * nt, LANES), F32), jax.ShapeDtypeStruct((t, LANES), F32),
                   jax.ShapeDtypeStruct((_INFO_ROWS, t), F32), jax.ShapeDtypeStruct((1, LANES), F32)],
        scratch_shapes=[pltpu.VMEM((1, LANES), F32)],
        compiler_params=_cparams("arbitrary"),
        name="post_mixer",
    )(h, out_r, out_m, y_s, u, d_s5, w_glu, b_glu, w_out, g_cross, wq, kv, kv, w_co, g_ffn, w_hi, w_lo, b_r)


def _token_copy(src_hbm, dst_vmem, src_row, dst_token, nt, sem):
    return pltpu.make_async_copy(src_hbm.at[pl.ds(pl.multiple_of(src_row, nt), nt), :],
                                 dst_vmem.at[pl.ds(dst_token * nt, nt), :], sem)


_ISSUE_UNROLL = 8


def _start_token_gather(src_hbm, idx_ref, n_tokens, dst, sem):
    nt = dst.shape[0] // n_tokens

    def body(j, carry):
        for p in range(2):
            r = 2 * j + p
            _token_copy(src_hbm, dst, idx_ref[0, 0, r], r, nt, sem).start(priority=p)
        return carry

    lax.fori_loop(0, n_tokens // 2, body, 0, unroll=_ISSUE_UNROLL)


def _wait_token_gather(src_hbm, dst, sem):
    pltpu.make_async_copy(src_hbm.at[pl.ds(0, dst.shape[0]), :], dst, sem).wait()


def _expert_kernel(be_ref, nu_ref, cur_ref, nxt_ref, h_hbm, g_ref, wg_ref, wu_ref, wd_ref, y_ref,
                   xbuf, wg_s, wu_s, wd_s, sem):
    i = pl.program_id(0)
    n_used = nu_ref[0]
    slot = i % 2
    used = i < n_used

    @pl.when((i == 0) & used)
    def _():
        _start_token_gather(h_hbm, cur_ref, MOE_BLOCK, xbuf.at[0], sem.at[0])

    @pl.when(i + 1 < n_used)
    def _():
        _start_token_gather(h_hbm, nxt_ref, MOE_BLOCK, xbuf.at[1 - slot], sem.at[1 - slot])

    @pl.when(used & ((i == 0) | (be_ref[i] != be_ref[jnp.maximum(i - 1, 0)])))
    def _():
        wg_s[...] = wg_ref[0, 0].astype(BF16)
        wu_s[...] = wu_ref[0, 0].astype(BF16)
        wd_s[...] = wd_ref[0, 0].astype(BF16)

    @pl.when(used)
    def _():
        _wait_token_gather(h_hbm, xbuf.at[slot], sem.at[slot])
        xn = _rms(_load_token_tiles(xbuf.at[slot], MOE_BLOCK), g_ref[...]).astype(BF16)
        hid = (_silu(_dot(xn, wg_s[...])) * _dot(xn, wu_s[...])).astype(BF16)
        _store_token_tiles(y_ref, _dot(hid, wd_s[...]))

    @pl.when(jnp.logical_not(used))
    def _():
        y_ref[...] = jnp.zeros_like(y_ref)


def _experts(h_tiles, g, block_e, n_used, src_row, w_gate, w_up, w_down, layer):
    d = w_gate.shape[-2]
    nt = d // LANES
    nb = block_e.shape[0]
    de = w_gate.shape[-1]
    idx_blk = lambda f: pl.BlockSpec((1, 1, MOE_BLOCK), lambda i, be, nu: (f(i), 0, 0),
                                     memory_space=pltpu.SMEM)
    grid_spec = pltpu.PrefetchScalarGridSpec(
        num_scalar_prefetch=2,
        grid=(nb,),
        in_specs=[idx_blk(lambda i: i), idx_blk(lambda i: jnp.minimum(i + 1, nb - 1)),
                  pl.BlockSpec(memory_space=pl.ANY),
                  pl.BlockSpec((1, d), lambda i, be, nu: (0, 0)),
                  pl.BlockSpec((1, 1, d, de), lambda i, be, nu: (layer, be[i], 0, 0)),
                  pl.BlockSpec((1, 1, d, de), lambda i, be, nu: (layer, be[i], 0, 0)),
                  pl.BlockSpec((1, 1, de, d), lambda i, be, nu: (layer, be[i], 0, 0))],
        out_specs=pl.BlockSpec((MOE_BLOCK * nt, LANES), lambda i, be, nu: (i, 0)),
        scratch_shapes=[pltpu.VMEM((2, MOE_BLOCK * nt, LANES), F32), pltpu.VMEM((d, de), BF16),
                        pltpu.VMEM((d, de), BF16), pltpu.VMEM((de, d), BF16),
                        pltpu.SemaphoreType.DMA((2,))],
    )
    src3 = src_row.reshape(nb, 1, MOE_BLOCK)
    return pl.pallas_call(
        _expert_kernel,
        grid_spec=grid_spec,
        out_shape=jax.ShapeDtypeStruct((nb * MOE_BLOCK * nt, LANES), F32),
        compiler_params=_cparams("arbitrary"),
        name="moe_experts",
    )(block_e, n_used, src3, src3, h_tiles, g, w_gate, w_up, w_down)


def _combine_kernel(dest_ref, h_ref, info_ref, y_hbm, fg_ref, o_ref, ybuf, sem, *, final_norm):
    tm = info_ref.shape[0]
    nt = h_ref.shape[0] // tm

    def start(r, carry):
        for k in range(2):
            _token_copy(y_hbm, ybuf.at[k], dest_ref[0, 0, k * tm + r], r, nt, sem.at[k]).start(priority=k)
        return carry

    lax.fori_loop(0, tm, start, 0, unroll=_ISSUE_UNROLL)
    for k in range(2):
        _wait_token_gather(y_hbm, ybuf.at[k], sem.at[k])
    info = info_ref[...]
    out = _load_token_tiles(h_ref, tm) + (info[:, 4:5] * _load_token_tiles(ybuf.at[0], tm)
                                          + info[:, 5:6] * _load_token_tiles(ybuf.at[1], tm))
    if final_norm:
        out = _rms(out, fg_ref[...])
    o_ref[...] = out


def _combine(h_tiles, info, dest, ybuf, final_g, final_norm):
    t = info.shape[0]
    d = final_g.shape[-1]
    nt = d // LANES
    tm = MOE_ROW_TILE
    return pl.pallas_call(
        functools.partial(_combine_kernel, final_norm=final_norm),
        grid=(t // tm,),
        in_specs=[pl.BlockSpec((1, 1, 2 * tm), lambda i: (i, 0, 0), memory_space=pltpu.SMEM),
                  pl.BlockSpec((tm * nt, LANES), lambda i: (i, 0)),
                  pl.BlockSpec((tm, LANES), lambda i: (i, 0)),
                  pl.BlockSpec(memory_space=pl.ANY),
                  pl.BlockSpec((1, d), lambda i: (0, 0))],
        out_specs=pl.BlockSpec((tm, d), lambda i: (i, 0)),
        out_shape=jax.ShapeDtypeStruct((t, d), F32),
        scratch_shapes=[pltpu.VMEM((2, tm * nt, LANES), F32), pltpu.SemaphoreType.DMA((2,))],
        compiler_params=_cparams("arbitrary"),
        name="moe_combine",
    )(dest.reshape(2, t // tm, tm).transpose(1, 0, 2).reshape(t // tm, 1, 2 * tm), h_tiles, info, ybuf, final_g)


def _router_params(w_rg, b_rg, w_re, b_re):
    d = w_rg.shape[0]
    pad_cols = LANES - N_EXPERTS - MOE_GROUPS
    w_r = jnp.concatenate([w_re, w_rg, jnp.zeros((d, pad_cols), F32)], axis=1)
    b_r = jnp.concatenate([b_re, b_rg, jnp.zeros((pad_cols,), F32)])[None, :]
    return w_r, b_r


def _moe(h_tiles, routing, g, w_gate, w_up, w_down, layer, final_g, final_norm):
    info, infot, cnt = routing
    d = g.shape[-1]
    nt = d // LANES
    t = h_tiles.shape[0] // nt
    expert = infot[0:2].astype(jnp.int32)
    rank = infot[2:4].astype(jnp.int32)
    counts = cnt[0, :N_EXPERTS].astype(jnp.int32)
    padded = (counts + MOE_BLOCK - 1) // MOE_BLOCK * MOE_BLOCK
    pends = jnp.cumsum(padded)
    pstarts = pends - padded
    starts = jnp.cumsum(counts) - counts
    ids = jnp.arange(N_EXPERTS, dtype=jnp.int32)
    dest = jnp.sum(jnp.where(expert[..., None] == ids, pstarts, 0), axis=-1) + rank
    nb = (2 * t) // MOE_BLOCK + N_EXPERTS
    blk = jnp.arange(nb, dtype=jnp.int32)
    block_e = jnp.minimum(jnp.sum(pends[None, :] <= blk[:, None] * MOE_BLOCK, axis=1), N_EXPERTS - 1)
    block_e = block_e.astype(jnp.int32)
    n_used = (pends[-1:] // MOE_BLOCK).astype(jnp.int32)
    keys = expert * t + jnp.arange(t, dtype=jnp.int32)[None, :]
    sorted_tok = jnp.sort(keys.reshape(-1)) % t
    shift = jnp.sum(jnp.where(block_e[:, None] == ids, pstarts - starts, 0), axis=-1)
    pair = blk[:, None] * MOE_BLOCK + jnp.arange(MOE_BLOCK, dtype=jnp.int32)[None, :] - shift[:, None]
    src_tok = sorted_tok[jnp.clip(pair, 0, 2 * t - 1)]
    ybuf = _experts(h_tiles, g, block_e, n_used, src_tok * nt, w_gate, w_up, w_down, layer)
    return _combine(h_tiles, info, dest * nt, ybuf, final_g, final_norm)


def kernel(x, mem, norm_mix_g, w_in, ret_gn_g, ssd_conv_w, ssd_conv_b, ssd_dt_bias, ssd_A_log, ssd_D,
           ssd_norm_g, s5_A_re, s5_A_im, s5_B_re, s5_B_im, s5_C_re, s5_C_im, s5_log_step, s5_D, s5_w_glu,
           s5_b_glu, w_out, norm_cross_g, mem_norm_g, w_cq, w_ck, w_cv, w_co, norm_ffn_g, w_route_group,
           b_route_group, w_route_expert, b_route_expert, w_gate, w_up, w_down, norm_final_g):
    batch, seq, d = x.shape
    depth = w_in.shape[0]
    mem_len = mem.shape[1]
    t = batch * seq
    assert d == RET_W * 4 and t % ROW_TILE == 0 and seq % ROW_TILE == 0
    assert seq % RET_CHUNK == 0 and seq % SSD_CHUNK == 0 and seq % (S5_CHUNK * S5_BLOCK_STEPS) == 0
    assert t % MOE_BLOCK == 0 and t % MOE_ROW_TILE == 0
    row = lambda v: v[None, :]
    h = x.reshape(t, d)

    w_kv = jnp.concatenate([w for i in range(depth) for w in (w_ck[i], w_cv[i])], axis=1).astype(BF16)
    kv = _norm_matmul(mem.reshape(batch * mem_len, d), row(mem_norm_g), w_kv, BF16,
                      tm=mem_len, tn=d)
    ret_tables = _retention_tables(seq)
    c0 = _QKVG_W + SSD_W + SSD_XBC_W
    c1 = c0 + SSD_HEADS

    for i in range(depth):
        w_pack = jnp.concatenate(
            [w_in[i][:, :c0], w_in[i][:, c1:], w_in[i][:, c0:c1], jnp.zeros((d, LANES - SSD_HEADS), F32)],
            axis=1).astype(BF16)
        qkvg, z, xbc, u, dt = _in_proj(h, row(norm_mix_g[i]), w_pack)
        out_r = _retention(qkvg, ret_tables, row(ret_gn_g[i]), batch, seq)
        out_m = _ssd(z, xbc, dt, ssd_conv_w[i], ssd_conv_b[i], ssd_dt_bias[i], ssd_A_log[i], ssd_D[i],
                     ssd_norm_g[i], batch, seq)
        s5_ops = _s5_operators(s5_A_re[i], s5_A_im[i], s5_B_re[i], s5_B_im[i], s5_C_re[i], s5_C_im[i],
                               s5_log_step[i])
        y_s = _s5(u, s5_ops, batch, seq)
        w_r, b_r = _router_params(w_route_group[i], b_route_group[i], w_route_expert[i], b_route_expert[i])
        h_tiles, *routing = _post_mixer(
            h, out_r, out_m, y_s, u, row(s5_D[i]), s5_w_glu[i].astype(BF16), row(s5_b_glu[i]),
            w_out[i].astype(BF16), row(norm_cross_g[i]), w_cq[i].astype(BF16), kv, i, w_co[i].astype(BF16),
            row(norm_ffn_g[i]), w_r, b_r, seq, mem_len)
        h = _moe(h_tiles, routing, row(norm_ffn_g[i]), w_gate, w_up, w_down, i, row(norm_final_g),
                 final_norm=(i == depth - 1))
    return h.reshape(batch, seq, d)
```

```python
import functools
import math

import jax
import jax.numpy as jnp
from jax import lax
from jax.experimental import pallas as pl
from jax.experimental.pallas import tpu as pltpu

F32 = jnp.float32
BF16 = jnp.bfloat16
HIGHEST = lax.Precision.HIGHEST

EPS = 1e-6
RET_HEADS = 4
RET_HEAD_DIM = 64
RET_W = RET_HEADS * RET_HEAD_DIM
ROPE_BASE = 10000.0
SSD_HEAD_DIM = 64
SSD_HEADS = 8
SSD_GROUPS = 2
SSD_STATE = 128
SSD_CONV = 4
SSD_W = SSD_HEADS * SSD_HEAD_DIM
SSD_XBC_W = SSD_W + 2 * SSD_GROUPS * SSD_STATE
S5_GROUP = 16
S5_GROUPS = 16
S5_STATE = 64
S5_W = S5_GROUP * S5_GROUPS
CROSS_HEADS = 4
MOE_GROUPS = 4
EXPERTS_PER_GROUP = 8
N_EXPERTS = MOE_GROUPS * EXPERTS_PER_GROUP

LANES = 128
ROW_TILE = 512
RET_CHUNK = 256
SSD_CHUNK = 128
S5_CHUNK = 8
S5_LANE_GROUPS = LANES // S5_GROUP
S5_HALVES = S5_W // LANES
S5_BLOCK_STEPS = 64
MOE_BLOCK = 256
MOE_ROW_TILE = 256
VMEM_LIMIT = 48 * 1024 * 1024


def _cparams(*sem):
    return pltpu.CompilerParams(dimension_semantics=sem, vmem_limit_bytes=VMEM_LIMIT)


def _rms(x, g):
    return x * lax.rsqrt(jnp.mean(x * x, axis=-1, keepdims=True) + EPS) * g


def _silu(x):
    return x * jax.nn.sigmoid(x)


def _dot(a, b):
    return jnp.dot(a, b, preferred_element_type=F32)


def _dot_nt(a, b):
    return lax.dot_general(a, b, (((1,), (1,)), ((), ())), preferred_element_type=F32)


def _dot_tn(a, b):
    return lax.dot_general(a, b, (((0,), (0,)), ((), ())), preferred_element_type=F32)


def _load_token_tiles(ref, n_rows):
    nt = ref.shape[0] // n_rows
    return jnp.concatenate([ref[pl.ds(j, n_rows, stride=nt), :] for j in range(nt)], axis=-1)


def _store_token_tiles(ref, x):
    n_rows = x.shape[0]
    nt = ref.shape[0] // n_rows
    for j in range(nt):
        ref[pl.ds(j, n_rows, stride=nt), :] = x[:, j * LANES:(j + 1) * LANES]


def _norm_matmul_kernel(x_ref, g_ref, w_ref, o_ref):
    xn = _rms(x_ref[...], g_ref[...]).astype(BF16)
    o_ref[...] = _dot(xn, w_ref[...]).astype(o_ref.dtype)


def _norm_matmul(x, g, w, out_dtype, tm, tn):
    m, d = x.shape
    n = w.shape[1]
    return pl.pallas_call(
        _norm_matmul_kernel,
        grid=(m // tm, n // tn),
        in_specs=[pl.BlockSpec((tm, d), lambda i, j: (i, 0)),
                  pl.BlockSpec((1, d), lambda i, j: (0, 0)),
                  pl.BlockSpec((d, tn), lambda i, j: (0, j))],
        out_specs=pl.BlockSpec((tm, tn), lambda i, j: (i, j)),
        out_shape=jax.ShapeDtypeStruct((m, n), out_dtype),
        compiler_params=_cparams("parallel", "parallel"),
        name="norm_matmul",
    )(x, g, w)


_QKVG_W = 4 * RET_W
_IN_SPLITS = (_QKVG_W, SSD_W, SSD_XBC_W, S5_W, LANES)


def _in_proj_kernel(h_ref, g_ref, w_ref, qkvg_ref, z_ref, xbc_ref, u_ref, dt_ref):
    xn = _rms(h_ref[...], g_ref[...]).astype(BF16)
    lo = 0
    for ref, width in zip((qkvg_ref, z_ref, xbc_ref, u_ref, dt_ref), _IN_SPLITS):
        ref[...] = _dot(xn, w_ref[:, lo:lo + width]).astype(ref.dtype)
        lo += width


def _in_proj(h, g, w_pack):
    t, d = h.shape
    tm = ROW_TILE
    n = w_pack.shape[1]
    dts = (BF16, BF16, BF16, F32, F32)
    return pl.pallas_call(
        _in_proj_kernel,
        grid=(t // tm,),
        in_specs=[pl.BlockSpec((tm, d), lambda i: (i, 0)),
                  pl.BlockSpec((1, d), lambda i: (0, 0)),
                  pl.BlockSpec((d, n), lambda i: (0, 0))],
        out_specs=[pl.BlockSpec((tm, w), lambda i: (i, 0)) for w in _IN_SPLITS],
        out_shape=[jax.ShapeDtypeStruct((t, w), dt) for w, dt in zip(_IN_SPLITS, dts)],
        compiler_params=_cparams("parallel"),
        name="in_proj",
    )(h, g, w_pack)


def _retention_kernel(qkvg_ref, cos_ref, sin_ref, decay_ref, qdec_ref, kdec_ref, cdec_ref, gn_ref,
                      out_ref, s_ref):
    @pl.when(pl.program_id(1) == 0)
    def _():
        s_ref[...] = jnp.zeros_like(s_ref)

    x = qkvg_ref[...]
    w = RET_W
    q = x[:, 0:w].astype(F32)
    k = x[:, w:2 * w].astype(F32)
    v = x[:, 2 * w:3 * w]
    g = x[:, 3 * w:4 * w].astype(F32)
    half = RET_HEAD_DIM // 2
    lane = lax.broadcasted_iota(jnp.int32, q.shape, 1)
    first_half = (lane % RET_HEAD_DIM) < half

    def rot(t):
        swapped = jnp.where(first_half, pltpu.roll(t, w - half, 1), pltpu.roll(t, half, 1))
        return t * cos_ref[...] + swapped * sin_ref[...]

    qr = rot(q)
    kr = rot(k) * (RET_HEAD_DIM ** -0.5)
    qb = qr.astype(BF16)
    kb = kr.astype(BF16)
    qd = (qr * qdec_ref[...]).astype(BF16)
    kd = (kr * kdec_ref[...]).astype(BF16)
    outs = []
    for h in range(RET_HEADS):
        sl = slice(h * RET_HEAD_DIM, (h + 1) * RET_HEAD_DIM)
        s = _dot_nt(qb[:, sl], kb[:, sl]) * decay_ref[h]
        state = s_ref[h]
        y = _dot(s.astype(BF16), v[:, sl]) + _dot(qd[:, sl], state.astype(BF16))
        s_ref[h] = state * cdec_ref[h] + _dot_tn(kd[:, sl], v[:, sl])
        outs.append(y * lax.rsqrt(jnp.mean(y * y, axis=-1, keepdims=True) + EPS))
    yr = jnp.concatenate(outs, axis=-1)
    out_ref[...] = (_silu(g) * (yr * gn_ref[...])).astype(out_ref.dtype)


def _retention_tables(seq):
    c = RET_CHUNK
    dh = RET_HEAD_DIM
    inv = ROPE_BASE ** (-jnp.arange(0, dh, 2, dtype=F32) / dh)
    ang = jnp.arange(seq, dtype=F32)[:, None] * inv[None, :]
    cos, sin = jnp.cos(ang), jnp.sin(ang)
    cos4 = jnp.tile(jnp.concatenate([cos, cos], axis=-1), (1, RET_HEADS))
    sin4 = jnp.tile(jnp.concatenate([-sin, sin], axis=-1), (1, RET_HEADS))
    lg = jnp.log1p(-(2.0 ** (-5.0 - jnp.arange(RET_HEADS, dtype=F32))))
    i = jnp.arange(c, dtype=F32)
    rel = i[:, None] - i[None, :]
    decay = jnp.where(rel[None] >= 0, jnp.exp(lg[:, None, None] * jnp.maximum(rel, 0.0)[None]), 0.0)
    per_head = lambda t: jnp.repeat(t.T, dh, axis=1)
    qdec = per_head(jnp.exp(lg[:, None] * (i + 1.0)[None]))
    kdec = per_head(jnp.exp(lg[:, None] * (c - 1.0 - i)[None]))
    cdec = jnp.broadcast_to(jnp.exp(lg * c)[:, None, None], (RET_HEADS, dh, dh))
    return cos4, sin4, decay, qdec, kdec, cdec


def _retention(qkvg, tables, gn, batch, seq):
    c = RET_CHUNK
    nc = seq // c
    cos4, sin4, decay, qdec, kdec, cdec = tables
    w = RET_W
    full = lambda shape: pl.BlockSpec(shape, lambda b, j: (0,) * len(shape))
    return pl.pallas_call(
        _retention_kernel,
        grid=(batch, nc),
        in_specs=[pl.BlockSpec((c, _QKVG_W), lambda b, j: (b * nc + j, 0)),
                  pl.BlockSpec((c, w), lambda b, j: (j, 0)),
                  pl.BlockSpec((c, w), lambda b, j: (j, 0)),
                  full((RET_HEADS, c, c)), full((c, w)), full((c, w)),
                  full((RET_HEADS, RET_HEAD_DIM, RET_HEAD_DIM)), full((1, w))],
        out_specs=pl.BlockSpec((c, w), lambda b, j: (b * nc + j, 0)),
        out_shape=jax.ShapeDtypeStruct((batch * seq, w), BF16),
        scratch_shapes=[pltpu.VMEM((RET_HEADS, RET_HEAD_DIM, RET_HEAD_DIM), F32)],
        compiler_params=_cparams("parallel", "arbitrary"),
        name="retention",
    )(qkvg, cos4, sin4, decay, qdec, kdec, cdec, gn)


def _split3(v):
    hi = v.astype(BF16)
    rest = v - hi.astype(F32)
    mid = rest.astype(BF16)
    return hi, mid, (rest - mid.astype(F32)).astype(BF16)


def _ssd_kernel(z_ref, xbc_ref, dt_ref, shift_ref, expand_ref, cw_ref, cb_ref, dtb_ref, a_ref, d_ref,
                ng_ref, out_ref, xcat_ref, s_ref):
    c = SSD_CHUNK
    p = SSD_HEAD_DIM
    n = SSD_STATE

    @pl.when(pl.program_id(1) == 0)
    def _():
        xcat_ref[0:c, :] = jnp.zeros((c, SSD_XBC_W), BF16)
        s_ref[...] = jnp.zeros_like(s_ref)

    x_cur = xbc_ref[...]
    xcat_ref[c:2 * c, :] = x_cur
    x_cat = xcat_ref[...]
    conv = cb_ref[...] + x_cur.astype(F32) * cw_ref[SSD_CONV - 1:SSD_CONV, :]
    for j in range(SSD_CONV - 1):
        conv = conv + _dot(shift_ref[j], x_cat) * cw_ref[j:j + 1, :]
    xcat_ref[0:c, :] = x_cur
    act = _silu(conv)
    xs = act[:, :SSD_W]
    bm = act[:, SSD_W:SSD_W + SSD_GROUPS * n]
    cm = act[:, SSD_W + SSD_GROUPS * n:]

    dt_in = dt_ref[...] + dtb_ref[...]
    dt = jnp.maximum(dt_in, 0.0) + jnp.log1p(jnp.exp(-jnp.abs(dt_in)))
    row = lax.broadcasted_iota(jnp.int32, (c, c), 0)
    col = lax.broadcasted_iota(jnp.int32, (c, c), 1)
    causal = row >= col
    a_cum = jnp.dot(causal.astype(F32), dt * a_ref[...], precision=HIGHEST,
                    preferred_element_type=F32)
    a_cum_t = a_cum.T

    expand = expand_ref[...]
    widen = lambda v: sum(_dot(term, expand) for term in _split3(v))
    dt_w = widen(dt)
    a_w = widen(a_cum)
    a_last_w = a_w[c - 1:c, :]
    chunk_dec_w = jnp.exp(a_last_w)
    xdt = xs * dt_w
    xdt_b = xdt.astype(BF16)
    xdec_b = (xdt * jnp.exp(a_last_w - a_w)).astype(BF16)

    heads_per_group = SSD_HEADS // SSD_GROUPS
    gw = heads_per_group * p
    y_diag, y_off = [], []
    for g in range(SSD_GROUPS):
        bg = bm[:, g * n:(g + 1) * n].astype(BF16)
        cg = cm[:, g * n:(g + 1) * n].astype(BF16)
        cb = _dot_nt(cg, bg)
        gl = slice(g * gw, (g + 1) * gw)
        states = s_ref[g]
        y_off.append(_dot(cg, states.astype(BF16)))
        s_ref[g] = states * chunk_dec_w[:, gl] + _dot_tn(bg, xdec_b[:, gl])
        for r in range(heads_per_group):
            h = g * heads_per_group + r
            lmat = jnp.exp(jnp.where(causal, a_cum[:, h:h + 1] - a_cum_t[h:h + 1, :], -jnp.inf))
            y_diag.append(_dot((cb * lmat).astype(BF16), xdt_b[:, h * p:(h + 1) * p]))
    y = (jnp.concatenate(y_diag, axis=-1) + jnp.concatenate(y_off, axis=-1) * jnp.exp(a_w)
         + xs * d_ref[...])
    out_ref[...] = _rms(y * _silu(z_ref[...].astype(F32)), ng_ref[...]).astype(out_ref.dtype)


def _ssd(z, xbc, dt, conv_w, conv_b, dt_bias, a_log, d_skip, norm_g, batch, seq):
    c = SSD_CHUNK
    nc = seq // c
    pad = lambda v: jnp.pad(v, (0, LANES - v.shape[0]))[None, :]
    a_neg = pad(-jnp.exp(a_log))
    d_wide = jnp.repeat(d_skip, SSD_HEAD_DIM)[None, :]
    t_idx = jnp.arange(c)[None, :, None]
    lag = (SSD_CONV - 1 - jnp.arange(SSD_CONV - 1))[:, None, None]
    shift = (jnp.arange(2 * c)[None, None, :] == c + t_idx - lag).astype(BF16)
    expand = (jnp.arange(LANES)[:, None] == jnp.arange(SSD_W)[None, :] // SSD_HEAD_DIM).astype(BF16)
    full = lambda shape: pl.BlockSpec(shape, lambda b, j: (0,) * len(shape))
    blk = lambda w: pl.BlockSpec((c, w), lambda b, j: (b * nc + j, 0))
    return pl.pallas_call(
        _ssd_kernel,
        grid=(batch, nc),
        in_specs=[blk(SSD_W), blk(SSD_XBC_W), blk(LANES), full((SSD_CONV - 1, c, 2 * c)),
                  full((LANES, SSD_W)), full((SSD_CONV, SSD_XBC_W)), full((1, SSD_XBC_W)),
                  full((1, LANES)), full((1, LANES)), full((1, SSD_W)), full((1, SSD_W))],
        out_specs=blk(SSD_W),
        out_shape=jax.ShapeDtypeStruct((batch * seq, SSD_W), BF16),
        scratch_shapes=[pltpu.VMEM((2 * c, SSD_XBC_W), BF16),
                        pltpu.VMEM((SSD_GROUPS, SSD_STATE, SSD_W // SSD_GROUPS), F32)],
        compiler_params=_cparams("parallel", "arbitrary"),
        name="ssd",
    )(z, xbc, dt, shift, expand, conv_w, conv_b[None, :], pad(dt_bias), a_neg, d_wide, norm_g[None, :])


def _s5_kernel(u_ref, t1_ref, pre_ref, pim_ref, qre_ref, qim_ref, are_ref, aim_ref, y_ref,
               ere_ref, eim_ref, xre_ref, xim_ref, sre_ref, sim_ref):
    batch, tb, _ = u_ref.shape
    cs = S5_CHUNK
    ns = tb // cs

    @pl.when(pl.program_id(1) == 0)
    def _():
        sre_ref[...] = jnp.zeros_like(sre_ref)
        sim_ref[...] = jnp.zeros_like(sim_ref)

    u = jnp.concatenate(
        [jnp.concatenate([u_ref[b, pl.ds(s, ns, stride=cs), :] for s in range(cs)], axis=-1)
         for b in range(batch)], axis=0).astype(BF16)
    n_tiles = ere_ref.shape[0]
    lanes_of = lambda j: slice(j * LANES, (j + 1) * LANES)
    e_re = _dot(u, pre_ref[0])
    e_im = _dot(u, pim_ref[0])
    for j in range(n_tiles):
        ere_ref[j] = e_re[:, lanes_of(j)]
        eim_ref[j] = e_im[:, lanes_of(j)]
    shape = (batch, LANES)
    ar = [jnp.broadcast_to(are_ref[0, :, lanes_of(j)], shape) for j in range(n_tiles)]
    ai = [jnp.broadcast_to(aim_ref[0, :, lanes_of(j)], shape) for j in range(n_tiles)]

    def step(n, carry):
        rows = pl.ds(n, batch, stride=ns)
        out = []
        for j in range(n_tiles):
            xr, xi = carry[j]
            xre_ref[j, rows, :] = xr
            xim_ref[j, rows, :] = xi
            out.append((ar[j] * xr - ai[j] * xi + ere_ref[j, rows, :],
                        ar[j] * xi + ai[j] * xr + eim_ref[j, rows, :]))
        return tuple(out)

    init = tuple((sre_ref[j], sim_ref[j]) for j in range(n_tiles))
    final = lax.fori_loop(0, ns, step, init)
    for j in range(n_tiles):
        sre_ref[j], sim_ref[j] = final[j]
    x_re = jnp.concatenate([xre_ref[j] for j in range(n_tiles)], axis=-1).astype(BF16)
    x_im = jnp.concatenate([xim_ref[j] for j in range(n_tiles)], axis=-1).astype(BF16)
    y = _dot(u, t1_ref[0]) + _dot(x_re, qre_ref[0]) + _dot(x_im, qim_ref[0])
    for b in range(batch):
        for s in range(cs):
            y_ref[b, pl.ds(s, ns, stride=cs), :] = y[b * ns:(b + 1) * ns, s * LANES:(s + 1) * LANES]


def _s5_operators(a_re, a_im, b_re, b_im, c_re, c_im, log_step):
    cs = S5_CHUNK
    ein = functools.partial(jnp.einsum, precision=HIGHEST)
    delta = jnp.exp(log_step)[:, None]
    ar, ai = a_re, a_im
    mag = jnp.exp(ar * delta)
    ang = ai * delta
    lr, li = mag * jnp.cos(ang), mag * jnp.sin(ang)
    den = ar * ar + ai * ai
    nr, ni = lr - 1.0, li
    cr = (nr * ar + ni * ai) / den
    ci = (ni * ar - nr * ai) / den
    bbr = cr[..., None] * b_re - ci[..., None] * b_im
    bbi = cr[..., None] * b_im + ci[..., None] * b_re
    k = jnp.arange(cs + 1, dtype=F32)
    pmag = jnp.exp((ar * delta)[..., None] * k)
    pang = ang[..., None] * k
    pr, pi = pmag * jnp.cos(pang), pmag * jnp.sin(pang)
    clr = c_re[..., None] * pr[:, None] - c_im[..., None] * pi[:, None]
    cli = c_re[..., None] * pi[:, None] + c_im[..., None] * pr[:, None]
    kern = ein('gcpk,gpd->gkcd', clr, bbr) - ein('gcpk,gpd->gkcd', cli, bbi)
    kern = jnp.concatenate([kern[:, :cs], jnp.zeros_like(kern[:, :1])], axis=1)
    s = jnp.arange(cs)
    lag = jnp.where(s[None, :] >= s[:, None], s[None, :] - s[:, None], cs)
    nh, ng = S5_HALVES, S5_LANE_GROUPS
    eye = jnp.eye(ng, dtype=F32)
    halves = lambda t: t.reshape((nh, ng) + t.shape[1:])
    bd = halves(kern).transpose(0, 2, 1, 4, 3)
    bd = (bd[:, :, :, :, None, :] * eye[None, None, :, None, :, None]).reshape(nh, cs + 1, LANES, LANES)
    t1 = bd.astype(BF16)[:, lag].transpose(0, 1, 3, 2, 4).reshape(nh, cs * LANES, cs * LANES)
    rev = cs - 1 - s
    prr, pri = pr[..., rev], pi[..., rev]
    p_re = prr[..., None] * bbr[:, :, None] - pri[..., None] * bbi[:, :, None]
    p_im = prr[..., None] * bbi[:, :, None] + pri[..., None] * bbr[:, :, None]

    def flat_p(t):
        t = halves(t.transpose(0, 2, 3, 1)).transpose(0, 2, 1, 3, 4)
        t = t[:, :, :, :, None, :] * eye[None, None, :, None, :, None]
        return t.reshape(nh, cs * LANES, ng * S5_STATE).astype(BF16)

    def flat_q(t):
        t = halves(t.transpose(0, 2, 3, 1))
        t = t[:, :, :, :, None, :] * eye[None, :, None, None, :, None]
        return t.reshape(nh, ng * S5_STATE, cs * LANES).astype(BF16)

    a_chunk_re = pr[..., cs].reshape(nh, 1, ng * S5_STATE)
    a_chunk_im = pi[..., cs].reshape(nh, 1, ng * S5_STATE)
    return (t1, flat_p(p_re), flat_p(p_im), flat_q(clr[..., 1:]), flat_q(-cli[..., 1:]),
            a_chunk_re, a_chunk_im)


def _s5(u, ops, batch, seq):
    tb = S5_CHUNK * S5_BLOCK_STEPS
    rows = batch * S5_BLOCK_STEPS
    flat = S5_CHUNK * LANES
    nstate = S5_LANE_GROUPS * S5_STATE
    per_h = lambda a, b: pl.BlockSpec((1, a, b), lambda h, j: (h, 0, 0))
    seq_blk = pl.BlockSpec((batch, tb, LANES), lambda h, j: (0, j, h))
    y = pl.pallas_call(
        _s5_kernel,
        grid=(S5_HALVES, seq // tb),
        in_specs=[seq_blk, per_h(flat, flat), per_h(flat, nstate), per_h(flat, nstate),
                  per_h(nstate, flat), per_h(nstate, flat), per_h(1, nstate), per_h(1, nstate)],
        out_specs=seq_blk,
        out_shape=jax.ShapeDtypeStruct((batch, seq, S5_W), F32),
        scratch_shapes=([pltpu.VMEM((nstate // LANES, rows, LANES), F32)] * 4
                        + [pltpu.VMEM((nstate // LANES, batch, LANES), F32)] * 2),
        compiler_params=_cparams("parallel", "arbitrary"),
        name="s5",
    )(u.reshape(batch, seq, S5_W), *ops)
    return y.reshape(batch * seq, S5_W)


_GROUP_LANE0 = N_EXPERTS
_INFO_ROWS = 8


def _mixer_out_proj(h, out_r, out_m, y_s, u, d_s5, w_glu, b_glu, w_out):
    y = y_s + d_s5 * u
    g = jax.nn.gelu(y)
    s = g * jax.nn.sigmoid(_dot(g.astype(BF16), w_glu[...]) + b_glu)
    acc = _dot(out_r, w_out[0:RET_W, :])
    acc = acc + _dot(out_m, w_out[RET_W:RET_W + SSD_W, :])
    acc = acc + _dot(s.astype(BF16), w_out[RET_W + SSD_W:, :])
    return h + acc


def _cross_attention(h, g, wq_ref, k_ref, v_ref, wo_ref):
    d = h.shape[-1]
    dh = d // CROSS_HEADS
    q = _dot(_rms(h, g).astype(BF16), wq_ref[...]).astype(BF16)
    outs = []
    for i in range(CROSS_HEADS):
        sl = slice(i * dh, (i + 1) * dh)
        s = _dot_nt(q[:, sl], k_ref[:, sl]) * (dh ** -0.5)
        p = jnp.exp(s - jnp.max(s, axis=-1, keepdims=True))
        o = _dot(p.astype(BF16), v_ref[:, sl])
        outs.append(o / jnp.sum(p, axis=-1, keepdims=True))
    o = jnp.concatenate(outs, axis=-1).astype(BF16)
    return h + _dot(o, wo_ref[...])


def _route(xn, whi_ref, wlo_ref, bias, carry):
    x_hi = xn.astype(BF16)
    x_lo = (xn - x_hi.astype(F32)).astype(BF16)
    logits = _dot(x_hi, whi_ref[...]) + _dot(x_lo, whi_ref[...]) + _dot(x_hi, wlo_ref[...]) + bias
    tm = logits.shape[0]
    lane = lax.broadcasted_iota(jnp.int32, logits.shape, 1).astype(F32)
    neg = -jnp.inf

    def first_argmax(vals):
        m = jnp.max(vals, axis=-1, keepdims=True)
        return m, jnp.min(jnp.where(vals == m, lane, float(LANES)), axis=-1, keepdims=True)

    gl = jnp.where((lane >= _GROUP_LANE0) & (lane < _GROUP_LANE0 + MOE_GROUPS), logits, neg)
    gmax, glane = first_argmax(gl)
    pg = 1.0 / jnp.sum(jnp.exp(gl - gmax), axis=-1, keepdims=True)
    lo = (glane - _GROUP_LANE0) * EXPERTS_PER_GROUP
    el = jnp.where((lane >= lo) & (lane < lo + EXPERTS_PER_GROUP), logits, neg)
    m1, e1 = first_argmax(el)
    m2, e2 = first_argmax(jnp.where(lane == e1, neg, el))
    p2 = jnp.exp(m2 - m1)
    gate1 = pg / (1.0 + p2)
    gate2 = pg * p2 / (1.0 + p2)

    hot = jnp.where((lane == e1) | (lane == e2), 1.0, 0.0)
    row = lax.broadcasted_iota(jnp.int32, (tm, tm), 0)
    col = lax.broadcasted_iota(jnp.int32, (tm, tm), 1)
    before = jnp.where(row > col, 1.0, 0.0).astype(BF16)
    cum = _dot(before, hot.astype(BF16)) + carry
    rank1 = jnp.sum(jnp.where(lane == e1, cum, 0.0), axis=-1, keepdims=True)
    rank2 = jnp.sum(jnp.where(lane == e2, cum, 0.0), axis=-1, keepdims=True)

    info = jnp.zeros(logits.shape, F32)
    for i, val in enumerate((e1, e2, rank1, rank2, gate1, gate2)):
        info = jnp.where(lane == i, val, info)
    return info, carry + jnp.sum(hot, axis=0, keepdims=True)


def _post_mixer_kernel(h_ref, r_ref, m_ref, ys_ref, u_ref, d_ref, wg_ref, bg_ref, wo_ref,
                       gc_ref, wq_ref, k_ref, v_ref, wco_ref, gf_ref, whi_ref, wlo_ref, br_ref,
                       o_ref, info_ref, infot_ref, cnt_ref, carry_ref):
    @pl.when(pl.program_id(0) == 0)
    def _():
        carry_ref[...] = jnp.zeros_like(carry_ref)

    h = _mixer_out_proj(h_ref[...], r_ref[...], m_ref[...], ys_ref[...], u_ref[...], d_ref[...], wg_ref,
                        bg_ref[...], wo_ref)
    h = _cross_attention(h, gc_ref[...], wq_ref, k_ref, v_ref, wco_ref)
    _store_token_tiles(o_ref, h)
    info, carry = _route(_rms(h, gf_ref[...]), whi_ref, wlo_ref, br_ref[...], carry_ref[...])
    carry_ref[...] = carry
    cnt_ref[...] = carry
    info_ref[...] = info
    infot_ref[...] = info.T[0:_INFO_ROWS, :]


def _post_mixer(h, out_r, out_m, y_s, u, d_s5, w_glu, b_glu, w_out, g_cross, wq, kv, layer, w_co,
                g_ffn, w_r, b_r, seq, mem_len):
    t, d = h.shape
    nt = d // LANES
    tm = ROW_TILE
    tiles_per_seq = seq // tm
    w_hi = w_r.astype(BF16)
    w_lo = (w_r - w_hi.astype(F32)).astype(BF16)
    row = lambda w: pl.BlockSpec((tm, w), lambda i: (i, 0))
    full = lambda a, b: pl.BlockSpec((a, b), lambda i: (0, 0))
    mem_blk = lambda col: pl.BlockSpec((mem_len, d), lambda i: (i // tiles_per_seq, col))
    return pl.pallas_call(
        _post_mixer_kernel,
        grid=(t // tm,),
        in_specs=[row(d), row(RET_W), row(SSD_W), row(S5_W), row(S5_W),
                  full(1, S5_W), full(S5_W, S5_W), full(1, S5_W), full(d, d),
                  full(1, d), full(d, d), mem_blk(2 * layer), mem_blk(2 * layer + 1), full(d, d),
                  full(1, d), full(d, LANES), full(d, LANES), full(1, LANES)],
        out_specs=[pl.BlockSpec((tm * nt, LANES), lambda i: (i, 0)),
                   pl.BlockSpec((tm, LANES), lambda i: (i, 0)),
                   pl.BlockSpec((_INFO_ROWS, tm), lambda i: (0, i)),
                   pl.BlockSpec((1, LANES), lambda i: (0, 0))],
        out_shape=[jax.ShapeDtypeStruct((t * nt, LANES), F32), jax.ShapeDtypeStruct((t, LANES), F32),
                   jax.ShapeDtypeStruct((_INFO_ROWS, t), F32), jax.ShapeDtypeStruct((1, LANES), F32)],
        scratch_shapes=[pltpu.VMEM((1, LANES), F32)],
        compiler_params=_cparams("arbitrary"),
        name="post_mixer",
    )(h, out_r, out_m, y_s, u, d_s5, w_glu, b_glu, w_out, g_cross, wq, kv, kv, w_co, g_ffn, w_hi, w_lo, b_r)


def _token_copy(src_hbm, dst_vmem, src_row, dst_token, nt, sem):
    return pltpu.make_async_copy(src_hbm.at[pl.ds(pl.multiple_of(src_row, nt), nt), :],
                                 dst_vmem.at[pl.ds(dst_token * nt, nt), :], sem)


_ISSUE_UNROLL = 8


def _start_token_gather(src_hbm, idx_ref, n_tokens, dst, sem):
    nt = dst.shape[0] // n_tokens

    def body(j, carry):
        for p in range(2):
            r = 2 * j + p
            _token_copy(src_hbm, dst, idx_ref[0, 0, r], r, nt, sem).start(priority=p)
        return carry

    lax.fori_loop(0, n_tokens // 2, body, 0, unroll=_ISSUE_UNROLL)


def _wait_token_gather(src_hbm, dst, sem):
    pltpu.make_async_copy(src_hbm.at[pl.ds(0, dst.shape[0]), :], dst, sem).wait()


def _expert_kernel(be_ref, nu_ref, cur_ref, nxt_ref, h_hbm, g_ref, wg_ref, wu_ref, wd_ref, y_ref,
                   xbuf, wg_s, wu_s, wd_s, sem):
    i = pl.program_id(0)
    n_used = nu_ref[0]
    slot = i % 2
    used = i < n_used

    @pl.when((i == 0) & used)
    def _():
        _start_token_gather(h_hbm, cur_ref, MOE_BLOCK, xbuf.at[0], sem.at[0])

    @pl.when(i + 1 < n_used)
    def _():
        _start_token_gather(h_hbm, nxt_ref, MOE_BLOCK, xbuf.at[1 - slot], sem.at[1 - slot])

    @pl.when(used & ((i == 0) | (be_ref[i] != be_ref[jnp.maximum(i - 1, 0)])))
    def _():
        wg_s[...] = wg_ref[0, 0].astype(BF16)
        wu_s[...] = wu_ref[0, 0].astype(BF16)
        wd_s[...] = wd_ref[0, 0].astype(BF16)

    @pl.when(used)
    def _():
        _wait_token_gather(h_hbm, xbuf.at[slot], sem.at[slot])
        xn = _rms(_load_token_tiles(xbuf.at[slot], MOE_BLOCK), g_ref[...]).astype(BF16)
        hid = (_silu(_dot(xn, wg_s[...])) * _dot(xn, wu_s[...])).astype(BF16)
        _store_token_tiles(y_ref, _dot(hid, wd_s[...]))

    @pl.when(jnp.logical_not(used))
    def _():
        y_ref[...] = jnp.zeros_like(y_ref)


def _experts(h_tiles, g, block_e, n_used, src_row, w_gate, w_up, w_down, layer):
    d = w_gate.shape[-2]
    nt = d // LANES
    nb = block_e.shape[0]
    de = w_gate.shape[-1]
    idx_blk = lambda f: pl.BlockSpec((1, 1, MOE_BLOCK), lambda i, be, nu: (f(i), 0, 0),
                                     memory_space=pltpu.SMEM)
    grid_spec = pltpu.PrefetchScalarGridSpec(
        num_scalar_prefetch=2,
        grid=(nb,),
        in_specs=[idx_blk(lambda i: i), idx_blk(lambda i: jnp.minimum(i + 1, nb - 1)),
                  pl.BlockSpec(memory_space=pl.ANY),
                  pl.BlockSpec((1, d), lambda i, be, nu: (0, 0)),
                  pl.BlockSpec((1, 1, d, de), lambda i, be, nu: (layer, be[i], 0, 0)),
                  pl.BlockSpec((1, 1, d, de), lambda i, be, nu: (layer, be[i], 0, 0)),
                  pl.BlockSpec((1, 1, de, d), lambda i, be, nu: (layer, be[i], 0, 0))],
        out_specs=pl.BlockSpec((MOE_BLOCK * nt, LANES), lambda i, be, nu: (i, 0)),
        scratch_shapes=[pltpu.VMEM((2, MOE_BLOCK * nt, LANES), F32), pltpu.VMEM((d, de), BF16),
                        pltpu.VMEM((d, de), BF16), pltpu.VMEM((de, d), BF16),
                        pltpu.SemaphoreType.DMA((2,))],
    )
    src3 = src_row.reshape(nb, 1, MOE_BLOCK)
    return pl.pallas_call(
        _expert_kernel,
        grid_spec=grid_spec,
        out_shape=jax.ShapeDtypeStruct((nb * MOE_BLOCK * nt, LANES), F32),
        compiler_params=_cparams("arbitrary"),
        name="moe_experts",
    )(block_e, n_used, src3, src3, h_tiles, g, w_gate, w_up, w_down)


def _combine_kernel(cur_ref, nxt_ref, h_ref, info_ref, y_hbm, fg_ref, o_ref, ybuf, sem, *, final_norm):
    i = pl.program_id(0)
    slot = i % 2
    tm = info_ref.shape[0]
    nt = h_ref.shape[0] // tm

    def start_gather(dest_ref, s):
        def body(r, carry):
            for k in range(2):
                _token_copy(y_hbm, ybuf.at[s, k], dest_ref[0, 0, k * tm + r], r, nt,
                            sem.at[s, k]).start(priority=k)
            return carry

        lax.fori_loop(0, tm, body, 0, unroll=_ISSUE_UNROLL)

    @pl.when(i == 0)
    def _():
        start_gather(cur_ref, 0)

    @pl.when(i + 1 < pl.num_programs(0))
    def _():
        start_gather(nxt_ref, 1 - slot)

    for k in range(2):
        _wait_token_gather(y_hbm, ybuf.at[slot, k], sem.at[slot, k])
    info = info_ref[...]
    out = _load_token_tiles(h_ref, tm) + (info[:, 4:5] * _load_token_tiles(ybuf.at[slot, 0], tm)
                                          + info[:, 5:6] * _load_token_tiles(ybuf.at[slot, 1], tm))
    if final_norm:
        out = _rms(out, fg_ref[...])
    o_ref[...] = out


def _combine(h_tiles, info, dest, ybuf, final_g, final_norm):
    t = info.shape[0]
    d = final_g.shape[-1]
    nt = d // LANES
    tm = MOE_ROW_TILE
    n_tiles = t // tm
    dest3 = dest.reshape(2, n_tiles, tm).transpose(1, 0, 2).reshape(n_tiles, 1, 2 * tm)
    idx_blk = lambda f: pl.BlockSpec((1, 1, 2 * tm), lambda i: (f(i), 0, 0), memory_space=pltpu.SMEM)
    return pl.pallas_call(
        functools.partial(_combine_kernel, final_norm=final_norm),
        grid=(n_tiles,),
        in_specs=[idx_blk(lambda i: i), idx_blk(lambda i: jnp.minimum(i + 1, n_tiles - 1)),
                  pl.BlockSpec((tm * nt, LANES), lambda i: (i, 0)),
                  pl.BlockSpec((tm, LANES), lambda i: (i, 0)),
                  pl.BlockSpec(memory_space=pl.ANY),
                  pl.BlockSpec((1, d), lambda i: (0, 0))],
        out_specs=pl.BlockSpec((tm, d), lambda i: (i, 0)),
        out_shape=jax.ShapeDtypeStruct((t, d), F32),
        scratch_shapes=[pltpu.VMEM((2, 2, tm * nt, LANES), F32), pltpu.SemaphoreType.DMA((2, 2))],
        compiler_params=_cparams("arbitrary"),
        name="moe_combine",
    )(dest3, dest3, h_tiles, info, ybuf, final_g)


def _router_params(w_rg, b_rg, w_re, b_re):
    d = w_rg.shape[0]
    pad_cols = LANES - N_EXPERTS - MOE_GROUPS
    w_r = jnp.concatenate([w_re, w_rg, jnp.zeros((d, pad_cols), F32)], axis=1)
    b_r = jnp.concatenate([b_re, b_rg, jnp.zeros((pad_cols,), F32)])[None, :]
    return w_r, b_r


def _moe(h_tiles, routing, g, w_gate, w_up, w_down, layer, final_g, final_norm):
    info, infot, cnt = routing
    d = g.shape[-1]
    nt = d // LANES
    t = h_tiles.shape[0] // nt
    expert = infot[0:2].astype(jnp.int32)
    rank = infot[2:4].astype(jnp.int32)
    counts = cnt[0, :N_EXPERTS].astype(jnp.int32)
    padded = (counts + MOE_BLOCK - 1) // MOE_BLOCK * MOE_BLOCK
    pends = jnp.cumsum(padded)
    pstarts = pends - padded
    starts = jnp.cumsum(counts) - counts
    ids = jnp.arange(N_EXPERTS, dtype=jnp.int32)
    dest = jnp.sum(jnp.where(expert[..., None] == ids, pstarts, 0), axis=-1) + rank
    nb = (2 * t) // MOE_BLOCK + N_EXPERTS
    blk = jnp.arange(nb, dtype=jnp.int32)
    block_e = jnp.minimum(jnp.sum(pends[None, :] <= blk[:, None] * MOE_BLOCK, axis=1), N_EXPERTS - 1)
    block_e = block_e.astype(jnp.int32)
    n_used = (pends[-1:] // MOE_BLOCK).astype(jnp.int32)
    keys = expert * t + jnp.arange(t, dtype=jnp.int32)[None, :]
    sorted_tok = jnp.sort(keys.reshape(-1)) % t
    shift = jnp.sum(jnp.where(block_e[:, None] == ids, pstarts - starts, 0), axis=-1)
    pair = blk[:, None] * MOE_BLOCK + jnp.arange(MOE_BLOCK, dtype=jnp.int32)[None, :] - shift[:, None]
    src_tok = sorted_tok[jnp.clip(pair, 0, 2 * t - 1)]
    ybuf = _experts(h_tiles, g, block_e, n_used, src_tok * nt, w_gate, w_up, w_down, layer)
    return _combine(h_tiles, info, dest * nt, ybuf, final_g, final_norm)


def kernel(x, mem, norm_mix_g, w_in, ret_gn_g, ssd_conv_w, ssd_conv_b, ssd_dt_bias, ssd_A_log, ssd_D,
           ssd_norm_g, s5_A_re, s5_A_im, s5_B_re, s5_B_im, s5_C_re, s5_C_im, s5_log_step, s5_D, s5_w_glu,
           s5_b_glu, w_out, norm_cross_g, mem_norm_g, w_cq, w_ck, w_cv, w_co, norm_ffn_g, w_route_group,
           b_route_group, w_route_expert, b_route_expert, w_gate, w_up, w_down, norm_final_g):
    batch, seq, d = x.shape
    depth = w_in.shape[0]
    mem_len = mem.shape[1]
    t = batch * seq
    assert d == RET_W * 4 and t % ROW_TILE == 0 and seq % ROW_TILE == 0
    assert seq % RET_CHUNK == 0 and seq % SSD_CHUNK == 0 and seq % (S5_CHUNK * S5_BLOCK_STEPS) == 0
    assert t % MOE_BLOCK == 0 and t % MOE_ROW_TILE == 0
    row = lambda v: v[None, :]
    h = x.reshape(t, d)

    w_kv = jnp.concatenate([w for i in range(depth) for w in (w_ck[i], w_cv[i])], axis=1).astype(BF16)
    kv = _norm_matmul(mem.reshape(batch * mem_len, d), row(mem_norm_g), w_kv, BF16,
                      tm=mem_len, tn=d)
    ret_tables = _retention_tables(seq)
    c0 = _QKVG_W + SSD_W + SSD_XBC_W
    c1 = c0 + SSD_HEADS

    w_pack = jnp.concatenate(
        [w_in[:, :, :c0], w_in[:, :, c1:], w_in[:, :, c0:c1], jnp.zeros((depth, d, LANES - SSD_HEADS), F32)],
        axis=2).astype(BF16)
    w_glu_b, w_out_b, w_cq_b, w_co_b = (w.astype(BF16) for w in (s5_w_glu, w_out, w_cq, w_co))
    s5_ops_all = jax.vmap(_s5_operators)(s5_A_re, s5_A_im, s5_B_re, s5_B_im, s5_C_re, s5_C_im, s5_log_step)

    for i in range(depth):
        qkvg, z, xbc, u, dt = _in_proj(h, row(norm_mix_g[i]), w_pack[i])
        out_r = _retention(qkvg, ret_tables, row(ret_gn_g[i]), batch, seq)
        out_m = _ssd(z, xbc, dt, ssd_conv_w[i], ssd_conv_b[i], ssd_dt_bias[i], ssd_A_log[i], ssd_D[i],
                     ssd_norm_g[i], batch, seq)
        y_s = _s5(u, [op[i] for op in s5_ops_all], batch, seq)
        w_r, b_r = _router_params(w_route_group[i], b_route_group[i], w_route_expert[i], b_route_expert[i])
        h_tiles, *routing = _post_mixer(
            h, out_r, out_m, y_s, u, row(s5_D[i]), w_glu_b[i], row(s5_b_glu[i]), w_out_b[i],
            row(norm_cross_g[i]), w_cq_b[i], kv, i, w_co_b[i], row(norm_ffn_g[i]), w_r, b_r, seq, mem_len)
        h = _moe(h_tiles, routing, row(norm_ffn_g[i]), w_gate, w_up, w_down, i, row(norm_final_g),
                 final_norm=(i == depth - 1))
    return h.reshape(batch, seq, d)
```

```python
import functools
import math

import jax
import jax.numpy as jnp
from jax import lax
from jax.experimental import pallas as pl
from jax.experimental.pallas import tpu as pltpu

F32 = jnp.float32
BF16 = jnp.bfloat16
HIGHEST = lax.Precision.HIGHEST

EPS = 1e-6
RET_HEADS = 4
RET_HEAD_DIM = 64
RET_W = RET_HEADS * RET_HEAD_DIM
ROPE_BASE = 10000.0
SSD_HEAD_DIM = 64
SSD_HEADS = 8
SSD_GROUPS = 2
SSD_STATE = 128
SSD_CONV = 4
SSD_W = SSD_HEADS * SSD_HEAD_DIM
SSD_XBC_W = SSD_W + 2 * SSD_GROUPS * SSD_STATE
S5_GROUP = 16
S5_GROUPS = 16
S5_STATE = 64
S5_W = S5_GROUP * S5_GROUPS
CROSS_HEADS = 4
MOE_GROUPS = 4
EXPERTS_PER_GROUP = 8
N_EXPERTS = MOE_GROUPS * EXPERTS_PER_GROUP

LANES = 128
ROW_TILE = 512
RET_CHUNK = 256
SSD_CHUNK = 128
S5_CHUNK = 8
S5_LANE_GROUPS = LANES // S5_GROUP
S5_HALVES = S5_W // LANES
S5_BLOCK_STEPS = 64
MOE_BLOCK = 256
MOE_ROW_TILE = 256
VMEM_LIMIT = 48 * 1024 * 1024


def _cparams(*sem):
    return pltpu.CompilerParams(dimension_semantics=sem, vmem_limit_bytes=VMEM_LIMIT)


def _rms(x, g):
    return x * lax.rsqrt(jnp.mean(x * x, axis=-1, keepdims=True) + EPS) * g


def _silu(x):
    return x * jax.nn.sigmoid(x)


def _dot(a, b):
    return jnp.dot(a, b, preferred_element_type=F32)


def _dot_nt(a, b):
    return lax.dot_general(a, b, (((1,), (1,)), ((), ())), preferred_element_type=F32)


def _dot_tn(a, b):
    return lax.dot_general(a, b, (((0,), (0,)), ((), ())), preferred_element_type=F32)


def _load_token_tiles(ref, n_rows):
    nt = ref.shape[0] // n_rows
    return jnp.concatenate([ref[pl.ds(j, n_rows, stride=nt), :] for j in range(nt)], axis=-1)


def _store_token_tiles(ref, x):
    n_rows = x.shape[0]
    nt = ref.shape[0] // n_rows
    for j in range(nt):
        ref[pl.ds(j, n_rows, stride=nt), :] = x[:, j * LANES:(j + 1) * LANES]


def _norm_matmul_kernel(x_ref, g_ref, w_ref, o_ref):
    xn = _rms(x_ref[...], g_ref[...]).astype(BF16)
    o_ref[...] = _dot(xn, w_ref[...]).astype(o_ref.dtype)


def _norm_matmul(x, g, w, out_dtype, tm, tn):
    m, d = x.shape
    n = w.shape[1]
    return pl.pallas_call(
        _norm_matmul_kernel,
        grid=(m // tm, n // tn),
        in_specs=[pl.BlockSpec((tm, d), lambda i, j: (i, 0)),
                  pl.BlockSpec((1, d), lambda i, j: (0, 0)),
                  pl.BlockSpec((d, tn), lambda i, j: (0, j))],
        out_specs=pl.BlockSpec((tm, tn), lambda i, j: (i, j)),
        out_shape=jax.ShapeDtypeStruct((m, n), out_dtype),
        compiler_params=_cparams("parallel", "parallel"),
        name="norm_matmul",
    )(x, g, w)


_QKVG_W = 4 * RET_W
_IN_SPLITS = (_QKVG_W, SSD_W, SSD_XBC_W, S5_W, LANES)


def _in_proj_kernel(h_ref, g_ref, w_ref, qkvg_ref, z_ref, xbc_ref, u_ref, dt_ref):
    xn = _rms(h_ref[...], g_ref[...]).astype(BF16)
    lo = 0
    for ref, width in zip((qkvg_ref, z_ref, xbc_ref, u_ref, dt_ref), _IN_SPLITS):
        ref[...] = _dot(xn, w_ref[:, lo:lo + width]).astype(ref.dtype)
        lo += width


def _in_proj(h, g, w_pack):
    t, d = h.shape
    tm = ROW_TILE
    n = w_pack.shape[1]
    dts = (BF16, BF16, BF16, F32, F32)
    return pl.pallas_call(
        _in_proj_kernel,
        grid=(t // tm,),
        in_specs=[pl.BlockSpec((tm, d), lambda i: (i, 0)),
                  pl.BlockSpec((1, d), lambda i: (0, 0)),
                  pl.BlockSpec((d, n), lambda i: (0, 0))],
        out_specs=[pl.BlockSpec((tm, w), lambda i: (i, 0)) for w in _IN_SPLITS],
        out_shape=[jax.ShapeDtypeStruct((t, w), dt) for w, dt in zip(_IN_SPLITS, dts)],
        compiler_params=_cparams("parallel"),
        name="in_proj",
    )(h, g, w_pack)


def _retention_kernel(qkvg_ref, cos_ref, sin_ref, decay_ref, qdec_ref, kdec_ref, cdec_ref, gn_ref,
                      out_ref, s_ref):
    @pl.when(pl.program_id(1) == 0)
    def _():
        s_ref[...] = jnp.zeros_like(s_ref)

    x = qkvg_ref[...]
    w = RET_W
    q = x[:, 0:w].astype(F32)
    k = x[:, w:2 * w].astype(F32)
    v = x[:, 2 * w:3 * w]
    g = x[:, 3 * w:4 * w].astype(F32)
    half = RET_HEAD_DIM // 2
    lane = lax.broadcasted_iota(jnp.int32, q.shape, 1)
    first_half = (lane % RET_HEAD_DIM) < half

    def rot(t):
        swapped = jnp.where(first_half, pltpu.roll(t, w - half, 1), pltpu.roll(t, half, 1))
        return t * cos_ref[...] + swapped * sin_ref[...]

    qr = rot(q)
    kr = rot(k) * (RET_HEAD_DIM ** -0.5)
    qb = qr.astype(BF16)
    kb = kr.astype(BF16)
    qd = (qr * qdec_ref[...]).astype(BF16)
    kd = (kr * kdec_ref[...]).astype(BF16)
    outs = []
    for h in range(RET_HEADS):
        sl = slice(h * RET_HEAD_DIM, (h + 1) * RET_HEAD_DIM)
        s = _dot_nt(qb[:, sl], kb[:, sl]) * decay_ref[h]
        state = s_ref[h]
        y = _dot(s.astype(BF16), v[:, sl]) + _dot(qd[:, sl], state.astype(BF16))
        s_ref[h] = state * cdec_ref[h] + _dot_tn(kd[:, sl], v[:, sl])
        outs.append(y * lax.rsqrt(jnp.mean(y * y, axis=-1, keepdims=True) + EPS))
    yr = jnp.concatenate(outs, axis=-1)
    out_ref[...] = (_silu(g) * (yr * gn_ref[...])).astype(out_ref.dtype)


def _retention_tables(seq):
    c = RET_CHUNK
    dh = RET_HEAD_DIM
    inv = ROPE_BASE ** (-jnp.arange(0, dh, 2, dtype=F32) / dh)
    ang = jnp.arange(seq, dtype=F32)[:, None] * inv[None, :]
    cos, sin = jnp.cos(ang), jnp.sin(ang)
    cos4 = jnp.tile(jnp.concatenate([cos, cos], axis=-1), (1, RET_HEADS))
    sin4 = jnp.tile(jnp.concatenate([-sin, sin], axis=-1), (1, RET_HEADS))
    lg = jnp.log1p(-(2.0 ** (-5.0 - jnp.arange(RET_HEADS, dtype=F32))))
    i = jnp.arange(c, dtype=F32)
    rel = i[:, None] - i[None, :]
    decay = jnp.where(rel[None] >= 0, jnp.exp(lg[:, None, None] * jnp.maximum(rel, 0.0)[None]), 0.0)
    per_head = lambda t: jnp.repeat(t.T, dh, axis=1)
    qdec = per_head(jnp.exp(lg[:, None] * (i + 1.0)[None]))
    kdec = per_head(jnp.exp(lg[:, None] * (c - 1.0 - i)[None]))
    cdec = jnp.broadcast_to(jnp.exp(lg * c)[:, None, None], (RET_HEADS, dh, dh))
    return cos4, sin4, decay, qdec, kdec, cdec


def _retention(qkvg, tables, gn, batch, seq):
    c = RET_CHUNK
    nc = seq // c
    cos4, sin4, decay, qdec, kdec, cdec = tables
    w = RET_W
    full = lambda shape: pl.BlockSpec(shape, lambda b, j: (0,) * len(shape))
    return pl.pallas_call(
        _retention_kernel,
        grid=(batch, nc),
        in_specs=[pl.BlockSpec((c, _QKVG_W), lambda b, j: (b * nc + j, 0)),
                  pl.BlockSpec((c, w), lambda b, j: (j, 0)),
                  pl.BlockSpec((c, w), lambda b, j: (j, 0)),
                  full((RET_HEADS, c, c)), full((c, w)), full((c, w)),
                  full((RET_HEADS, RET_HEAD_DIM, RET_HEAD_DIM)), full((1, w))],
        out_specs=pl.BlockSpec((c, w), lambda b, j: (b * nc + j, 0)),
        out_shape=jax.ShapeDtypeStruct((batch * seq, w), BF16),
        scratch_shapes=[pltpu.VMEM((RET_HEADS, RET_HEAD_DIM, RET_HEAD_DIM), F32)],
        compiler_params=_cparams("parallel", "arbitrary"),
        name="retention",
    )(qkvg, cos4, sin4, decay, qdec, kdec, cdec, gn)


def _split3(v):
    hi = v.astype(BF16)
    rest = v - hi.astype(F32)
    mid = rest.astype(BF16)
    return hi, mid, (rest - mid.astype(F32)).astype(BF16)


def _ssd_kernel(z_ref, xbc_ref, dt_ref, shift_ref, expand_ref, cw_ref, cb_ref, dtb_ref, a_ref, d_ref,
                ng_ref, out_ref, xcat_ref, s_ref):
    c = SSD_CHUNK
    p = SSD_HEAD_DIM
    n = SSD_STATE

    @pl.when(pl.program_id(1) == 0)
    def _():
        xcat_ref[0:c, :] = jnp.zeros((c, SSD_XBC_W), BF16)
        s_ref[...] = jnp.zeros_like(s_ref)

    x_cur = xbc_ref[...]
    xcat_ref[c:2 * c, :] = x_cur
    x_cat = xcat_ref[...]
    conv = cb_ref[...] + x_cur.astype(F32) * cw_ref[SSD_CONV - 1:SSD_CONV, :]
    shifted = _dot(shift_ref[...], x_cat)
    for j in range(SSD_CONV - 1):
        conv = conv + shifted[j * c:(j + 1) * c, :] * cw_ref[j:j + 1, :]
    xcat_ref[0:c, :] = x_cur
    act = _silu(conv)
    xs = act[:, :SSD_W]
    bm = act[:, SSD_W:SSD_W + SSD_GROUPS * n]
    cm = act[:, SSD_W + SSD_GROUPS * n:]

    dt_in = dt_ref[...] + dtb_ref[...]
    dt = jnp.maximum(dt_in, 0.0) + jnp.log1p(jnp.exp(-jnp.abs(dt_in)))
    row = lax.broadcasted_iota(jnp.int32, (c, c), 0)
    col = lax.broadcasted_iota(jnp.int32, (c, c), 1)
    causal = row >= col
    a_cum = jnp.dot(causal.astype(F32), dt * a_ref[...], precision=HIGHEST,
                    preferred_element_type=F32)
    a_cum_t = a_cum.T

    terms = _split3(dt) + _split3(a_cum)
    wide = _dot(jnp.concatenate(terms, axis=0), expand_ref[...])
    dt_w = wide[0:c] + wide[c:2 * c] + wide[2 * c:3 * c]
    a_w = wide[3 * c:4 * c] + wide[4 * c:5 * c] + wide[5 * c:6 * c]
    a_last_w = a_w[c - 1:c, :]
    chunk_dec_w = jnp.exp(a_last_w)
    xdt = xs * dt_w
    xdt_b = xdt.astype(BF16)
    xdec_b = (xdt * jnp.exp(a_last_w - a_w)).astype(BF16)

    heads_per_group = SSD_HEADS // SSD_GROUPS
    gw = heads_per_group * p
    y_diag, y_off = [], []
    for g in range(SSD_GROUPS):
        bg = bm[:, g * n:(g + 1) * n].astype(BF16)
        cg = cm[:, g * n:(g + 1) * n].astype(BF16)
        cb = _dot_nt(cg, bg)
        gl = slice(g * gw, (g + 1) * gw)
        states = s_ref[g]
        y_off.append(_dot(cg, states.astype(BF16)))
        s_ref[g] = states * chunk_dec_w[:, gl] + _dot_tn(bg, xdec_b[:, gl])
        for r in range(heads_per_group):
            h = g * heads_per_group + r
            lmat = jnp.exp(jnp.where(causal, a_cum[:, h:h + 1] - a_cum_t[h:h + 1, :], -jnp.inf))
            y_diag.append(_dot((cb * lmat).astype(BF16), xdt_b[:, h * p:(h + 1) * p]))
    y = (jnp.concatenate(y_diag, axis=-1) + jnp.concatenate(y_off, axis=-1) * jnp.exp(a_w)
         + xs * d_ref[...])
    out_ref[...] = _rms(y * _silu(z_ref[...].astype(F32)), ng_ref[...]).astype(out_ref.dtype)


def _ssd(z, xbc, dt, conv_w, conv_b, dt_bias, a_log, d_skip, norm_g, batch, seq):
    c = SSD_CHUNK
    nc = seq // c
    pad = lambda v: jnp.pad(v, (0, LANES - v.shape[0]))[None, :]
    a_neg = pad(-jnp.exp(a_log))
    d_wide = jnp.repeat(d_skip, SSD_HEAD_DIM)[None, :]
    t_idx = jnp.arange(c)[None, :, None]
    lag = (SSD_CONV - 1 - jnp.arange(SSD_CONV - 1))[:, None, None]
    shift = (jnp.arange(2 * c)[None, None, :] == c + t_idx - lag).astype(BF16).reshape(-1, 2 * c)
    expand = (jnp.arange(LANES)[:, None] == jnp.arange(SSD_W)[None, :] // SSD_HEAD_DIM).astype(BF16)
    full = lambda shape: pl.BlockSpec(shape, lambda b, j: (0,) * len(shape))
    blk = lambda w: pl.BlockSpec((c, w), lambda b, j: (b * nc + j, 0))
    return pl.pallas_call(
        _ssd_kernel,
        grid=(batch, nc),
        in_specs=[blk(SSD_W), blk(SSD_XBC_W), blk(LANES), full(((SSD_CONV - 1) * c, 2 * c)),
                  full((LANES, SSD_W)), full((SSD_CONV, SSD_XBC_W)), full((1, SSD_XBC_W)),
                  full((1, LANES)), full((1, LANES)), full((1, SSD_W)), full((1, SSD_W))],
        out_specs=blk(SSD_W),
        out_shape=jax.ShapeDtypeStruct((batch * seq, SSD_W), BF16),
        scratch_shapes=[pltpu.VMEM((2 * c, SSD_XBC_W), BF16),
                        pltpu.VMEM((SSD_GROUPS, SSD_STATE, SSD_W // SSD_GROUPS), F32)],
        compiler_params=_cparams("parallel", "arbitrary"),
        name="ssd",
    )(z, xbc, dt, shift, expand, conv_w, conv_b[None, :], pad(dt_bias), a_neg, d_wide, norm_g[None, :])


def _s5_kernel(u_ref, t1_ref, pre_ref, pim_ref, qre_ref, qim_ref, are_ref, aim_ref, y_ref,
               ere_ref, eim_ref, xre_ref, xim_ref, sre_ref, sim_ref):
    batch, tb, _ = u_ref.shape
    cs = S5_CHUNK
    ns = tb // cs

    @pl.when(pl.program_id(1) == 0)
    def _():
        sre_ref[...] = jnp.zeros_like(sre_ref)
        sim_ref[...] = jnp.zeros_like(sim_ref)

    u = jnp.concatenate(
        [jnp.concatenate([u_ref[b, pl.ds(s, ns, stride=cs), :] for s in range(cs)], axis=-1)
         for b in range(batch)], axis=0).astype(BF16)
    n_tiles = ere_ref.shape[0]
    lanes_of = lambda j: slice(j * LANES, (j + 1) * LANES)
    e_re = _dot(u, pre_ref[0])
    e_im = _dot(u, pim_ref[0])
    for j in range(n_tiles):
        ere_ref[j] = e_re[:, lanes_of(j)]
        eim_ref[j] = e_im[:, lanes_of(j)]
    shape = (batch, LANES)
    ar = [jnp.broadcast_to(are_ref[0, :, lanes_of(j)], shape) for j in range(n_tiles)]
    ai = [jnp.broadcast_to(aim_ref[0, :, lanes_of(j)], shape) for j in range(n_tiles)]

    def step(n, carry):
        rows = pl.ds(n, batch, stride=ns)
        out = []
        for j in range(n_tiles):
            xr, xi = carry[j]
            xre_ref[j, rows, :] = xr
            xim_ref[j, rows, :] = xi
            out.append((ar[j] * xr - ai[j] * xi + ere_ref[j, rows, :],
                        ar[j] * xi + ai[j] * xr + eim_ref[j, rows, :]))
        return tuple(out)

    init = tuple((sre_ref[j], sim_ref[j]) for j in range(n_tiles))
    final = lax.fori_loop(0, ns, step, init, unroll=8)
    for j in range(n_tiles):
        sre_ref[j], sim_ref[j] = final[j]
    x_re = jnp.concatenate([xre_ref[j] for j in range(n_tiles)], axis=-1).astype(BF16)
    x_im = jnp.concatenate([xim_ref[j] for j in range(n_tiles)], axis=-1).astype(BF16)
    y = _dot(u, t1_ref[0]) + _dot(x_re, qre_ref[0]) + _dot(x_im, qim_ref[0])
    for b in range(batch):
        for s in range(cs):
            y_ref[b, pl.ds(s, ns, stride=cs), :] = y[b * ns:(b + 1) * ns, s * LANES:(s + 1) * LANES]


def _s5_operators(a_re, a_im, b_re, b_im, c_re, c_im, log_step):
    cs = S5_CHUNK
    ein = functools.partial(jnp.einsum, precision=HIGHEST)
    delta = jnp.exp(log_step)[:, None]
    ar, ai = a_re, a_im
    mag = jnp.exp(ar * delta)
    ang = ai * delta
    lr, li = mag * jnp.cos(ang), mag * jnp.sin(ang)
    den = ar * ar + ai * ai
    nr, ni = lr - 1.0, li
    cr = (nr * ar + ni * ai) / den
    ci = (ni * ar - nr * ai) / den
    bbr = cr[..., None] * b_re - ci[..., None] * b_im
    bbi = cr[..., None] * b_im + ci[..., None] * b_re
    k = jnp.arange(cs + 1, dtype=F32)
    pmag = jnp.exp((ar * delta)[..., None] * k)
    pang = ang[..., None] * k
    pr, pi = pmag * jnp.cos(pang), pmag * jnp.sin(pang)
    clr = c_re[..., None] * pr[:, None] - c_im[..., None] * pi[:, None]
    cli = c_re[..., None] * pi[:, None] + c_im[..., None] * pr[:, None]
    kern = ein('gcpk,gpd->gkcd', clr, bbr) - ein('gcpk,gpd->gkcd', cli, bbi)
    kern = jnp.concatenate([kern[:, :cs], jnp.zeros_like(kern[:, :1])], axis=1)
    s = jnp.arange(cs)
    lag = jnp.where(s[None, :] >= s[:, None], s[None, :] - s[:, None], cs)
    nh, ng = S5_HALVES, S5_LANE_GROUPS
    eye = jnp.eye(ng, dtype=F32)
    halves = lambda t: t.reshape((nh, ng) + t.shape[1:])
    bd = halves(kern).transpose(0, 2, 1, 4, 3)
    bd = (bd[:, :, :, :, None, :] * eye[None, None, :, None, :, None]).reshape(nh, cs + 1, LANES, LANES)
    t1 = bd.astype(BF16)[:, lag].transpose(0, 1, 3, 2, 4).reshape(nh, cs * LANES, cs * LANES)
    rev = cs - 1 - s
    prr, pri = pr[..., rev], pi[..., rev]
    p_re = prr[..., None] * bbr[:, :, None] - pri[..., None] * bbi[:, :, None]
    p_im = prr[..., None] * bbi[:, :, None] + pri[..., None] * bbr[:, :, None]

    def flat_p(t):
        t = halves(t.transpose(0, 2, 3, 1)).transpose(0, 2, 1, 3, 4)
        t = t[:, :, :, :, None, :] * eye[None, None, :, None, :, None]
        return t.reshape(nh, cs * LANES, ng * S5_STATE).astype(BF16)

    def flat_q(t):
        t = halves(t.transpose(0, 2, 3, 1))
        t = t[:, :, :, :, None, :] * eye[None, :, None, None, :, None]
        return t.reshape(nh, ng * S5_STATE, cs * LANES).astype(BF16)

    a_chunk_re = pr[..., cs].reshape(nh, 1, ng * S5_STATE)
    a_chunk_im = pi[..., cs].reshape(nh, 1, ng * S5_STATE)
    return (t1, flat_p(p_re), flat_p(p_im), flat_q(clr[..., 1:]), flat_q(-cli[..., 1:]),
            a_chunk_re, a_chunk_im)


def _s5(u, ops, batch, seq):
    tb = S5_CHUNK * S5_BLOCK_STEPS
    rows = batch * S5_BLOCK_STEPS
    flat = S5_CHUNK * LANES
    nstate = S5_LANE_GROUPS * S5_STATE
    per_h = lambda a, b: pl.BlockSpec((1, a, b), lambda h, j: (h, 0, 0))
    seq_blk = pl.BlockSpec((batch, tb, LANES), lambda h, j: (0, j, h))
    y = pl.pallas_call(
        _s5_kernel,
        grid=(S5_HALVES, seq // tb),
        in_specs=[seq_blk, per_h(flat, flat), per_h(flat, nstate), per_h(flat, nstate),
                  per_h(nstate, flat), per_h(nstate, flat), per_h(1, nstate), per_h(1, nstate)],
        out_specs=seq_blk,
        out_shape=jax.ShapeDtypeStruct((batch, seq, S5_W), F32),
        scratch_shapes=([pltpu.VMEM((nstate // LANES, rows, LANES), F32)] * 4
                        + [pltpu.VMEM((nstate // LANES, batch, LANES), F32)] * 2),
        compiler_params=_cparams("parallel", "arbitrary"),
        name="s5",
    )(u.reshape(batch, seq, S5_W), *ops)
    return y.reshape(batch * seq, S5_W)


_GROUP_LANE0 = N_EXPERTS
_INFO_ROWS = 8


def _mixer_out_proj(h, out_r, out_m, y_s, u, d_s5, w_glu, b_glu, w_out):
    y = y_s + d_s5 * u
    g = jax.nn.gelu(y)
    s = g * jax.nn.sigmoid(_dot(g.astype(BF16), w_glu[...]) + b_glu)
    acc = _dot(out_r, w_out[0:RET_W, :])
    acc = acc + _dot(out_m, w_out[RET_W:RET_W + SSD_W, :])
    acc = acc + _dot(s.astype(BF16), w_out[RET_W + SSD_W:, :])
    return h + acc


def _cross_attention(h, g, wq_ref, k_ref, v_ref, wo_ref):
    d = h.shape[-1]
    dh = d // CROSS_HEADS
    q = _dot(_rms(h, g).astype(BF16), wq_ref[...]).astype(BF16)
    outs = []
    for i in range(CROSS_HEADS):
        sl = slice(i * dh, (i + 1) * dh)
        s = _dot_nt(q[:, sl], k_ref[:, sl]) * (dh ** -0.5)
        p = jnp.exp(s - jnp.max(s, axis=-1, keepdims=True))
        o = _dot(p.astype(BF16), v_ref[:, sl])
        outs.append(o / jnp.sum(p, axis=-1, keepdims=True))
    o = jnp.concatenate(outs, axis=-1).astype(BF16)
    return h + _dot(o, wo_ref[...])


def _route(xn, whi_ref, wlo_ref, bias, carry):
    x_hi = xn.astype(BF16)
    x_lo = (xn - x_hi.astype(F32)).astype(BF16)
    logits = _dot(x_hi, whi_ref[...]) + _dot(x_lo, whi_ref[...]) + _dot(x_hi, wlo_ref[...]) + bias
    tm = logits.shape[0]
    lane = lax.broadcasted_iota(jnp.int32, logits.shape, 1).astype(F32)
    neg = -jnp.inf

    def first_argmax(vals):
        m = jnp.max(vals, axis=-1, keepdims=True)
        return m, jnp.min(jnp.where(vals == m, lane, float(LANES)), axis=-1, keepdims=True)

    gl = jnp.where((lane >= _GROUP_LANE0) & (lane < _GROUP_LANE0 + MOE_GROUPS), logits, neg)
    gmax, glane = first_argmax(gl)
    pg = 1.0 / jnp.sum(jnp.exp(gl - gmax), axis=-1, keepdims=True)
    lo = (glane - _GROUP_LANE0) * EXPERTS_PER_GROUP
    el = jnp.where((lane >= lo) & (lane < lo + EXPERTS_PER_GROUP), logits, neg)
    m1, e1 = first_argmax(el)
    m2, e2 = first_argmax(jnp.where(lane == e1, neg, el))
    p2 = jnp.exp(m2 - m1)
    gate1 = pg / (1.0 + p2)
    gate2 = pg * p2 / (1.0 + p2)

    hot = jnp.where((lane == e1) | (lane == e2), 1.0, 0.0)
    row = lax.broadcasted_iota(jnp.int32, (tm, tm), 0)
    col = lax.broadcasted_iota(jnp.int32, (tm, tm), 1)
    before = jnp.where(row > col, 1.0, 0.0).astype(BF16)
    cum = _dot(before, hot.astype(BF16)) + carry
    rank1 = jnp.sum(jnp.where(lane == e1, cum, 0.0), axis=-1, keepdims=True)
    rank2 = jnp.sum(jnp.where(lane == e2, cum, 0.0), axis=-1, keepdims=True)

    info = jnp.zeros(logits.shape, F32)
    for i, val in enumerate((e1, e2, rank1, rank2, gate1, gate2)):
        info = jnp.where(lane == i, val, info)
    return info, carry + jnp.sum(hot, axis=0, keepdims=True)


def _post_mixer_kernel(h_ref, r_ref, m_ref, ys_ref, u_ref, d_ref, wg_ref, bg_ref, wo_ref,
                       gc_ref, wq_ref, k_ref, v_ref, wco_ref, gf_ref, whi_ref, wlo_ref, br_ref,
                       o_ref, info_ref, infot_ref, cnt_ref, carry_ref):
    @pl.when(pl.program_id(0) == 0)
    def _():
        carry_ref[...] = jnp.zeros_like(carry_ref)

    h = _mixer_out_proj(h_ref[...], r_ref[...], m_ref[...], ys_ref[...], u_ref[...], d_ref[...], wg_ref,
                        bg_ref[...], wo_ref)
    h = _cross_attention(h, gc_ref[...], wq_ref, k_ref, v_ref, wco_ref)
    _store_token_tiles(o_ref, h)
    info, carry = _route(_rms(h, gf_ref[...]), whi_ref, wlo_ref, br_ref[...], carry_ref[...])
    carry_ref[...] = carry
    cnt_ref[...] = carry
    info_ref[...] = info
    infot_ref[...] = info.T[0:_INFO_ROWS, :]


def _post_mixer(h, out_r, out_m, y_s, u, d_s5, w_glu, b_glu, w_out, g_cross, wq, kv, layer, w_co,
                g_ffn, w_r, b_r, seq, mem_len):
    t, d = h.shape
    nt = d // LANES
    tm = ROW_TILE
    tiles_per_seq = seq // tm
    w_hi = w_r.astype(BF16)
    w_lo = (w_r - w_hi.astype(F32)).astype(BF16)
    row = lambda w: pl.BlockSpec((tm, w), lambda i: (i, 0))
    full = lambda a, b: pl.BlockSpec((a, b), lambda i: (0, 0))
    mem_blk = lambda col: pl.BlockSpec((mem_len, d), lambda i: (i // tiles_per_seq, col))
    return pl.pallas_call(
        _post_mixer_kernel,
        grid=(t // tm,),
        in_specs=[row(d), row(RET_W), row(SSD_W), row(S5_W), row(S5_W),
                  full(1, S5_W), full(S5_W, S5_W), full(1, S5_W), full(d, d),
                  full(1, d), full(d, d), mem_blk(2 * layer), mem_blk(2 * layer + 1), full(d, d),
                  full(1, d), full(d, LANES), full(d, LANES), full(1, LANES)],
        out_specs=[pl.BlockSpec((tm * nt, LANES), lambda i: (i, 0)),
                   pl.BlockSpec((tm, LANES), lambda i: (i, 0)),
                   pl.BlockSpec((_INFO_ROWS, tm), lambda i: (0, i)),
                   pl.BlockSpec((1, LANES), lambda i: (0, 0))],
        out_shape=[jax.ShapeDtypeStruct((t * nt, LANES), F32), jax.ShapeDtypeStruct((t, LANES), F32),
                   jax.ShapeDtypeStruct((_INFO_ROWS, t), F32), jax.ShapeDtypeStruct((1, LANES), F32)],
        scratch_shapes=[pltpu.VMEM((1, LANES), F32)],
        compiler_params=_cparams("arbitrary"),
        name="post_mixer",
    )(h, out_r, out_m, y_s, u, d_s5, w_glu, b_glu, w_out, g_cross, wq, kv, kv, w_co, g_ffn, w_hi, w_lo, b_r)


def _token_copy(src_hbm, dst_vmem, src_row, dst_token, nt, sem):
    return pltpu.make_async_copy(src_hbm.at[pl.ds(pl.multiple_of(src_row, nt), nt), :],
                                 dst_vmem.at[pl.ds(dst_token * nt, nt), :], sem)


_ISSUE_UNROLL = 8


def _start_token_gather(src_hbm, idx_ref, n_tokens, dst, sem):
    nt = dst.shape[0] // n_tokens

    def body(j, carry):
        for p in range(2):
            r = 2 * j + p
            _token_copy(src_hbm, dst, idx_ref[0, 0, r], r, nt, sem).start(priority=p)
        return carry

    lax.fori_loop(0, n_tokens // 2, body, 0, unroll=_ISSUE_UNROLL)


def _wait_token_gather(src_hbm, dst, sem):
    pltpu.make_async_copy(src_hbm.at[pl.ds(0, dst.shape[0]), :], dst, sem).wait()


def _expert_kernel(be_ref, nu_ref, cur_ref, nxt_ref, h_hbm, g_ref, wg_ref, wu_ref, wd_ref, y_ref,
                   xbuf, wg_s, wu_s, wd_s, sem):
    i = pl.program_id(0)
    n_used = nu_ref[0]
    slot = i % 2
    used = i < n_used

    @pl.when((i == 0) & used)
    def _():
        _start_token_gather(h_hbm, cur_ref, MOE_BLOCK, xbuf.at[0], sem.at[0])

    @pl.when(i + 1 < n_used)
    def _():
        _start_token_gather(h_hbm, nxt_ref, MOE_BLOCK, xbuf.at[1 - slot], sem.at[1 - slot])

    @pl.when(used & ((i == 0) | (be_ref[i] != be_ref[jnp.maximum(i - 1, 0)])))
    def _():
        wg_s[...] = wg_ref[0, 0].astype(BF16)
        wu_s[...] = wu_ref[0, 0].astype(BF16)
        wd_s[...] = wd_ref[0, 0].astype(BF16)

    @pl.when(used)
    def _():
        _wait_token_gather(h_hbm, xbuf.at[slot], sem.at[slot])
        xn = _rms(_load_token_tiles(xbuf.at[slot], MOE_BLOCK), g_ref[...]).astype(BF16)
        hid = (_silu(_dot(xn, wg_s[...])) * _dot(xn, wu_s[...])).astype(BF16)
        _store_token_tiles(y_ref, _dot(hid, wd_s[...]))

    @pl.when(jnp.logical_not(used))
    def _():
        y_ref[...] = jnp.zeros_like(y_ref)


def _experts(h_tiles, g, block_e, n_used, src_row, w_gate, w_up, w_down, layer):
    d = w_gate.shape[-2]
    nt = d // LANES
    nb = block_e.shape[0]
    de = w_gate.shape[-1]
    idx_blk = lambda f: pl.BlockSpec((1, 1, MOE_BLOCK), lambda i, be, nu: (f(i), 0, 0),
                                     memory_space=pltpu.SMEM)
    grid_spec = pltpu.PrefetchScalarGridSpec(
        num_scalar_prefetch=2,
        grid=(nb,),
        in_specs=[idx_blk(lambda i: i), idx_blk(lambda i: jnp.minimum(i + 1, nb - 1)),
                  pl.BlockSpec(memory_space=pl.ANY),
                  pl.BlockSpec((1, d), lambda i, be, nu: (0, 0)),
                  pl.BlockSpec((1, 1, d, de), lambda i, be, nu: (layer, be[i], 0, 0)),
                  pl.BlockSpec((1, 1, d, de), lambda i, be, nu: (layer, be[i], 0, 0)),
                  pl.BlockSpec((1, 1, de, d), lambda i, be, nu: (layer, be[i], 0, 0))],
        out_specs=pl.BlockSpec((MOE_BLOCK * nt, LANES), lambda i, be, nu: (i, 0)),
        scratch_shapes=[pltpu.VMEM((2, MOE_BLOCK * nt, LANES), F32), pltpu.VMEM((d, de), BF16),
                        pltpu.VMEM((d, de), BF16), pltpu.VMEM((de, d), BF16),
                        pltpu.SemaphoreType.DMA((2,))],
    )
    src3 = src_row.reshape(nb, 1, MOE_BLOCK)
    return pl.pallas_call(
        _expert_kernel,
        grid_spec=grid_spec,
        out_shape=jax.ShapeDtypeStruct((nb * MOE_BLOCK * nt, LANES), F32),
        compiler_params=_cparams("arbitrary"),
        name="moe_experts",
    )(block_e, n_used, src3, src3, h_tiles, g, w_gate, w_up, w_down)


def _combine_kernel(cur_ref, nxt_ref, h_ref, info_ref, y_hbm, fg_ref, o_ref, ybuf, sem, *, final_norm):
    i = pl.program_id(0)
    slot = i % 2
    tm = info_ref.shape[0]
    nt = h_ref.shape[0] // tm

    def start_gather(dest_ref, s):
        def body(r, carry):
            for k in range(2):
                _token_copy(y_hbm, ybuf.at[s, k], dest_ref[0, 0, k * tm + r], r, nt,
                            sem.at[s, k]).start(priority=k)
            return carry

        lax.fori_loop(0, tm, body, 0, unroll=_ISSUE_UNROLL)

    @pl.when(i == 0)
    def _():
        start_gather(cur_ref, 0)

    @pl.when(i + 1 < pl.num_programs(0))
    def _():
        start_gather(nxt_ref, 1 - slot)

    for k in range(2):
        _wait_token_gather(y_hbm, ybuf.at[slot, k], sem.at[slot, k])
    info = info_ref[...]
    out = _load_token_tiles(h_ref, tm) + (info[:, 4:5] * _load_token_tiles(ybuf.at[slot, 0], tm)
                                          + info[:, 5:6] * _load_token_tiles(ybuf.at[slot, 1], tm))
    if final_norm:
        out = _rms(out, fg_ref[...])
    o_ref[...] = out


def _combine(h_tiles, info, dest, ybuf, final_g, final_norm):
    t = info.shape[0]
    d = final_g.shape[-1]
    nt = d // LANES
    tm = MOE_ROW_TILE
    n_tiles = t // tm
    dest3 = dest.reshape(2, n_tiles, tm).transpose(1, 0, 2).reshape(n_tiles, 1, 2 * tm)
    idx_blk = lambda f: pl.BlockSpec((1, 1, 2 * tm), lambda i: (f(i), 0, 0), memory_space=pltpu.SMEM)
    return pl.pallas_call(
        functools.partial(_combine_kernel, final_norm=final_norm),
        grid=(n_tiles,),
        in_specs=[idx_blk(lambda i: i), idx_blk(lambda i: jnp.minimum(i + 1, n_tiles - 1)),
                  pl.BlockSpec((tm * nt, LANES), lambda i: (i, 0)),
                  pl.BlockSpec((tm, LANES), lambda i: (i, 0)),
                  pl.BlockSpec(memory_space=pl.ANY),
                  pl.BlockSpec((1, d), lambda i: (0, 0))],
        out_specs=pl.BlockSpec((tm, d), lambda i: (i, 0)),
        out_shape=jax.ShapeDtypeStruct((t, d), F32),
        scratch_shapes=[pltpu.VMEM((2, 2, tm * nt, LANES), F32), pltpu.SemaphoreType.DMA((2, 2))],
        compiler_params=_cparams("arbitrary"),
        name="moe_combine",
    )(dest3, dest3, h_tiles, info, ybuf, final_g)


def _router_params(w_rg, b_rg, w_re, b_re):
    d = w_rg.shape[0]
    pad_cols = LANES - N_EXPERTS - MOE_GROUPS
    w_r = jnp.concatenate([w_re, w_rg, jnp.zeros((d, pad_cols), F32)], axis=1)
    b_r = jnp.concatenate([b_re, b_rg, jnp.zeros((pad_cols,), F32)])[None, :]
    return w_r, b_r


def _moe(h_tiles, routing, g, w_gate, w_up, w_down, layer, final_g, final_norm):
    info, infot, cnt = routing
    d = g.shape[-1]
    nt = d // LANES
    t = h_tiles.shape[0] // nt
    expert = infot[0:2].astype(jnp.int32)
    rank = infot[2:4].astype(jnp.int32)
    counts = cnt[0, :N_EXPERTS].astype(jnp.int32)
    padded = (counts + MOE_BLOCK - 1) // MOE_BLOCK * MOE_BLOCK
    pends = jnp.cumsum(padded)
    pstarts = pends - padded
    starts = jnp.cumsum(counts) - counts
    ids = jnp.arange(N_EXPERTS, dtype=jnp.int32)
    dest = jnp.sum(jnp.where(expert[..., None] == ids, pstarts, 0), axis=-1) + rank
    nb = (2 * t) // MOE_BLOCK + N_EXPERTS
    blk = jnp.arange(nb, dtype=jnp.int32)
    block_e = jnp.minimum(jnp.sum(pends[None, :] <= blk[:, None] * MOE_BLOCK, axis=1), N_EXPERTS - 1)
    block_e = block_e.astype(jnp.int32)
    n_used = (pends[-1:] // MOE_BLOCK).astype(jnp.int32)
    keys = expert * t + jnp.arange(t, dtype=jnp.int32)[None, :]
    sorted_tok = jnp.sort(keys.reshape(-1)) % t
    shift = jnp.sum(jnp.where(block_e[:, None] == ids, pstarts - starts, 0), axis=-1)
    pair = blk[:, None] * MOE_BLOCK + jnp.arange(MOE_BLOCK, dtype=jnp.int32)[None, :] - shift[:, None]
    src_tok = sorted_tok[jnp.clip(pair, 0, 2 * t - 1)]
    ybuf = _experts(h_tiles, g, block_e, n_used, src_tok * nt, w_gate, w_up, w_down, layer)
    return _combine(h_tiles, info, dest * nt, ybuf, final_g, final_norm)


def kernel(x, mem, norm_mix_g, w_in, ret_gn_g, ssd_conv_w, ssd_conv_b, ssd_dt_bias, ssd_A_log, ssd_D,
           ssd_norm_g, s5_A_re, s5_A_im, s5_B_re, s5_B_im, s5_C_re, s5_C_im, s5_log_step, s5_D, s5_w_glu,
           s5_b_glu, w_out, norm_cross_g, mem_norm_g, w_cq, w_ck, w_cv, w_co, norm_ffn_g, w_route_group,
           b_route_group, w_route_expert, b_route_expert, w_gate, w_up, w_down, norm_final_g):
    batch, seq, d = x.shape
    depth = w_in.shape[0]
    mem_len = mem.shape[1]
    t = batch * seq
    assert d == RET_W * 4 and t % ROW_TILE == 0 and seq % ROW_TILE == 0
    assert seq % RET_CHUNK == 0 and seq % SSD_CHUNK == 0 and seq % (S5_CHUNK * S5_BLOCK_STEPS) == 0
    assert t % MOE_BLOCK == 0 and t % MOE_ROW_TILE == 0
    row = lambda v: v[None, :]
    h = x.reshape(t, d)

    w_kv = jnp.concatenate([w for i in range(depth) for w in (w_ck[i], w_cv[i])], axis=1).astype(BF16)
    kv = _norm_matmul(mem.reshape(batch * mem_len, d), row(mem_norm_g), w_kv, BF16,
                      tm=mem_len, tn=d)
    ret_tables = _retention_tables(seq)
    c0 = _QKVG_W + SSD_W + SSD_XBC_W
    c1 = c0 + SSD_HEADS

    w_pack = jnp.concatenate(
        [w_in[:, :, :c0], w_in[:, :, c1:], w_in[:, :, c0:c1], jnp.zeros((depth, d, LANES - SSD_HEADS), F32)],
        axis=2).astype(BF16)
    w_glu_b, w_out_b, w_cq_b, w_co_b = (w.astype(BF16) for w in (s5_w_glu, w_out, w_cq, w_co))
    s5_ops_all = jax.vmap(_s5_operators)(s5_A_re, s5_A_im, s5_B_re, s5_B_im, s5_C_re, s5_C_im, s5_log_step)

    for i in range(depth):
        qkvg, z, xbc, u, dt = _in_proj(h, row(norm_mix_g[i]), w_pack[i])
        out_r = _retention(qkvg, ret_tables, row(ret_gn_g[i]), batch, seq)
        out_m = _ssd(z, xbc, dt, ssd_conv_w[i], ssd_conv_b[i], ssd_dt_bias[i], ssd_A_log[i], ssd_D[i],
                     ssd_norm_g[i], batch, seq)
        y_s = _s5(u, [op[i] for op in s5_ops_all], batch, seq)
        w_r, b_r = _router_params(w_route_group[i], b_route_group[i], w_route_expert[i], b_route_expert[i])
        h_tiles, *routing = _post_mixer(
            h, out_r, out_m, y_s, u, row(s5_D[i]), w_glu_b[i], row(s5_b_glu[i]), w_out_b[i],
            row(norm_cross_g[i]), w_cq_b[i], kv, i, w_co_b[i], row(norm_ffn_g[i]), w_r, b_r, seq, mem_len)
        h = _moe(h_tiles, routing, row(norm_ffn_g[i]), w_gate, w_up, w_down, i, row(norm_final_g),
                 final_norm=(i == depth - 1))
    return h.reshape(batch, seq, d)
```

```python
import functools
import math

import jax
import jax.numpy as jnp
from jax import lax
from jax.experimental import pallas as pl
from jax.experimental.pallas import tpu as pltpu

F32 = jnp.float32
BF16 = jnp.bfloat16
HIGHEST = lax.Precision.HIGHEST

EPS = 1e-6
RET_HEADS = 4
RET_HEAD_DIM = 64
RET_W = RET_HEADS * RET_HEAD_DIM
ROPE_BASE = 10000.0
SSD_HEAD_DIM = 64
SSD_HEADS = 8
SSD_GROUPS = 2
SSD_STATE = 128
SSD_CONV = 4
SSD_W = SSD_HEADS * SSD_HEAD_DIM
SSD_XBC_W = SSD_W + 2 * SSD_GROUPS * SSD_STATE
S5_GROUP = 16
S5_GROUPS = 16
S5_STATE = 64
S5_W = S5_GROUP * S5_GROUPS
CROSS_HEADS = 4
MOE_GROUPS = 4
EXPERTS_PER_GROUP = 8
N_EXPERTS = MOE_GROUPS * EXPERTS_PER_GROUP

LANES = 128
ROW_TILE = 512
RET_CHUNK = 256
SSD_CHUNK = 128
S5_CHUNK = 8
S5_LANE_GROUPS = LANES // S5_GROUP
S5_HALVES = S5_W // LANES
S5_BLOCK_STEPS = 64
MOE_BLOCK = 512
MOE_ROW_TILE = 512
VMEM_LIMIT = 48 * 1024 * 1024


def _cparams(*sem):
    return pltpu.CompilerParams(dimension_semantics=sem, vmem_limit_bytes=VMEM_LIMIT)


def _rms(x, g):
    return x * lax.rsqrt(jnp.mean(x * x, axis=-1, keepdims=True) + EPS) * g


def _silu(x):
    return x * jax.nn.sigmoid(x)


def _dot(a, b):
    return jnp.dot(a, b, preferred_element_type=F32)


def _dot_nt(a, b):
    return lax.dot_general(a, b, (((1,), (1,)), ((), ())), preferred_element_type=F32)


def _dot_tn(a, b):
    return lax.dot_general(a, b, (((0,), (0,)), ((), ())), preferred_element_type=F32)


def _load_token_tiles(ref, n_rows):
    nt = ref.shape[0] // n_rows
    return jnp.concatenate([ref[pl.ds(j, n_rows, stride=nt), :] for j in range(nt)], axis=-1)


def _store_token_tiles(ref, x):
    n_rows = x.shape[0]
    nt = ref.shape[0] // n_rows
    for j in range(nt):
        ref[pl.ds(j, n_rows, stride=nt), :] = x[:, j * LANES:(j + 1) * LANES]


def _norm_matmul_kernel(x_ref, g_ref, w_ref, o_ref):
    xn = _rms(x_ref[...], g_ref[...]).astype(BF16)
    o_ref[...] = _dot(xn, w_ref[...]).astype(o_ref.dtype)


def _norm_matmul(x, g, w, out_dtype, tm, tn):
    m, d = x.shape
    n = w.shape[1]
    return pl.pallas_call(
        _norm_matmul_kernel,
        grid=(m // tm, n // tn),
        in_specs=[pl.BlockSpec((tm, d), lambda i, j: (i, 0)),
                  pl.BlockSpec((1, d), lambda i, j: (0, 0)),
                  pl.BlockSpec((d, tn), lambda i, j: (0, j))],
        out_specs=pl.BlockSpec((tm, tn), lambda i, j: (i, j)),
        out_shape=jax.ShapeDtypeStruct((m, n), out_dtype),
        compiler_params=_cparams("parallel", "parallel"),
        name="norm_matmul",
    )(x, g, w)


_QKVG_W = 4 * RET_W
_IN_SPLITS = (_QKVG_W, SSD_W, SSD_XBC_W, S5_W, LANES)


def _in_proj_kernel(h_ref, g_ref, w_ref, qkvg_ref, z_ref, xbc_ref, u_ref, dt_ref):
    xn = _rms(h_ref[...], g_ref[...]).astype(BF16)
    lo = 0
    for ref, width in zip((qkvg_ref, z_ref, xbc_ref, u_ref, dt_ref), _IN_SPLITS):
        ref[...] = _dot(xn, w_ref[:, lo:lo + width]).astype(ref.dtype)
        lo += width


def _in_proj(h, g, w_pack):
    t, d = h.shape
    tm = ROW_TILE
    n = w_pack.shape[1]
    dts = (BF16, BF16, BF16, F32, F32)
    return pl.pallas_call(
        _in_proj_kernel,
        grid=(t // tm,),
        in_specs=[pl.BlockSpec((tm, d), lambda i: (i, 0)),
                  pl.BlockSpec((1, d), lambda i: (0, 0)),
                  pl.BlockSpec((d, n), lambda i: (0, 0))],
        out_specs=[pl.BlockSpec((tm, w), lambda i: (i, 0)) for w in _IN_SPLITS],
        out_shape=[jax.ShapeDtypeStruct((t, w), dt) for w, dt in zip(_IN_SPLITS, dts)],
        compiler_params=_cparams("parallel"),
        name="in_proj",
    )(h, g, w_pack)


def _retention_kernel(qkvg_ref, cos_ref, sin_ref, decay_ref, qdec_ref, kdec_ref, cdec_ref, gn_ref,
                      out_ref, s_ref):
    @pl.when(pl.program_id(1) == 0)
    def _():
        s_ref[...] = jnp.zeros_like(s_ref)

    x = qkvg_ref[...]
    w = RET_W
    q = x[:, 0:w].astype(F32)
    k = x[:, w:2 * w].astype(F32)
    v = x[:, 2 * w:3 * w]
    g = x[:, 3 * w:4 * w].astype(F32)
    half = RET_HEAD_DIM // 2
    lane = lax.broadcasted_iota(jnp.int32, q.shape, 1)
    first_half = (lane % RET_HEAD_DIM) < half

    def rot(t):
        swapped = jnp.where(first_half, pltpu.roll(t, w - half, 1), pltpu.roll(t, half, 1))
        return t * cos_ref[...] + swapped * sin_ref[...]

    qr = rot(q)
    kr = rot(k) * (RET_HEAD_DIM ** -0.5)
    qb = qr.astype(BF16)
    kb = kr.astype(BF16)
    qd = (qr * qdec_ref[...]).astype(BF16)
    kd = (kr * kdec_ref[...]).astype(BF16)
    outs = []
    for h in range(RET_HEADS):
        sl = slice(h * RET_HEAD_DIM, (h + 1) * RET_HEAD_DIM)
        s = _dot_nt(qb[:, sl], kb[:, sl]) * decay_ref[h]
        state = s_ref[h]
        y = _dot(s.astype(BF16), v[:, sl]) + _dot(qd[:, sl], state.astype(BF16))
        s_ref[h] = state * cdec_ref[h] + _dot_tn(kd[:, sl], v[:, sl])
        outs.append(y * lax.rsqrt(jnp.mean(y * y, axis=-1, keepdims=True) + EPS))
    yr = jnp.concatenate(outs, axis=-1)
    out_ref[...] = (_silu(g) * (yr * gn_ref[...])).astype(out_ref.dtype)


def _retention_tables(seq):
    c = RET_CHUNK
    dh = RET_HEAD_DIM
    inv = ROPE_BASE ** (-jnp.arange(0, dh, 2, dtype=F32) / dh)
    ang = jnp.arange(seq, dtype=F32)[:, None] * inv[None, :]
    cos, sin = jnp.cos(ang), jnp.sin(ang)
    cos4 = jnp.tile(jnp.concatenate([cos, cos], axis=-1), (1, RET_HEADS))
    sin4 = jnp.tile(jnp.concatenate([-sin, sin], axis=-1), (1, RET_HEADS))
    lg = jnp.log1p(-(2.0 ** (-5.0 - jnp.arange(RET_HEADS, dtype=F32))))
    i = jnp.arange(c, dtype=F32)
    rel = i[:, None] - i[None, :]
    decay = jnp.where(rel[None] >= 0, jnp.exp(lg[:, None, None] * jnp.maximum(rel, 0.0)[None]), 0.0)
    per_head = lambda t: jnp.repeat(t.T, dh, axis=1)
    qdec = per_head(jnp.exp(lg[:, None] * (i + 1.0)[None]))
    kdec = per_head(jnp.exp(lg[:, None] * (c - 1.0 - i)[None]))
    cdec = jnp.broadcast_to(jnp.exp(lg * c)[:, None, None], (RET_HEADS, dh, dh))
    return cos4, sin4, decay, qdec, kdec, cdec


def _retention(qkvg, tables, gn, batch, seq):
    c = RET_CHUNK
    nc = seq // c
    cos4, sin4, decay, qdec, kdec, cdec = tables
    w = RET_W
    full = lambda shape: pl.BlockSpec(shape, lambda b, j: (0,) * len(shape))
    return pl.pallas_call(
        _retention_kernel,
        grid=(batch, nc),
        in_specs=[pl.BlockSpec((c, _QKVG_W), lambda b, j: (b * nc + j, 0)),
                  pl.BlockSpec((c, w), lambda b, j: (j, 0)),
                  pl.BlockSpec((c, w), lambda b, j: (j, 0)),
                  full((RET_HEADS, c, c)), full((c, w)), full((c, w)),
                  full((RET_HEADS, RET_HEAD_DIM, RET_HEAD_DIM)), full((1, w))],
        out_specs=pl.BlockSpec((c, w), lambda b, j: (b * nc + j, 0)),
        out_shape=jax.ShapeDtypeStruct((batch * seq, w), BF16),
        scratch_shapes=[pltpu.VMEM((RET_HEADS, RET_HEAD_DIM, RET_HEAD_DIM), F32)],
        compiler_params=_cparams("parallel", "arbitrary"),
        name="retention",
    )(qkvg, cos4, sin4, decay, qdec, kdec, cdec, gn)


def _split3(v):
    hi = v.astype(BF16)
    rest = v - hi.astype(F32)
    mid = rest.astype(BF16)
    return hi, mid, (rest - mid.astype(F32)).astype(BF16)


def _ssd_kernel(z_ref, xbc_ref, dt_ref, shift_ref, expand_ref, cw_ref, cb_ref, dtb_ref, a_ref, d_ref,
                ng_ref, out_ref, xcat_ref, s_ref):
    c = SSD_CHUNK
    p = SSD_HEAD_DIM
    n = SSD_STATE

    @pl.when(pl.program_id(1) == 0)
    def _():
        xcat_ref[0:c, :] = jnp.zeros((c, SSD_XBC_W), BF16)
        s_ref[...] = jnp.zeros_like(s_ref)

    x_cur = xbc_ref[...]
    xcat_ref[c:2 * c, :] = x_cur
    x_cat = xcat_ref[...]
    conv = cb_ref[...] + x_cur.astype(F32) * cw_ref[SSD_CONV - 1:SSD_CONV, :]
    shifted = _dot(shift_ref[...], x_cat)
    for j in range(SSD_CONV - 1):
        conv = conv + shifted[j * c:(j + 1) * c, :] * cw_ref[j:j + 1, :]
    xcat_ref[0:c, :] = x_cur
    act = _silu(conv)
    xs = act[:, :SSD_W]
    bm = act[:, SSD_W:SSD_W + SSD_GROUPS * n]
    cm = act[:, SSD_W + SSD_GROUPS * n:]

    dt_in = dt_ref[...] + dtb_ref[...]
    dt = jnp.maximum(dt_in, 0.0) + jnp.log1p(jnp.exp(-jnp.abs(dt_in)))
    row = lax.broadcasted_iota(jnp.int32, (c, c), 0)
    col = lax.broadcasted_iota(jnp.int32, (c, c), 1)
    causal = row >= col
    a_cum = jnp.dot(causal.astype(F32), dt * a_ref[...], precision=HIGHEST,
                    preferred_element_type=F32)
    a_cum_t = a_cum.T

    terms = _split3(dt) + _split3(a_cum)
    wide = _dot(jnp.concatenate(terms, axis=0), expand_ref[...])
    dt_w = wide[0:c] + wide[c:2 * c] + wide[2 * c:3 * c]
    a_w = wide[3 * c:4 * c] + wide[4 * c:5 * c] + wide[5 * c:6 * c]
    a_last_w = a_w[c - 1:c, :]
    chunk_dec_w = jnp.exp(a_last_w)
    xdt = xs * dt_w
    xdt_b = xdt.astype(BF16)
    xdec_b = (xdt * jnp.exp(a_last_w - a_w)).astype(BF16)

    heads_per_group = SSD_HEADS // SSD_GROUPS
    gw = heads_per_group * p
    y_diag, y_off = [], []
    for g in range(SSD_GROUPS):
        bg = bm[:, g * n:(g + 1) * n].astype(BF16)
        cg = cm[:, g * n:(g + 1) * n].astype(BF16)
        cb = _dot_nt(cg, bg)
        gl = slice(g * gw, (g + 1) * gw)
        states = s_ref[g]
        y_off.append(_dot(cg, states.astype(BF16)))
        s_ref[g] = states * chunk_dec_w[:, gl] + _dot_tn(bg, xdec_b[:, gl])
        for r in range(heads_per_group):
            h = g * heads_per_group + r
            lmat = jnp.exp(jnp.where(causal, a_cum[:, h:h + 1] - a_cum_t[h:h + 1, :], -jnp.inf))
            y_diag.append(_dot((cb * lmat).astype(BF16), xdt_b[:, h * p:(h + 1) * p]))
    y = (jnp.concatenate(y_diag, axis=-1) + jnp.concatenate(y_off, axis=-1) * jnp.exp(a_w)
         + xs * d_ref[...])
    out_ref[...] = _rms(y * _silu(z_ref[...].astype(F32)), ng_ref[...]).astype(out_ref.dtype)


def _ssd(z, xbc, dt, conv_w, conv_b, dt_bias, a_log, d_skip, norm_g, batch, seq):
    c = SSD_CHUNK
    nc = seq // c
    pad = lambda v: jnp.pad(v, (0, LANES - v.shape[0]))[None, :]
    a_neg = pad(-jnp.exp(a_log))
    d_wide = jnp.repeat(d_skip, SSD_HEAD_DIM)[None, :]
    t_idx = jnp.arange(c)[None, :, None]
    lag = (SSD_CONV - 1 - jnp.arange(SSD_CONV - 1))[:, None, None]
    shift = (jnp.arange(2 * c)[None, None, :] == c + t_idx - lag).astype(BF16).reshape(-1, 2 * c)
    expand = (jnp.arange(LANES)[:, None] == jnp.arange(SSD_W)[None, :] // SSD_HEAD_DIM).astype(BF16)
    full = lambda shape: pl.BlockSpec(shape, lambda b, j: (0,) * len(shape))
    blk = lambda w: pl.BlockSpec((c, w), lambda b, j: (b * nc + j, 0))
    return pl.pallas_call(
        _ssd_kernel,
        grid=(batch, nc),
        in_specs=[blk(SSD_W), blk(SSD_XBC_W), blk(LANES), full(((SSD_CONV - 1) * c, 2 * c)),
                  full((LANES, SSD_W)), full((SSD_CONV, SSD_XBC_W)), full((1, SSD_XBC_W)),
                  full((1, LANES)), full((1, LANES)), full((1, SSD_W)), full((1, SSD_W))],
        out_specs=blk(SSD_W),
        out_shape=jax.ShapeDtypeStruct((batch * seq, SSD_W), BF16),
        scratch_shapes=[pltpu.VMEM((2 * c, SSD_XBC_W), BF16),
                        pltpu.VMEM((SSD_GROUPS, SSD_STATE, SSD_W // SSD_GROUPS), F32)],
        compiler_params=_cparams("parallel", "arbitrary"),
        name="ssd",
    )(z, xbc, dt, shift, expand, conv_w, conv_b[None, :], pad(dt_bias), a_neg, d_wide, norm_g[None, :])


def _s5_kernel(u_ref, t1_ref, pre_ref, pim_ref, qre_ref, qim_ref, are_ref, aim_ref, y_ref,
               ere_ref, eim_ref, xre_ref, xim_ref, sre_ref, sim_ref):
    batch, tb, _ = u_ref.shape
    cs = S5_CHUNK
    ns = tb // cs

    @pl.when(pl.program_id(1) == 0)
    def _():
        sre_ref[...] = jnp.zeros_like(sre_ref)
        sim_ref[...] = jnp.zeros_like(sim_ref)

    u = jnp.concatenate(
        [jnp.concatenate([u_ref[b, pl.ds(s, ns, stride=cs), :] for s in range(cs)], axis=-1)
         for b in range(batch)], axis=0).astype(BF16)
    n_tiles = ere_ref.shape[0]
    lanes_of = lambda j: slice(j * LANES, (j + 1) * LANES)
    e_re = _dot(u, pre_ref[0])
    e_im = _dot(u, pim_ref[0])
    for j in range(n_tiles):
        ere_ref[j] = e_re[:, lanes_of(j)]
        eim_ref[j] = e_im[:, lanes_of(j)]
    shape = (batch, LANES)
    ar = [jnp.broadcast_to(are_ref[0, :, lanes_of(j)], shape) for j in range(n_tiles)]
    ai = [jnp.broadcast_to(aim_ref[0, :, lanes_of(j)], shape) for j in range(n_tiles)]

    def step(n, carry):
        rows = pl.ds(n, batch, stride=ns)
        out = []
        for j in range(n_tiles):
            xr, xi = carry[j]
            xre_ref[j, rows, :] = xr
            xim_ref[j, rows, :] = xi
            out.append((ar[j] * xr - ai[j] * xi + ere_ref[j, rows, :],
                        ar[j] * xi + ai[j] * xr + eim_ref[j, rows, :]))
        return tuple(out)

    init = tuple((sre_ref[j], sim_ref[j]) for j in range(n_tiles))
    final = lax.fori_loop(0, ns, step, init, unroll=8)
    for j in range(n_tiles):
        sre_ref[j], sim_ref[j] = final[j]
    x_re = jnp.concatenate([xre_ref[j] for j in range(n_tiles)], axis=-1).astype(BF16)
    x_im = jnp.concatenate([xim_ref[j] for j in range(n_tiles)], axis=-1).astype(BF16)
    y = _dot(u, t1_ref[0]) + _dot(x_re, qre_ref[0]) + _dot(x_im, qim_ref[0])
    for b in range(batch):
        for s in range(cs):
            y_ref[b, pl.ds(s, ns, stride=cs), :] = y[b * ns:(b + 1) * ns, s * LANES:(s + 1) * LANES]


def _s5_operators(a_re, a_im, b_re, b_im, c_re, c_im, log_step):
    cs = S5_CHUNK
    ein = functools.partial(jnp.einsum, precision=HIGHEST)
    delta = jnp.exp(log_step)[:, None]
    ar, ai = a_re, a_im
    mag = jnp.exp(ar * delta)
    ang = ai * delta
    lr, li = mag * jnp.cos(ang), mag * jnp.sin(ang)
    den = ar * ar + ai * ai
    nr, ni = lr - 1.0, li
    cr = (nr * ar + ni * ai) / den
    ci = (ni * ar - nr * ai) / den
    bbr = cr[..., None] * b_re - ci[..., None] * b_im
    bbi = cr[..., None] * b_im + ci[..., None] * b_re
    k = jnp.arange(cs + 1, dtype=F32)
    pmag = jnp.exp((ar * delta)[..., None] * k)
    pang = ang[..., None] * k
    pr, pi = pmag * jnp.cos(pang), pmag * jnp.sin(pang)
    clr = c_re[..., None] * pr[:, None] - c_im[..., None] * pi[:, None]
    cli = c_re[..., None] * pi[:, None] + c_im[..., None] * pr[:, None]
    kern = ein('gcpk,gpd->gkcd', clr, bbr) - ein('gcpk,gpd->gkcd', cli, bbi)
    kern = jnp.concatenate([kern[:, :cs], jnp.zeros_like(kern[:, :1])], axis=1)
    s = jnp.arange(cs)
    lag = jnp.where(s[None, :] >= s[:, None], s[None, :] - s[:, None], cs)
    nh, ng = S5_HALVES, S5_LANE_GROUPS
    eye = jnp.eye(ng, dtype=F32)
    halves = lambda t: t.reshape((nh, ng) + t.shape[1:])
    bd = halves(kern).transpose(0, 2, 1, 4, 3)
    bd = (bd[:, :, :, :, None, :] * eye[None, None, :, None, :, None]).reshape(nh, cs + 1, LANES, LANES)
    t1 = bd.astype(BF16)[:, lag].transpose(0, 1, 3, 2, 4).reshape(nh, cs * LANES, cs * LANES)
    rev = cs - 1 - s
    prr, pri = pr[..., rev], pi[..., rev]
    p_re = prr[..., None] * bbr[:, :, None] - pri[..., None] * bbi[:, :, None]
    p_im = prr[..., None] * bbi[:, :, None] + pri[..., None] * bbr[:, :, None]

    def flat_p(t):
        t = halves(t.transpose(0, 2, 3, 1)).transpose(0, 2, 1, 3, 4)
        t = t[:, :, :, :, None, :] * eye[None, None, :, None, :, None]
        return t.reshape(nh, cs * LANES, ng * S5_STATE).astype(BF16)

    def flat_q(t):
        t = halves(t.transpose(0, 2, 3, 1))
        t = t[:, :, :, :, None, :] * eye[None, :, None, None, :, None]
        return t.reshape(nh, ng * S5_STATE, cs * LANES).astype(BF16)

    a_chunk_re = pr[..., cs].reshape(nh, 1, ng * S5_STATE)
    a_chunk_im = pi[..., cs].reshape(nh, 1, ng * S5_STATE)
    return (t1, flat_p(p_re), flat_p(p_im), flat_q(clr[..., 1:]), flat_q(-cli[..., 1:]),
            a_chunk_re, a_chunk_im)


def _s5(u, ops, batch, seq):
    tb = S5_CHUNK * S5_BLOCK_STEPS
    rows = batch * S5_BLOCK_STEPS
    flat = S5_CHUNK * LANES
    nstate = S5_LANE_GROUPS * S5_STATE
    per_h = lambda a, b: pl.BlockSpec((1, a, b), lambda h, j: (h, 0, 0))
    seq_blk = pl.BlockSpec((batch, tb, LANES), lambda h, j: (0, j, h))
    y = pl.pallas_call(
        _s5_kernel,
        grid=(S5_HALVES, seq // tb),
        in_specs=[seq_blk, per_h(flat, flat), per_h(flat, nstate), per_h(flat, nstate),
                  per_h(nstate, flat), per_h(nstate, flat), per_h(1, nstate), per_h(1, nstate)],
        out_specs=seq_blk,
        out_shape=jax.ShapeDtypeStruct((batch, seq, S5_W), F32),
        scratch_shapes=([pltpu.VMEM((nstate // LANES, rows, LANES), F32)] * 4
                        + [pltpu.VMEM((nstate // LANES, batch, LANES), F32)] * 2),
        compiler_params=_cparams("parallel", "arbitrary"),
        name="s5",
    )(u.reshape(batch, seq, S5_W), *ops)
    return y.reshape(batch * seq, S5_W)


_GROUP_LANE0 = N_EXPERTS
_INFO_ROWS = 8


def _mixer_out_proj(h, out_r, out_m, y_s, u, d_s5, w_glu, b_glu, w_out):
    y = y_s + d_s5 * u
    g = jax.nn.gelu(y)
    s = g * jax.nn.sigmoid(_dot(g.astype(BF16), w_glu[...]) + b_glu)
    acc = _dot(out_r, w_out[0:RET_W, :])
    acc = acc + _dot(out_m, w_out[RET_W:RET_W + SSD_W, :])
    acc = acc + _dot(s.astype(BF16), w_out[RET_W + SSD_W:, :])
    return h + acc


def _cross_attention(h, g, wq_ref, k_ref, v_ref, wo_ref):
    d = h.shape[-1]
    dh = d // CROSS_HEADS
    q = _dot(_rms(h, g).astype(BF16), wq_ref[...]).astype(BF16)
    outs = []
    for i in range(CROSS_HEADS):
        sl = slice(i * dh, (i + 1) * dh)
        s = _dot_nt(q[:, sl], k_ref[:, sl]) * (dh ** -0.5)
        p = jnp.exp(s - jnp.max(s, axis=-1, keepdims=True))
        o = _dot(p.astype(BF16), v_ref[:, sl])
        outs.append(o / jnp.sum(p, axis=-1, keepdims=True))
    o = jnp.concatenate(outs, axis=-1).astype(BF16)
    return h + _dot(o, wo_ref[...])


def _route(xn, whi_ref, wlo_ref, bias, carry):
    x_hi = xn.astype(BF16)
    x_lo = (xn - x_hi.astype(F32)).astype(BF16)
    logits = _dot(x_hi, whi_ref[...]) + _dot(x_lo, whi_ref[...]) + _dot(x_hi, wlo_ref[...]) + bias
    tm = logits.shape[0]
    lane = lax.broadcasted_iota(jnp.int32, logits.shape, 1).astype(F32)
    neg = -jnp.inf

    def first_argmax(vals):
        m = jnp.max(vals, axis=-1, keepdims=True)
        return m, jnp.min(jnp.where(vals == m, lane, float(LANES)), axis=-1, keepdims=True)

    gl = jnp.where((lane >= _GROUP_LANE0) & (lane < _GROUP_LANE0 + MOE_GROUPS), logits, neg)
    gmax, glane = first_argmax(gl)
    pg = 1.0 / jnp.sum(jnp.exp(gl - gmax), axis=-1, keepdims=True)
    lo = (glane - _GROUP_LANE0) * EXPERTS_PER_GROUP
    el = jnp.where((lane >= lo) & (lane < lo + EXPERTS_PER_GROUP), logits, neg)
    m1, e1 = first_argmax(el)
    m2, e2 = first_argmax(jnp.where(lane == e1, neg, el))
    p2 = jnp.exp(m2 - m1)
    gate1 = pg / (1.0 + p2)
    gate2 = pg * p2 / (1.0 + p2)

    hot = jnp.where((lane == e1) | (lane == e2), 1.0, 0.0)
    row = lax.broadcasted_iota(jnp.int32, (tm, tm), 0)
    col = lax.broadcasted_iota(jnp.int32, (tm, tm), 1)
    before = jnp.where(row > col, 1.0, 0.0).astype(BF16)
    cum = _dot(before, hot.astype(BF16)) + carry
    rank1 = jnp.sum(jnp.where(lane == e1, cum, 0.0), axis=-1, keepdims=True)
    rank2 = jnp.sum(jnp.where(lane == e2, cum, 0.0), axis=-1, keepdims=True)

    info = jnp.zeros(logits.shape, F32)
    for i, val in enumerate((e1, e2, rank1, rank2, gate1, gate2)):
        info = jnp.where(lane == i, val, info)
    return info, carry + jnp.sum(hot, axis=0, keepdims=True)


def _post_mixer_kernel(h_ref, r_ref, m_ref, ys_ref, u_ref, d_ref, wg_ref, bg_ref, wo_ref,
                       gc_ref, wq_ref, k_ref, v_ref, wco_ref, gf_ref, whi_ref, wlo_ref, br_ref,
                       o_ref, info_ref, infot_ref, cnt_ref, carry_ref):
    @pl.when(pl.program_id(0) == 0)
    def _():
        carry_ref[...] = jnp.zeros_like(carry_ref)

    h = _mixer_out_proj(h_ref[...], r_ref[...], m_ref[...], ys_ref[...], u_ref[...], d_ref[...], wg_ref,
                        bg_ref[...], wo_ref)
    h = _cross_attention(h, gc_ref[...], wq_ref, k_ref, v_ref, wco_ref)
    _store_token_tiles(o_ref, h)
    info, carry = _route(_rms(h, gf_ref[...]), whi_ref, wlo_ref, br_ref[...], carry_ref[...])
    carry_ref[...] = carry
    cnt_ref[...] = carry
    info_ref[...] = info
    infot_ref[...] = info.T[0:_INFO_ROWS, :]


def _post_mixer(h, out_r, out_m, y_s, u, d_s5, w_glu, b_glu, w_out, g_cross, wq, kv, layer, w_co,
                g_ffn, w_r, b_r, seq, mem_len):
    t, d = h.shape
    nt = d // LANES
    tm = ROW_TILE
    tiles_per_seq = seq // tm
    w_hi = w_r.astype(BF16)
    w_lo = (w_r - w_hi.astype(F32)).astype(BF16)
    row = lambda w: pl.BlockSpec((tm, w), lambda i: (i, 0))
    full = lambda a, b: pl.BlockSpec((a, b), lambda i: (0, 0))
    mem_blk = lambda col: pl.BlockSpec((mem_len, d), lambda i: (i // tiles_per_seq, col))
    return pl.pallas_call(
        _post_mixer_kernel,
        grid=(t // tm,),
        in_specs=[row(d), row(RET_W), row(SSD_W), row(S5_W), row(S5_W),
                  full(1, S5_W), full(S5_W, S5_W), full(1, S5_W), full(d, d),
                  full(1, d), full(d, d), mem_blk(2 * layer), mem_blk(2 * layer + 1), full(d, d),
                  full(1, d), full(d, LANES), full(d, LANES), full(1, LANES)],
        out_specs=[pl.BlockSpec((tm * nt, LANES), lambda i: (i, 0)),
                   pl.BlockSpec((tm, LANES), lambda i: (i, 0)),
                   pl.BlockSpec((_INFO_ROWS, tm), lambda i: (0, i)),
                   pl.BlockSpec((1, LANES), lambda i: (0, 0))],
        out_shape=[jax.ShapeDtypeStruct((t * nt, LANES), F32), jax.ShapeDtypeStruct((t, LANES), F32),
                   jax.ShapeDtypeStruct((_INFO_ROWS, t), F32), jax.ShapeDtypeStruct((1, LANES), F32)],
        scratch_shapes=[pltpu.VMEM((1, LANES), F32)],
        compiler_params=_cparams("arbitrary"),
        name="post_mixer",
    )(h, out_r, out_m, y_s, u, d_s5, w_glu, b_glu, w_out, g_cross, wq, kv, kv, w_co, g_ffn, w_hi, w_lo, b_r)


def _token_copy(src_hbm, dst_vmem, src_row, dst_token, nt, sem):
    return pltpu.make_async_copy(src_hbm.at[pl.ds(pl.multiple_of(src_row, nt), nt), :],
                                 dst_vmem.at[pl.ds(dst_token * nt, nt), :], sem)


_ISSUE_UNROLL = 8


def _start_token_gather(src_hbm, idx_ref, n_tokens, dst, sem):
    nt = dst.shape[0] // n_tokens

    def body(j, carry):
        for p in range(2):
            r = 2 * j + p
            _token_copy(src_hbm, dst, idx_ref[0, 0, r], r, nt, sem).start(priority=p)
        return carry

    lax.fori_loop(0, n_tokens // 2, body, 0, unroll=_ISSUE_UNROLL)


def _wait_token_gather(src_hbm, dst, sem):
    pltpu.make_async_copy(src_hbm.at[pl.ds(0, dst.shape[0]), :], dst, sem).wait()


def _expert_kernel(be_ref, nu_ref, cur_ref, nxt_ref, h_hbm, g_ref, wg_ref, wu_ref, wd_ref, y_ref,
                   xbuf, wg_s, wu_s, wd_s, sem):
    i = pl.program_id(0)
    n_used = nu_ref[0]
    slot = i % 2
    used = i < n_used

    @pl.when((i == 0) & used)
    def _():
        _start_token_gather(h_hbm, cur_ref, MOE_BLOCK, xbuf.at[0], sem.at[0])

    @pl.when(i + 1 < n_used)
    def _():
        _start_token_gather(h_hbm, nxt_ref, MOE_BLOCK, xbuf.at[1 - slot], sem.at[1 - slot])

    @pl.when(used & ((i == 0) | (be_ref[i] != be_ref[jnp.maximum(i - 1, 0)])))
    def _():
        wg_s[...] = wg_ref[0, 0].astype(BF16)
        wu_s[...] = wu_ref[0, 0].astype(BF16)
        wd_s[...] = wd_ref[0, 0].astype(BF16)

    @pl.when(used)
    def _():
        _wait_token_gather(h_hbm, xbuf.at[slot], sem.at[slot])
        xn = _rms(_load_token_tiles(xbuf.at[slot], MOE_BLOCK), g_ref[...]).astype(BF16)
        hid = (_silu(_dot(xn, wg_s[...])) * _dot(xn, wu_s[...])).astype(BF16)
        _store_token_tiles(y_ref, _dot(hid, wd_s[...]))

    @pl.when(jnp.logical_not(used))
    def _():
        y_ref[...] = jnp.zeros_like(y_ref)


def _experts(h_tiles, g, block_e, n_used, src_row, w_gate, w_up, w_down, layer):
    d = w_gate.shape[-2]
    nt = d // LANES
    nb = block_e.shape[0]
    de = w_gate.shape[-1]
    idx_blk = lambda f: pl.BlockSpec((1, 1, MOE_BLOCK), lambda i, be, nu: (f(i), 0, 0),
                                     memory_space=pltpu.SMEM)
    grid_spec = pltpu.PrefetchScalarGridSpec(
        num_scalar_prefetch=2,
        grid=(nb,),
        in_specs=[idx_blk(lambda i: i), idx_blk(lambda i: jnp.minimum(i + 1, nb - 1)),
                  pl.BlockSpec(memory_space=pl.ANY),
                  pl.BlockSpec((1, d), lambda i, be, nu: (0, 0)),
                  pl.BlockSpec((1, 1, d, de), lambda i, be, nu: (layer, be[i], 0, 0)),
                  pl.BlockSpec((1, 1, d, de), lambda i, be, nu: (layer, be[i], 0, 0)),
                  pl.BlockSpec((1, 1, de, d), lambda i, be, nu: (layer, be[i], 0, 0))],
        out_specs=pl.BlockSpec((MOE_BLOCK * nt, LANES), lambda i, be, nu: (i, 0)),
        scratch_shapes=[pltpu.VMEM((2, MOE_BLOCK * nt, LANES), F32), pltpu.VMEM((d, de), BF16),
                        pltpu.VMEM((d, de), BF16), pltpu.VMEM((de, d), BF16),
                        pltpu.SemaphoreType.DMA((2,))],
    )
    src3 = src_row.reshape(nb, 1, MOE_BLOCK)
    return pl.pallas_call(
        _expert_kernel,
        grid_spec=grid_spec,
        out_shape=jax.ShapeDtypeStruct((nb * MOE_BLOCK * nt, LANES), F32),
        compiler_params=_cparams("arbitrary"),
        name="moe_experts",
    )(block_e, n_used, src3, src3, h_tiles, g, w_gate, w_up, w_down)


def _combine_kernel(cur_ref, nxt_ref, h_ref, info_ref, y_hbm, fg_ref, o_ref, ybuf, sem, *, final_norm):
    i = pl.program_id(0)
    slot = i % 2
    tm = info_ref.shape[0]
    nt = h_ref.shape[0] // tm

    def start_gather(dest_ref, s):
        def body(r, carry):
            for k in range(2):
                _token_copy(y_hbm, ybuf.at[s, k], dest_ref[0, 0, k * tm + r], r, nt,
                            sem.at[s, k]).start(priority=k)
            return carry

        lax.fori_loop(0, tm, body, 0, unroll=_ISSUE_UNROLL)

    @pl.when(i == 0)
    def _():
        start_gather(cur_ref, 0)

    @pl.when(i + 1 < pl.num_programs(0))
    def _():
        start_gather(nxt_ref, 1 - slot)

    for k in range(2):
        _wait_token_gather(y_hbm, ybuf.at[slot, k], sem.at[slot, k])
    info = info_ref[...]
    out = _load_token_tiles(h_ref, tm) + (info[:, 4:5] * _load_token_tiles(ybuf.at[slot, 0], tm)
                                          + info[:, 5:6] * _load_token_tiles(ybuf.at[slot, 1], tm))
    if final_norm:
        out = _rms(out, fg_ref[...])
    o_ref[...] = out


def _combine(h_tiles, info, dest, ybuf, final_g, final_norm):
    t = info.shape[0]
    d = final_g.shape[-1]
    nt = d // LANES
    tm = MOE_ROW_TILE
    n_tiles = t // tm
    dest3 = dest.reshape(2, n_tiles, tm).transpose(1, 0, 2).reshape(n_tiles, 1, 2 * tm)
    idx_blk = lambda f: pl.BlockSpec((1, 1, 2 * tm), lambda i: (f(i), 0, 0), memory_space=pltpu.SMEM)
    return pl.pallas_call(
        functools.partial(_combine_kernel, final_norm=final_norm),
        grid=(n_tiles,),
        in_specs=[idx_blk(lambda i: i), idx_blk(lambda i: jnp.minimum(i + 1, n_tiles - 1)),
                  pl.BlockSpec((tm * nt, LANES), lambda i: (i, 0)),
                  pl.BlockSpec((tm, LANES), lambda i: (i, 0)),
                  pl.BlockSpec(memory_space=pl.ANY),
                  pl.BlockSpec((1, d), lambda i: (0, 0))],
        out_specs=pl.BlockSpec((tm, d), lambda i: (i, 0)),
        out_shape=jax.ShapeDtypeStruct((t, d), F32),
        scratch_shapes=[pltpu.VMEM((2, 2, tm * nt, LANES), F32), pltpu.SemaphoreType.DMA((2, 2))],
        compiler_params=_cparams("arbitrary"),
        name="moe_combine",
    )(dest3, dest3, h_tiles, info, ybuf, final_g)


def _router_params(w_rg, b_rg, w_re, b_re):
    d = w_rg.shape[0]
    pad_cols = LANES - N_EXPERTS - MOE_GROUPS
    w_r = jnp.concatenate([w_re, w_rg, jnp.zeros((d, pad_cols), F32)], axis=1)
    b_r = jnp.concatenate([b_re, b_rg, jnp.zeros((pad_cols,), F32)])[None, :]
    return w_r, b_r


def _moe(h_tiles, routing, g, w_gate, w_up, w_down, layer, final_g, final_norm):
    info, infot, cnt = routing
    d = g.shape[-1]
    nt = d // LANES
    t = h_tiles.shape[0] // nt
    expert = infot[0:2].astype(jnp.int32)
    rank = infot[2:4].astype(jnp.int32)
    counts = cnt[0, :N_EXPERTS].astype(jnp.int32)
    padded = (counts + MOE_BLOCK - 1) // MOE_BLOCK * MOE_BLOCK
    pends = jnp.cumsum(padded)
    pstarts = pends - padded
    starts = jnp.cumsum(counts) - counts
    ids = jnp.arange(N_EXPERTS, dtype=jnp.int32)
    dest = jnp.sum(jnp.where(expert[..., None] == ids, pstarts, 0), axis=-1) + rank
    nb = (2 * t) // MOE_BLOCK + N_EXPERTS
    blk = jnp.arange(nb, dtype=jnp.int32)
    block_e = jnp.minimum(jnp.sum(pends[None, :] <= blk[:, None] * MOE_BLOCK, axis=1), N_EXPERTS - 1)
    block_e = block_e.astype(jnp.int32)
    n_used = (pends[-1:] // MOE_BLOCK).astype(jnp.int32)
    keys = expert * t + jnp.arange(t, dtype=jnp.int32)[None, :]
    sorted_tok = jnp.sort(keys.reshape(-1)) % t
    shift = jnp.sum(jnp.where(block_e[:, None] == ids, pstarts - starts, 0), axis=-1)
    pair = blk[:, None] * MOE_BLOCK + jnp.arange(MOE_BLOCK, dtype=jnp.int32)[None, :] - shift[:, None]
    src_tok = sorted_tok[jnp.clip(pair, 0, 2 * t - 1)]
    ybuf = _experts(h_tiles, g, block_e, n_used, src_tok * nt, w_gate, w_up, w_down, layer)
    return _combine(h_tiles, info, dest * nt, ybuf, final_g, final_norm)


def kernel(x, mem, norm_mix_g, w_in, ret_gn_g, ssd_conv_w, ssd_conv_b, ssd_dt_bias, ssd_A_log, ssd_D,
           ssd_norm_g, s5_A_re, s5_A_im, s5_B_re, s5_B_im, s5_C_re, s5_C_im, s5_log_step, s5_D, s5_w_glu,
           s5_b_glu, w_out, norm_cross_g, mem_norm_g, w_cq, w_ck, w_cv, w_co, norm_ffn_g, w_route_group,
           b_route_group, w_route_expert, b_route_expert, w_gate, w_up, w_down, norm_final_g):
    batch, seq, d = x.shape
    depth = w_in.shape[0]
    mem_len = mem.shape[1]
    t = batch * seq
    assert d == RET_W * 4 and t % ROW_TILE == 0 and seq % ROW_TILE == 0
    assert seq % RET_CHUNK == 0 and seq % SSD_CHUNK == 0 and seq % (S5_CHUNK * S5_BLOCK_STEPS) == 0
    assert t % MOE_BLOCK == 0 and t % MOE_ROW_TILE == 0
    row = lambda v: v[None, :]
    h = x.reshape(t, d)

    w_kv = jnp.concatenate([w for i in range(depth) for w in (w_ck[i], w_cv[i])], axis=1).astype(BF16)
    kv = _norm_matmul(mem.reshape(batch * mem_len, d), row(mem_norm_g), w_kv, BF16,
                      tm=mem_len, tn=d)
    ret_tables = _retention_tables(seq)
    c0 = _QKVG_W + SSD_W + SSD_XBC_W
    c1 = c0 + SSD_HEADS

    w_pack = jnp.concatenate(
        [w_in[:, :, :c0], w_in[:, :, c1:], w_in[:, :, c0:c1], jnp.zeros((depth, d, LANES - SSD_HEADS), F32)],
        axis=2).astype(BF16)
    w_glu_b, w_out_b, w_cq_b, w_co_b = (w.astype(BF16) for w in (s5_w_glu, w_out, w_cq, w_co))
    s5_ops_all = jax.vmap(_s5_operators)(s5_A_re, s5_A_im, s5_B_re, s5_B_im, s5_C_re, s5_C_im, s5_log_step)

    for i in range(depth):
        qkvg, z, xbc, u, dt = _in_proj(h, row(norm_mix_g[i]), w_pack[i])
        out_r = _retention(qkvg, ret_tables, row(ret_gn_g[i]), batch, seq)
        out_m = _ssd(z, xbc, dt, ssd_conv_w[i], ssd_conv_b[i], ssd_dt_bias[i], ssd_A_log[i], ssd_D[i],
                     ssd_norm_g[i], batch, seq)
        y_s = _s5(u, [op[i] for op in s5_ops_all], batch, seq)
        w_r, b_r = _router_params(w_route_group[i], b_route_group[i], w_route_expert[i], b_route_expert[i])
        h_tiles, *routing = _post_mixer(
            h, out_r, out_m, y_s, u, row(s5_D[i]), w_glu_b[i], row(s5_b_glu[i]), w_out_b[i],
            row(norm_cross_g[i]), w_cq_b[i], kv, i, w_co_b[i], row(norm_ffn_g[i]), w_r, b_r, seq, mem_len)
        h = _moe(h_tiles, routing, row(norm_ffn_g[i]), w_gate, w_up, w_down, i, row(norm_final_g),
                 final_norm=(i == depth - 1))
    return h.reshape(batch, seq, d)
```

```python
import functools
import math

import jax
import jax.numpy as jnp
from jax import lax
from jax.experimental import pallas as pl
from jax.experimental.pallas import tpu as pltpu

F32 = jnp.float32
BF16 = jnp.bfloat16
HIGHEST = lax.Precision.HIGHEST

EPS = 1e-6
RET_HEADS = 4
RET_HEAD_DIM = 64
RET_W = RET_HEADS * RET_HEAD_DIM
ROPE_BASE = 10000.0
SSD_HEAD_DIM = 64
SSD_HEADS = 8
SSD_GROUPS = 2
SSD_STATE = 128
SSD_CONV = 4
SSD_W = SSD_HEADS * SSD_HEAD_DIM
SSD_XBC_W = SSD_W + 2 * SSD_GROUPS * SSD_STATE
S5_GROUP = 16
S5_GROUPS = 16
S5_STATE = 64
S5_W = S5_GROUP * S5_GROUPS
CROSS_HEADS = 4
MOE_GROUPS = 4
EXPERTS_PER_GROUP = 8
N_EXPERTS = MOE_GROUPS * EXPERTS_PER_GROUP

LANES = 128
ROW_TILE = 512
RET_CHUNK = 256
SSD_CHUNK = 128
S5_CHUNK = 8
S5_LANE_GROUPS = LANES // S5_GROUP
S5_HALVES = S5_W // LANES
S5_BLOCK_STEPS = 64
MOE_BLOCK = 512
MOE_ROW_TILE = 256
VMEM_LIMIT = 48 * 1024 * 1024


def _cparams(*sem):
    return pltpu.CompilerParams(dimension_semantics=sem, vmem_limit_bytes=VMEM_LIMIT)


def _rms(x, g):
    return x * lax.rsqrt(jnp.mean(x * x, axis=-1, keepdims=True) + EPS) * g


def _silu(x):
    return x * jax.nn.sigmoid(x)


def _dot(a, b):
    return jnp.dot(a, b, preferred_element_type=F32)


def _dot_nt(a, b):
    return lax.dot_general(a, b, (((1,), (1,)), ((), ())), preferred_element_type=F32)


def _dot_tn(a, b):
    return lax.dot_general(a, b, (((0,), (0,)), ((), ())), preferred_element_type=F32)


def _load_token_tiles(ref, n_rows):
    nt = ref.shape[0] // n_rows
    return jnp.concatenate([ref[pl.ds(j, n_rows, stride=nt), :] for j in range(nt)], axis=-1)


def _store_token_tiles(ref, x):
    n_rows = x.shape[0]
    nt = ref.shape[0] // n_rows
    for j in range(nt):
        ref[pl.ds(j, n_rows, stride=nt), :] = x[:, j * LANES:(j + 1) * LANES]


def _norm_matmul_kernel(x_ref, g_ref, w_ref, o_ref):
    xn = _rms(x_ref[...], g_ref[...]).astype(BF16)
    o_ref[...] = _dot(xn, w_ref[...]).astype(o_ref.dtype)


def _norm_matmul(x, g, w, out_dtype, tm, tn):
    m, d = x.shape
    n = w.shape[1]
    return pl.pallas_call(
        _norm_matmul_kernel,
        grid=(m // tm, n // tn),
        in_specs=[pl.BlockSpec((tm, d), lambda i, j: (i, 0)),
                  pl.BlockSpec((1, d), lambda i, j: (0, 0)),
                  pl.BlockSpec((d, tn), lambda i, j: (0, j))],
        out_specs=pl.BlockSpec((tm, tn), lambda i, j: (i, j)),
        out_shape=jax.ShapeDtypeStruct((m, n), out_dtype),
        compiler_params=_cparams("parallel", "parallel"),
        name="norm_matmul",
    )(x, g, w)


_QKVG_W = 4 * RET_W
_IN_SPLITS = (_QKVG_W, SSD_W, SSD_XBC_W, S5_W, LANES)


def _in_proj_kernel(h_ref, g_ref, w_ref, qkvg_ref, z_ref, xbc_ref, u_ref, dt_ref):
    xn = _rms(h_ref[...], g_ref[...]).astype(BF16)
    lo = 0
    for ref, width in zip((qkvg_ref, z_ref, xbc_ref, u_ref, dt_ref), _IN_SPLITS):
        ref[...] = _dot(xn, w_ref[:, lo:lo + width]).astype(ref.dtype)
        lo += width


def _in_proj(h, g, w_pack):
    t, d = h.shape
    tm = ROW_TILE
    n = w_pack.shape[1]
    dts = (BF16, BF16, BF16, F32, F32)
    return pl.pallas_call(
        _in_proj_kernel,
        grid=(t // tm,),
        in_specs=[pl.BlockSpec((tm, d), lambda i: (i, 0)),
                  pl.BlockSpec((1, d), lambda i: (0, 0)),
                  pl.BlockSpec((d, n), lambda i: (0, 0))],
        out_specs=[pl.BlockSpec((tm, w), lambda i: (i, 0)) for w in _IN_SPLITS],
        out_shape=[jax.ShapeDtypeStruct((t, w), dt) for w, dt in zip(_IN_SPLITS, dts)],
        compiler_params=_cparams("parallel"),
        name="in_proj",
    )(h, g, w_pack)


def _retention_kernel(qkvg_ref, cos_ref, sin_ref, decay_ref, qdec_ref, kdec_ref, cdec_ref, gn_ref,
                      out_ref, s_ref):
    @pl.when(pl.program_id(1) == 0)
    def _():
        s_ref[...] = jnp.zeros_like(s_ref)

    x = qkvg_ref[...]
    w = RET_W
    q = x[:, 0:w].astype(F32)
    k = x[:, w:2 * w].astype(F32)
    v = x[:, 2 * w:3 * w]
    g = x[:, 3 * w:4 * w].astype(F32)
    half = RET_HEAD_DIM // 2
    lane = lax.broadcasted_iota(jnp.int32, q.shape, 1)
    first_half = (lane % RET_HEAD_DIM) < half

    def rot(t):
        swapped = jnp.where(first_half, pltpu.roll(t, w - half, 1), pltpu.roll(t, half, 1))
        return t * cos_ref[...] + swapped * sin_ref[...]

    qr = rot(q)
    kr = rot(k) * (RET_HEAD_DIM ** -0.5)
    qb = qr.astype(BF16)
    kb = kr.astype(BF16)
    qd = (qr * qdec_ref[...]).astype(BF16)
    kd = (kr * kdec_ref[...]).astype(BF16)
    outs = []
    for h in range(RET_HEADS):
        sl = slice(h * RET_HEAD_DIM, (h + 1) * RET_HEAD_DIM)
        s = _dot_nt(qb[:, sl], kb[:, sl]) * decay_ref[h]
        state = s_ref[h]
        y = _dot(s.astype(BF16), v[:, sl]) + _dot(qd[:, sl], state.astype(BF16))
        s_ref[h] = state * cdec_ref[h] + _dot_tn(kd[:, sl], v[:, sl])
        outs.append(y * lax.rsqrt(jnp.mean(y * y, axis=-1, keepdims=True) + EPS))
    yr = jnp.concatenate(outs, axis=-1)
    out_ref[...] = (_silu(g) * (yr * gn_ref[...])).astype(out_ref.dtype)


def _retention_tables(seq):
    c = RET_CHUNK
    dh = RET_HEAD_DIM
    inv = ROPE_BASE ** (-jnp.arange(0, dh, 2, dtype=F32) / dh)
    ang = jnp.arange(seq, dtype=F32)[:, None] * inv[None, :]
    cos, sin = jnp.cos(ang), jnp.sin(ang)
    cos4 = jnp.tile(jnp.concatenate([cos, cos], axis=-1), (1, RET_HEADS))
    sin4 = jnp.tile(jnp.concatenate([-sin, sin], axis=-1), (1, RET_HEADS))
    lg = jnp.log1p(-(2.0 ** (-5.0 - jnp.arange(RET_HEADS, dtype=F32))))
    i = jnp.arange(c, dtype=F32)
    rel = i[:, None] - i[None, :]
    decay = jnp.where(rel[None] >= 0, jnp.exp(lg[:, None, None] * jnp.maximum(rel, 0.0)[None]), 0.0)
    per_head = lambda t: jnp.repeat(t.T, dh, axis=1)
    qdec = per_head(jnp.exp(lg[:, None] * (i + 1.0)[None]))
    kdec = per_head(jnp.exp(lg[:, None] * (c - 1.0 - i)[None]))
    cdec = jnp.broadcast_to(jnp.exp(lg * c)[:, None, None], (RET_HEADS, dh, dh))
    return cos4, sin4, decay, qdec, kdec, cdec


def _retention(qkvg, tables, gn, batch, seq):
    c = RET_CHUNK
    nc = seq // c
    cos4, sin4, decay, qdec, kdec, cdec = tables
    w = RET_W
    full = lambda shape: pl.BlockSpec(shape, lambda b, j: (0,) * len(shape))
    return pl.pallas_call(
        _retention_kernel,
        grid=(batch, nc),
        in_specs=[pl.BlockSpec((c, _QKVG_W), lambda b, j: (b * nc + j, 0)),
                  pl.BlockSpec((c, w), lambda b, j: (j, 0)),
                  pl.BlockSpec((c, w), lambda b, j: (j, 0)),
                  full((RET_HEADS, c, c)), full((c, w)), full((c, w)),
                  full((RET_HEADS, RET_HEAD_DIM, RET_HEAD_DIM)), full((1, w))],
        out_specs=pl.BlockSpec((c, w), lambda b, j: (b * nc + j, 0)),
        out_shape=jax.ShapeDtypeStruct((batch * seq, w), BF16),
        scratch_shapes=[pltpu.VMEM((RET_HEADS, RET_HEAD_DIM, RET_HEAD_DIM), F32)],
        compiler_params=_cparams("parallel", "arbitrary"),
        name="retention",
    )(qkvg, cos4, sin4, decay, qdec, kdec, cdec, gn)


def _split3(v):
    hi = v.astype(BF16)
    rest = v - hi.astype(F32)
    mid = rest.astype(BF16)
    return hi, mid, (rest - mid.astype(F32)).astype(BF16)


def _ssd_kernel(z_ref, xbc_ref, dt_ref, shift_ref, expand_ref, cw_ref, cb_ref, dtb_ref, a_ref, d_ref,
                ng_ref, out_ref, xcat_ref, s_ref):
    c = SSD_CHUNK
    p = SSD_HEAD_DIM
    n = SSD_STATE

    @pl.when(pl.program_id(1) == 0)
    def _():
        xcat_ref[0:c, :] = jnp.zeros((c, SSD_XBC_W), BF16)
        s_ref[...] = jnp.zeros_like(s_ref)

    x_cur = xbc_ref[...]
    xcat_ref[c:2 * c, :] = x_cur
    x_cat = xcat_ref[...]
    conv = cb_ref[...] + x_cur.astype(F32) * cw_ref[SSD_CONV - 1:SSD_CONV, :]
    shifted = _dot(shift_ref[...], x_cat)
    for j in range(SSD_CONV - 1):
        conv = conv + shifted[j * c:(j + 1) * c, :] * cw_ref[j:j + 1, :]
    xcat_ref[0:c, :] = x_cur
    act = _silu(conv)
    xs = act[:, :SSD_W]
    bm = act[:, SSD_W:SSD_W + SSD_GROUPS * n]
    cm = act[:, SSD_W + SSD_GROUPS * n:]

    dt_in = dt_ref[...] + dtb_ref[...]
    dt = jnp.maximum(dt_in, 0.0) + jnp.log1p(jnp.exp(-jnp.abs(dt_in)))
    row = lax.broadcasted_iota(jnp.int32, (c, c), 0)
    col = lax.broadcasted_iota(jnp.int32, (c, c), 1)
    causal = row >= col
    a_cum = jnp.dot(causal.astype(F32), dt * a_ref[...], precision=HIGHEST,
                    preferred_element_type=F32)
    a_cum_t = a_cum.T

    terms = _split3(dt) + _split3(a_cum)
    wide = _dot(jnp.concatenate(terms, axis=0), expand_ref[...])
    dt_w = wide[0:c] + wide[c:2 * c] + wide[2 * c:3 * c]
    a_w = wide[3 * c:4 * c] + wide[4 * c:5 * c] + wide[5 * c:6 * c]
    a_last_w = a_w[c - 1:c, :]
    chunk_dec_w = jnp.exp(a_last_w)
    xdt = xs * dt_w
    xdt_b = xdt.astype(BF16)
    xdec_b = (xdt * jnp.exp(a_last_w - a_w)).astype(BF16)

    heads_per_group = SSD_HEADS // SSD_GROUPS
    gw = heads_per_group * p
    y_diag, y_off = [], []
    for g in range(SSD_GROUPS):
        bg = bm[:, g * n:(g + 1) * n].astype(BF16)
        cg = cm[:, g * n:(g + 1) * n].astype(BF16)
        cb = _dot_nt(cg, bg)
        gl = slice(g * gw, (g + 1) * gw)
        states = s_ref[g]
        y_off.append(_dot(cg, states.astype(BF16)))
        s_ref[g] = states * chunk_dec_w[:, gl] + _dot_tn(bg, xdec_b[:, gl])
        for r in range(heads_per_group):
            h = g * heads_per_group + r
            lmat = jnp.exp(jnp.where(causal, a_cum[:, h:h + 1] - a_cum_t[h:h + 1, :], -jnp.inf))
            y_diag.append(_dot((cb * lmat).astype(BF16), xdt_b[:, h * p:(h + 1) * p]))
    y = (jnp.concatenate(y_diag, axis=-1) + jnp.concatenate(y_off, axis=-1) * jnp.exp(a_w)
         + xs * d_ref[...])
    out_ref[...] = _rms(y * _silu(z_ref[...].astype(F32)), ng_ref[...]).astype(out_ref.dtype)


def _ssd(z, xbc, dt, conv_w, conv_b, dt_bias, a_log, d_skip, norm_g, batch, seq):
    c = SSD_CHUNK
    nc = seq // c
    pad = lambda v: jnp.pad(v, (0, LANES - v.shape[0]))[None, :]
    a_neg = pad(-jnp.exp(a_log))
    d_wide = jnp.repeat(d_skip, SSD_HEAD_DIM)[None, :]
    t_idx = jnp.arange(c)[None, :, None]
    lag = (SSD_CONV - 1 - jnp.arange(SSD_CONV - 1))[:, None, None]
    shift = (jnp.arange(2 * c)[None, None, :] == c + t_idx - lag).astype(BF16).reshape(-1, 2 * c)
    expand = (jnp.arange(LANES)[:, None] == jnp.arange(SSD_W)[None, :] // SSD_HEAD_DIM).astype(BF16)
    full = lambda shape: pl.BlockSpec(shape, lambda b, j: (0,) * len(shape))
    blk = lambda w: pl.BlockSpec((c, w), lambda b, j: (b * nc + j, 0))
    return pl.pallas_call(
        _ssd_kernel,
        grid=(batch, nc),
        in_specs=[blk(SSD_W), blk(SSD_XBC_W), blk(LANES), full(((SSD_CONV - 1) * c, 2 * c)),
                  full((LANES, SSD_W)), full((SSD_CONV, SSD_XBC_W)), full((1, SSD_XBC_W)),
                  full((1, LANES)), full((1, LANES)), full((1, SSD_W)), full((1, SSD_W))],
        out_specs=blk(SSD_W),
        out_shape=jax.ShapeDtypeStruct((batch * seq, SSD_W), BF16),
        scratch_shapes=[pltpu.VMEM((2 * c, SSD_XBC_W), BF16),
                        pltpu.VMEM((SSD_GROUPS, SSD_STATE, SSD_W // SSD_GROUPS), F32)],
        compiler_params=_cparams("parallel", "arbitrary"),
        name="ssd",
    )(z, xbc, dt, shift, expand, conv_w, conv_b[None, :], pad(dt_bias), a_neg, d_wide, norm_g[None, :])


def _s5_kernel(u_ref, t1_ref, pre_ref, pim_ref, qre_ref, qim_ref, are_ref, aim_ref, y_ref,
               ere_ref, eim_ref, xre_ref, xim_ref, sre_ref, sim_ref):
    batch, tb, _ = u_ref.shape
    cs = S5_CHUNK
    ns = tb // cs

    @pl.when(pl.program_id(1) == 0)
    def _():
        sre_ref[...] = jnp.zeros_like(sre_ref)
        sim_ref[...] = jnp.zeros_like(sim_ref)

    u = jnp.concatenate(
        [jnp.concatenate([u_ref[b, pl.ds(s, ns, stride=cs), :] for s in range(cs)], axis=-1)
         for b in range(batch)], axis=0).astype(BF16)
    n_tiles = ere_ref.shape[0]
    lanes_of = lambda j: slice(j * LANES, (j + 1) * LANES)
    e_re = _dot(u, pre_ref[0])
    e_im = _dot(u, pim_ref[0])
    for j in range(n_tiles):
        ere_ref[j] = e_re[:, lanes_of(j)]
        eim_ref[j] = e_im[:, lanes_of(j)]
    shape = (batch, LANES)
    ar = [jnp.broadcast_to(are_ref[0, :, lanes_of(j)], shape) for j in range(n_tiles)]
    ai = [jnp.broadcast_to(aim_ref[0, :, lanes_of(j)], shape) for j in range(n_tiles)]

    def step(n, carry):
        rows = pl.ds(n, batch, stride=ns)
        out = []
        for j in range(n_tiles):
            xr, xi = carry[j]
            xre_ref[j, rows, :] = xr
            xim_ref[j, rows, :] = xi
            out.append((ar[j] * xr - ai[j] * xi + ere_ref[j, rows, :],
                        ar[j] * xi + ai[j] * xr + eim_ref[j, rows, :]))
        return tuple(out)

    init = tuple((sre_ref[j], sim_ref[j]) for j in range(n_tiles))
    final = lax.fori_loop(0, ns, step, init, unroll=8)
    for j in range(n_tiles):
        sre_ref[j], sim_ref[j] = final[j]
    x_re = jnp.concatenate([xre_ref[j] for j in range(n_tiles)], axis=-1).astype(BF16)
    x_im = jnp.concatenate([xim_ref[j] for j in range(n_tiles)], axis=-1).astype(BF16)
    y = _dot(u, t1_ref[0]) + _dot(x_re, qre_ref[0]) + _dot(x_im, qim_ref[0])
    for b in range(batch):
        for s in range(cs):
            y_ref[b, pl.ds(s, ns, stride=cs), :] = y[b * ns:(b + 1) * ns, s * LANES:(s + 1) * LANES]


def _s5_operators(a_re, a_im, b_re, b_im, c_re, c_im, log_step):
    cs = S5_CHUNK
    ein = functools.partial(jnp.einsum, precision=HIGHEST)
    delta = jnp.exp(log_step)[:, None]
    ar, ai = a_re, a_im
    mag = jnp.exp(ar * delta)
    ang = ai * delta
    lr, li = mag * jnp.cos(ang), mag * jnp.sin(ang)
    den = ar * ar + ai * ai
    nr, ni = lr - 1.0, li
    cr = (nr * ar + ni * ai) / den
    ci = (ni * ar - nr * ai) / den
    bbr = cr[..., None] * b_re - ci[..., None] * b_im
    bbi = cr[..., None] * b_im + ci[..., None] * b_re
    k = jnp.arange(cs + 1, dtype=F32)
    pmag = jnp.exp((ar * delta)[..., None] * k)
    pang = ang[..., None] * k
    pr, pi = pmag * jnp.cos(pang), pmag * jnp.sin(pang)
    clr = c_re[..., None] * pr[:, None] - c_im[..., None] * pi[:, None]
    cli = c_re[..., None] * pi[:, None] + c_im[..., None] * pr[:, None]
    kern = ein('gcpk,gpd->gkcd', clr, bbr) - ein('gcpk,gpd->gkcd', cli, bbi)
    kern = jnp.concatenate([kern[:, :cs], jnp.zeros_like(kern[:, :1])], axis=1)
    s = jnp.arange(cs)
    lag = jnp.where(s[None, :] >= s[:, None], s[None, :] - s[:, None], cs)
    nh, ng = S5_HALVES, S5_LANE_GROUPS
    eye = jnp.eye(ng, dtype=F32)
    halves = lambda t: t.reshape((nh, ng) + t.shape[1:])
    bd = halves(kern).transpose(0, 2, 1, 4, 3)
    bd = (bd[:, :, :, :, None, :] * eye[None, None, :, None, :, None]).reshape(nh, cs + 1, LANES, LANES)
    t1 = bd.astype(BF16)[:, lag].transpose(0, 1, 3, 2, 4).reshape(nh, cs * LANES, cs * LANES)
    rev = cs - 1 - s
    prr, pri = pr[..., rev], pi[..., rev]
    p_re = prr[..., None] * bbr[:, :, None] - pri[..., None] * bbi[:, :, None]
    p_im = prr[..., None] * bbi[:, :, None] + pri[..., None] * bbr[:, :, None]

    def flat_p(t):
        t = halves(t.transpose(0, 2, 3, 1)).transpose(0, 2, 1, 3, 4)
        t = t[:, :, :, :, None, :] * eye[None, None, :, None, :, None]
        return t.reshape(nh, cs * LANES, ng * S5_STATE).astype(BF16)

    def flat_q(t):
        t = halves(t.transpose(0, 2, 3, 1))
        t = t[:, :, :, :, None, :] * eye[None, :, None, None, :, None]
        return t.reshape(nh, ng * S5_STATE, cs * LANES).astype(BF16)

    a_chunk_re = pr[..., cs].reshape(nh, 1, ng * S5_STATE)
    a_chunk_im = pi[..., cs].reshape(nh, 1, ng * S5_STATE)
    return (t1, flat_p(p_re), flat_p(p_im), flat_q(clr[..., 1:]), flat_q(-cli[..., 1:]),
            a_chunk_re, a_chunk_im)


def _s5(u, ops, batch, seq):
    tb = S5_CHUNK * S5_BLOCK_STEPS
    rows = batch * S5_BLOCK_STEPS
    flat = S5_CHUNK * LANES
    nstate = S5_LANE_GROUPS * S5_STATE
    per_h = lambda a, b: pl.BlockSpec((1, a, b), lambda h, j: (h, 0, 0))
    seq_blk = pl.BlockSpec((batch, tb, LANES), lambda h, j: (0, j, h))
    y = pl.pallas_call(
        _s5_kernel,
        grid=(S5_HALVES, seq // tb),
        in_specs=[seq_blk, per_h(flat, flat), per_h(flat, nstate), per_h(flat, nstate),
                  per_h(nstate, flat), per_h(nstate, flat), per_h(1, nstate), per_h(1, nstate)],
        out_specs=seq_blk,
        out_shape=jax.ShapeDtypeStruct((batch, seq, S5_W), F32),
        scratch_shapes=([pltpu.VMEM((nstate // LANES, rows, LANES), F32)] * 4
                        + [pltpu.VMEM((nstate // LANES, batch, LANES), F32)] * 2),
        compiler_params=_cparams("parallel", "arbitrary"),
        name="s5",
    )(u.reshape(batch, seq, S5_W), *ops)
    return y.reshape(batch * seq, S5_W)


_GROUP_LANE0 = N_EXPERTS
_INFO_ROWS = 8


def _mixer_out_proj(h, out_r, out_m, y_s, u, d_s5, w_glu, b_glu, w_out):
    y = y_s + d_s5 * u
    g = jax.nn.gelu(y)
    s = g * jax.nn.sigmoid(_dot(g.astype(BF16), w_glu[...]) + b_glu)
    acc = _dot(out_r, w_out[0:RET_W, :])
    acc = acc + _dot(out_m, w_out[RET_W:RET_W + SSD_W, :])
    acc = acc + _dot(s.astype(BF16), w_out[RET_W + SSD_W:, :])
    return h + acc


def _cross_attention(h, g, wq_ref, k_ref, v_ref, wo_ref):
    d = h.shape[-1]
    dh = d // CROSS_HEADS
    q = _dot(_rms(h, g).astype(BF16), wq_ref[...]).astype(BF16)
    outs = []
    for i in range(CROSS_HEADS):
        sl = slice(i * dh, (i + 1) * dh)
        s = _dot_nt(q[:, sl], k_ref[:, sl]) * (dh ** -0.5)
        p = jnp.exp(s - jnp.max(s, axis=-1, keepdims=True))
        o = _dot(p.astype(BF16), v_ref[:, sl])
        outs.append(o / jnp.sum(p, axis=-1, keepdims=True))
    o = jnp.concatenate(outs, axis=-1).astype(BF16)
    return h + _dot(o, wo_ref[...])


def _route(xn, whi_ref, wlo_ref, bias, carry):
    x_hi = xn.astype(BF16)
    x_lo = (xn - x_hi.astype(F32)).astype(BF16)
    logits = _dot(x_hi, whi_ref[...]) + _dot(x_lo, whi_ref[...]) + _dot(x_hi, wlo_ref[...]) + bias
    tm = logits.shape[0]
    lane = lax.broadcasted_iota(jnp.int32, logits.shape, 1).astype(F32)
    neg = -jnp.inf

    def first_argmax(vals):
        m = jnp.max(vals, axis=-1, keepdims=True)
        return m, jnp.min(jnp.where(vals == m, lane, float(LANES)), axis=-1, keepdims=True)

    gl = jnp.where((lane >= _GROUP_LANE0) & (lane < _GROUP_LANE0 + MOE_GROUPS), logits, neg)
    gmax, glane = first_argmax(gl)
    pg = 1.0 / jnp.sum(jnp.exp(gl - gmax), axis=-1, keepdims=True)
    lo = (glane - _GROUP_LANE0) * EXPERTS_PER_GROUP
    el = jnp.where((lane >= lo) & (lane < lo + EXPERTS_PER_GROUP), logits, neg)
    m1, e1 = first_argmax(el)
    m2, e2 = first_argmax(jnp.where(lane == e1, neg, el))
    p2 = jnp.exp(m2 - m1)
    gate1 = pg / (1.0 + p2)
    gate2 = pg * p2 / (1.0 + p2)

    hot = jnp.where((lane == e1) | (lane == e2), 1.0, 0.0)
    row = lax.broadcasted_iota(jnp.int32, (tm, tm), 0)
    col = lax.broadcasted_iota(jnp.int32, (tm, tm), 1)
    before = jnp.where(row > col, 1.0, 0.0).astype(BF16)
    cum = _dot(before, hot.astype(BF16)) + carry
    rank1 = jnp.sum(jnp.where(lane == e1, cum, 0.0), axis=-1, keepdims=True)
    rank2 = jnp.sum(jnp.where(lane == e2, cum, 0.0), axis=-1, keepdims=True)

    info = jnp.zeros(logits.shape, F32)
    for i, val in enumerate((e1, e2, rank1, rank2, gate1, gate2)):
        info = jnp.where(lane == i, val, info)
    return info, carry + jnp.sum(hot, axis=0, keepdims=True)


def _post_mixer_kernel(h_ref, r_ref, m_ref, ys_ref, u_ref, d_ref, wg_ref, bg_ref, wo_ref,
                       gc_ref, wq_ref, k_ref, v_ref, wco_ref, gf_ref, whi_ref, wlo_ref, br_ref,
                       o_ref, info_ref, infot_ref, cnt_ref, carry_ref):
    @pl.when(pl.program_id(0) == 0)
    def _():
        carry_ref[...] = jnp.zeros_like(carry_ref)

    h = _mixer_out_proj(h_ref[...], r_ref[...], m_ref[...], ys_ref[...], u_ref[...], d_ref[...], wg_ref,
                        bg_ref[...], wo_ref)
    h = _cross_attention(h, gc_ref[...], wq_ref, k_ref, v_ref, wco_ref)
    _store_token_tiles(o_ref, h)
    info, carry = _route(_rms(h, gf_ref[...]), whi_ref, wlo_ref, br_ref[...], carry_ref[...])
    carry_ref[...] = carry
    cnt_ref[...] = carry
    info_ref[...] = info
    infot_ref[...] = info.T[0:_INFO_ROWS, :]


def _post_mixer(h, out_r, out_m, y_s, u, d_s5, w_glu, b_glu, w_out, g_cross, wq, kv, layer, w_co,
                g_ffn, w_r, b_r, seq, mem_len):
    t, d = h.shape
    nt = d // LANES
    tm = ROW_TILE
    tiles_per_seq = seq // tm
    w_hi = w_r.astype(BF16)
    w_lo = (w_r - w_hi.astype(F32)).astype(BF16)
    row = lambda w: pl.BlockSpec((tm, w), lambda i: (i, 0))
    full = lambda a, b: pl.BlockSpec((a, b), lambda i: (0, 0))
    mem_blk = lambda col: pl.BlockSpec((mem_len, d), lambda i: (i // tiles_per_seq, col))
    return pl.pallas_call(
        _post_mixer_kernel,
        grid=(t // tm,),
        in_specs=[row(d), row(RET_W), row(SSD_W), row(S5_W), row(S5_W),
                  full(1, S5_W), full(S5_W, S5_W), full(1, S5_W), full(d, d),
                  full(1, d), full(d, d), mem_blk(2 * layer), mem_blk(2 * layer + 1), full(d, d),
                  full(1, d), full(d, LANES), full(d, LANES), full(1, LANES)],
        out_specs=[pl.BlockSpec((tm * nt, LANES), lambda i: (i, 0)),
                   pl.BlockSpec((tm, LANES), lambda i: (i, 0)),
                   pl.BlockSpec((_INFO_ROWS, tm), lambda i: (0, i)),
                   pl.BlockSpec((1, LANES), lambda i: (0, 0))],
        out_shape=[jax.ShapeDtypeStruct((t * nt, LANES), F32), jax.ShapeDtypeStruct((t, LANES), F32),
                   jax.ShapeDtypeStruct((_INFO_ROWS, t), F32), jax.ShapeDtypeStruct((1, LANES), F32)],
        scratch_shapes=[pltpu.VMEM((1, LANES), F32)],
        compiler_params=_cparams("arbitrary"),
        name="post_mixer",
    )(h, out_r, out_m, y_s, u, d_s5, w_glu, b_glu, w_out, g_cross, wq, kv, kv, w_co, g_ffn, w_hi, w_lo, b_r)


def _token_copy(src_hbm, dst_vmem, src_row, dst_token, nt, sem):
    return pltpu.make_async_copy(src_hbm.at[pl.ds(pl.multiple_of(src_row, nt), nt), :],
                                 dst_vmem.at[pl.ds(dst_token * nt, nt), :], sem)


_ISSUE_UNROLL = 8


def _start_token_gather(src_hbm, idx_ref, n_tokens, dst, sem):
    nt = dst.shape[0] // n_tokens

    def body(j, carry):
        for p in range(2):
            r = 2 * j + p
            _token_copy(src_hbm, dst, idx_ref[0, 0, r], r, nt, sem).start(priority=p)
        return carry

    lax.fori_loop(0, n_tokens // 2, body, 0, unroll=_ISSUE_UNROLL)


def _wait_token_gather(src_hbm, dst, sem):
    pltpu.make_async_copy(src_hbm.at[pl.ds(0, dst.shape[0]), :], dst, sem).wait()


def _expert_kernel(be_ref, nu_ref, cur_ref, nxt_ref, h_hbm, g_ref, wg_ref, wu_ref, wd_ref, y_ref,
                   xbuf, wg_s, wu_s, wd_s, sem):
    i = pl.program_id(0)
    n_used = nu_ref[0]
    slot = i % 2
    used = i < n_used

    @pl.when((i == 0) & used)
    def _():
        _start_token_gather(h_hbm, cur_ref, MOE_BLOCK, xbuf.at[0], sem.at[0])

    @pl.when(i + 1 < n_used)
    def _():
        _start_token_gather(h_hbm, nxt_ref, MOE_BLOCK, xbuf.at[1 - slot], sem.at[1 - slot])

    @pl.when(used & ((i == 0) | (be_ref[i] != be_ref[jnp.maximum(i - 1, 0)])))
    def _():
        wg_s[...] = wg_ref[0, 0].astype(BF16)
        wu_s[...] = wu_ref[0, 0].astype(BF16)
        wd_s[...] = wd_ref[0, 0].astype(BF16)

    @pl.when(used)
    def _():
        _wait_token_gather(h_hbm, xbuf.at[slot], sem.at[slot])
        xn = _rms(_load_token_tiles(xbuf.at[slot], MOE_BLOCK), g_ref[...]).astype(BF16)
        hid = (_silu(_dot(xn, wg_s[...])) * _dot(xn, wu_s[...])).astype(BF16)
        _store_token_tiles(y_ref, _dot(hid, wd_s[...]))

    @pl.when(jnp.logical_not(used))
    def _():
        y_ref[...] = jnp.zeros_like(y_ref)


def _experts(h_tiles, g, block_e, n_used, src_row, w_gate, w_up, w_down, layer):
    d = w_gate.shape[-2]
    nt = d // LANES
    nb = block_e.shape[0]
    de = w_gate.shape[-1]
    idx_blk = lambda f: pl.BlockSpec((1, 1, MOE_BLOCK), lambda i, be, nu: (f(i), 0, 0),
                                     memory_space=pltpu.SMEM)
    grid_spec = pltpu.PrefetchScalarGridSpec(
        num_scalar_prefetch=2,
        grid=(nb,),
        in_specs=[idx_blk(lambda i: i), idx_blk(lambda i: jnp.minimum(i + 1, nb - 1)),
                  pl.BlockSpec(memory_space=pl.ANY),
                  pl.BlockSpec((1, d), lambda i, be, nu: (0, 0)),
                  pl.BlockSpec((1, 1, d, de), lambda i, be, nu: (layer, be[i], 0, 0)),
                  pl.BlockSpec((1, 1, d, de), lambda i, be, nu: (layer, be[i], 0, 0)),
                  pl.BlockSpec((1, 1, de, d), lambda i, be, nu: (layer, be[i], 0, 0))],
        out_specs=pl.BlockSpec((MOE_BLOCK * nt, LANES), lambda i, be, nu: (i, 0)),
        scratch_shapes=[pltpu.VMEM((2, MOE_BLOCK * nt, LANES), F32), pltpu.VMEM((d, de), BF16),
                        pltpu.VMEM((d, de), BF16), pltpu.VMEM((de, d), BF16),
                        pltpu.SemaphoreType.DMA((2,))],
    )
    src3 = src_row.reshape(nb, 1, MOE_BLOCK)
    return pl.pallas_call(
        _expert_kernel,
        grid_spec=grid_spec,
        out_shape=jax.ShapeDtypeStruct((nb * MOE_BLOCK * nt, LANES), F32),
        compiler_params=_cparams("arbitrary"),
        name="moe_experts",
    )(block_e, n_used, src3, src3, h_tiles, g, w_gate, w_up, w_down)


def _combine_kernel(cur_ref, nxt_ref, h_ref, info_ref, y_hbm, fg_ref, o_ref, ybuf, sem, *, final_norm):
    i = pl.program_id(0)
    slot = i % 2
    tm = info_ref.shape[0]
    nt = h_ref.shape[0] // tm

    def start_gather(dest_ref, s):
        def body(r, carry):
            for k in range(2):
                _token_copy(y_hbm, ybuf.at[s, k], dest_ref[0, 0, k * tm + r], r, nt,
                            sem.at[s, k]).start(priority=k)
            return carry

        lax.fori_loop(0, tm, body, 0, unroll=_ISSUE_UNROLL)

    @pl.when(i == 0)
    def _():
        start_gather(cur_ref, 0)

    @pl.when(i + 1 < pl.num_programs(0))
    def _():
        start_gather(nxt_ref, 1 - slot)

    for k in range(2):
        _wait_token_gather(y_hbm, ybuf.at[slot, k], sem.at[slot, k])
    info = info_ref[...]
    out = _load_token_tiles(h_ref, tm) + (info[:, 4:5] * _load_token_tiles(ybuf.at[slot, 0], tm)
                                          + info[:, 5:6] * _load_token_tiles(ybuf.at[slot, 1], tm))
    if final_norm:
        out = _rms(out, fg_ref[...])
    o_ref[...] = out


def _combine(h_tiles, info, dest, ybuf, final_g, final_norm):
    t = info.shape[0]
    d = final_g.shape[-1]
    nt = d // LANES
    tm = MOE_ROW_TILE
    n_tiles = t // tm
    dest3 = dest.reshape(2, n_tiles, tm).transpose(1, 0, 2).reshape(n_tiles, 1, 2 * tm)
    idx_blk = lambda f: pl.BlockSpec((1, 1, 2 * tm), lambda i: (f(i), 0, 0), memory_space=pltpu.SMEM)
    return pl.pallas_call(
        functools.partial(_combine_kernel, final_norm=final_norm),
        grid=(n_tiles,),
        in_specs=[idx_blk(lambda i: i), idx_blk(lambda i: jnp.minimum(i + 1, n_tiles - 1)),
                  pl.BlockSpec((tm * nt, LANES), lambda i: (i, 0)),
                  pl.BlockSpec((tm, LANES), lambda i: (i, 0)),
                  pl.BlockSpec(memory_space=pl.ANY),
                  pl.BlockSpec((1, d), lambda i: (0, 0))],
        out_specs=pl.BlockSpec((tm, d), lambda i: (i, 0)),
        out_shape=jax.ShapeDtypeStruct((t, d), F32),
        scratch_shapes=[pltpu.VMEM((2, 2, tm * nt, LANES), F32), pltpu.SemaphoreType.DMA((2, 2))],
        compiler_params=_cparams("arbitrary"),
        name="moe_combine",
    )(dest3, dest3, h_tiles, info, ybuf, final_g)


def _router_params(w_rg, b_rg, w_re, b_re):
    d = w_rg.shape[0]
    pad_cols = LANES - N_EXPERTS - MOE_GROUPS
    w_r = jnp.concatenate([w_re, w_rg, jnp.zeros((d, pad_cols), F32)], axis=1)
    b_r = jnp.concatenate([b_re, b_rg, jnp.zeros((pad_cols,), F32)])[None, :]
    return w_r, b_r


def _moe(h_tiles, routing, g, w_gate, w_up, w_down, layer, final_g, final_norm):
    info, infot, cnt = routing
    d = g.shape[-1]
    nt = d // LANES
    t = h_tiles.shape[0] // nt
    expert = infot[0:2].astype(jnp.int32)
    rank = infot[2:4].astype(jnp.int32)
    counts = cnt[0, :N_EXPERTS].astype(jnp.int32)
    padded = (counts + MOE_BLOCK - 1) // MOE_BLOCK * MOE_BLOCK
    pends = jnp.cumsum(padded)
    pstarts = pends - padded
    ids = jnp.arange(N_EXPERTS, dtype=jnp.int32)
    dest = jnp.sum(jnp.where(expert[..., None] == ids, pstarts, 0), axis=-1) + rank
    nb = (2 * t) // MOE_BLOCK + N_EXPERTS
    blk = jnp.arange(nb, dtype=jnp.int32)
    block_e = jnp.minimum(jnp.sum(pends[None, :] <= blk[:, None] * MOE_BLOCK, axis=1), N_EXPERTS - 1)
    block_e = block_e.astype(jnp.int32)
    n_used = (pends[-1:] // MOE_BLOCK).astype(jnp.int32)
    real_keys = (expert * (2 * t) + jnp.arange(t, dtype=jnp.int32)[None, :]).reshape(-1)
    pad_ends = jnp.cumsum(padded - counts)
    j = jnp.arange(nb * MOE_BLOCK - 2 * t, dtype=jnp.int32)
    pad_expert = jnp.sum(pad_ends[None, :] <= j[:, None], axis=1).astype(jnp.int32)
    pad_keys = pad_expert * (2 * t) + t + j % t
    src_tok = jnp.sort(jnp.concatenate([real_keys, pad_keys])) % t
    ybuf = _experts(h_tiles, g, block_e, n_used, src_tok * nt, w_gate, w_up, w_down, layer)
    return _combine(h_tiles, info, dest * nt, ybuf, final_g, final_norm)


def kernel(x, mem, norm_mix_g, w_in, ret_gn_g, ssd_conv_w, ssd_conv_b, ssd_dt_bias, ssd_A_log, ssd_D,
           ssd_norm_g, s5_A_re, s5_A_im, s5_B_re, s5_B_im, s5_C_re, s5_C_im, s5_log_step, s5_D, s5_w_glu,
           s5_b_glu, w_out, norm_cross_g, mem_norm_g, w_cq, w_ck, w_cv, w_co, norm_ffn_g, w_route_group,
           b_route_group, w_route_expert, b_route_expert, w_gate, w_up, w_down, norm_final_g):
    batch, seq, d = x.shape
    depth = w_in.shape[0]
    mem_len = mem.shape[1]
    t = batch * seq
    assert d == RET_W * 4 and t % ROW_TILE == 0 and seq % ROW_TILE == 0
    assert seq % RET_CHUNK == 0 and seq % SSD_CHUNK == 0 and seq % (S5_CHUNK * S5_BLOCK_STEPS) == 0
    assert t % MOE_BLOCK == 0 and t % MOE_ROW_TILE == 0
    row = lambda v: v[None, :]
    h = x.reshape(t, d)

    w_kv = jnp.concatenate([w for i in range(depth) for w in (w_ck[i], w_cv[i])], axis=1).astype(BF16)
    kv = _norm_matmul(mem.reshape(batch * mem_len, d), row(mem_norm_g), w_kv, BF16,
                      tm=mem_len, tn=d)
    ret_tables = _retention_tables(seq)
    c0 = _QKVG_W + SSD_W + SSD_XBC_W
    c1 = c0 + SSD_HEADS

    w_pack = jnp.concatenate(
        [w_in[:, :, :c0], w_in[:, :, c1:], w_in[:, :, c0:c1], jnp.zeros((depth, d, LANES - SSD_HEADS), F32)],
        axis=2).astype(BF16)
    w_glu_b, w_out_b, w_cq_b, w_co_b = (w.astype(BF16) for w in (s5_w_glu, w_out, w_cq, w_co))
    s5_ops_all = jax.vmap(_s5_operators)(s5_A_re, s5_A_im, s5_B_re, s5_B_im, s5_C_re, s5_C_im, s5_log_step)

    for i in range(depth):
        qkvg, z, xbc, u, dt = _in_proj(h, row(norm_mix_g[i]), w_pack[i])
        out_r = _retention(qkvg, ret_tables, row(ret_gn_g[i]), batch, seq)
        out_m = _ssd(z, xbc, dt, ssd_conv_w[i], ssd_conv_b[i], ssd_dt_bias[i], ssd_A_log[i], ssd_D[i],
                     ssd_norm_g[i], batch, seq)
        y_s = _s5(u, [op[i] for op in s5_ops_all], batch, seq)
        w_r, b_r = _router_params(w_route_group[i], b_route_group[i], w_route_expert[i], b_route_expert[i])
        h_tiles, *routing = _post_mixer(
            h, out_r, out_m, y_s, u, row(s5_D[i]), w_glu_b[i], row(s5_b_glu[i]), w_out_b[i],
            row(norm_cross_g[i]), w_cq_b[i], kv, i, w_co_b[i], row(norm_ffn_g[i]), w_r, b_r, seq, mem_len)
        h = _moe(h_tiles, routing, row(norm_ffn_g[i]), w_gate, w_up, w_down, i, row(norm_final_g),
                 final_norm=(i == depth - 1))
    return h.reshape(batch, seq, d)
```

```python
import functools
import math

import jax
import jax.numpy as jnp
from jax import lax
from jax.experimental import pallas as pl
from jax.experimental.pallas import tpu as pltpu

F32 = jnp.float32
BF16 = jnp.bfloat16
HIGHEST = lax.Precision.HIGHEST

EPS = 1e-6
RET_HEADS = 4
RET_HEAD_DIM = 64
RET_W = RET_HEADS * RET_HEAD_DIM
ROPE_BASE = 10000.0
SSD_HEAD_DIM = 64
SSD_HEADS = 8
SSD_GROUPS = 2
SSD_STATE = 128
SSD_CONV = 4
SSD_W = SSD_HEADS * SSD_HEAD_DIM
SSD_XBC_W = SSD_W + 2 * SSD_GROUPS * SSD_STATE
S5_GROUP = 16
S5_GROUPS = 16
S5_STATE = 64
S5_W = S5_GROUP * S5_GROUPS
CROSS_HEADS = 4
MOE_GROUPS = 4
EXPERTS_PER_GROUP = 8
N_EXPERTS = MOE_GROUPS * EXPERTS_PER_GROUP

LANES = 128
ROW_TILE = 512
RET_CHUNK = 256
SSD_CHUNK = 128
S5_CHUNK = 8
S5_LANE_GROUPS = LANES // S5_GROUP
S5_HALVES = S5_W // LANES
S5_BLOCK_STEPS = 64
MOE_BLOCK = 512
MOE_ROW_TILE = 256
VMEM_LIMIT = 48 * 1024 * 1024


def _cparams(*sem):
    return pltpu.CompilerParams(dimension_semantics=sem, vmem_limit_bytes=VMEM_LIMIT)


def _rms(x, g):
    return x * lax.rsqrt(jnp.mean(x * x, axis=-1, keepdims=True) + EPS) * g


def _silu(x):
    return x * jax.nn.sigmoid(x)


def _dot(a, b):
    return jnp.dot(a, b, preferred_element_type=F32)


def _dot_nt(a, b):
    return lax.dot_general(a, b, (((1,), (1,)), ((), ())), preferred_element_type=F32)


def _dot_tn(a, b):
    return lax.dot_general(a, b, (((0,), (0,)), ((), ())), preferred_element_type=F32)


def _load_token_tiles(ref, n_rows):
    nt = ref.shape[0] // n_rows
    return jnp.concatenate([ref[pl.ds(j, n_rows, stride=nt), :] for j in range(nt)], axis=-1)


def _store_token_tiles(ref, x):
    n_rows = x.shape[0]
    nt = ref.shape[0] // n_rows
    for j in range(nt):
        ref[pl.ds(j, n_rows, stride=nt), :] = x[:, j * LANES:(j + 1) * LANES]


def _norm_matmul_kernel(x_ref, g_ref, w_ref, o_ref):
    xn = _rms(x_ref[...], g_ref[...]).astype(BF16)
    o_ref[...] = _dot(xn, w_ref[...]).astype(o_ref.dtype)


def _norm_matmul(x, g, w, out_dtype, tm, tn):
    m, d = x.shape
    n = w.shape[1]
    return pl.pallas_call(
        _norm_matmul_kernel,
        grid=(m // tm, n // tn),
        in_specs=[pl.BlockSpec((tm, d), lambda i, j: (i, 0)),
                  pl.BlockSpec((1, d), lambda i, j: (0, 0)),
                  pl.BlockSpec((d, tn), lambda i, j: (0, j))],
        out_specs=pl.BlockSpec((tm, tn), lambda i, j: (i, j)),
        out_shape=jax.ShapeDtypeStruct((m, n), out_dtype),
        compiler_params=_cparams("parallel", "parallel"),
        name="norm_matmul",
    )(x, g, w)


_QKVG_W = 4 * RET_W
_IN_SPLITS = (_QKVG_W, SSD_W, SSD_XBC_W, S5_W, LANES)


def _in_proj_kernel(h_ref, g_ref, w_ref, qkvg_ref, z_ref, xbc_ref, u_ref, dt_ref):
    xn = _rms(h_ref[...], g_ref[...]).astype(BF16)
    lo = 0
    for ref, width in zip((qkvg_ref, z_ref, xbc_ref, u_ref, dt_ref), _IN_SPLITS):
        ref[...] = _dot(xn, w_ref[:, lo:lo + width]).astype(ref.dtype)
        lo += width


def _in_proj(h, g, w_pack):
    t, d = h.shape
    tm = ROW_TILE
    n = w_pack.shape[1]
    dts = (BF16, BF16, BF16, F32, F32)
    return pl.pallas_call(
        _in_proj_kernel,
        grid=(t // tm,),
        in_specs=[pl.BlockSpec((tm, d), lambda i: (i, 0)),
                  pl.BlockSpec((1, d), lambda i: (0, 0)),
                  pl.BlockSpec((d, n), lambda i: (0, 0))],
        out_specs=[pl.BlockSpec((tm, w), lambda i: (i, 0)) for w in _IN_SPLITS],
        out_shape=[jax.ShapeDtypeStruct((t, w), dt) for w, dt in zip(_IN_SPLITS, dts)],
        compiler_params=_cparams("parallel"),
        name="in_proj",
    )(h, g, w_pack)


def _retention_kernel(qkvg_ref, cos_ref, sin_ref, decay_ref, qdec_ref, kdec_ref, cdec_ref, gn_ref,
                      out_ref, s_ref):
    @pl.when(pl.program_id(1) == 0)
    def _():
        s_ref[...] = jnp.zeros_like(s_ref)

    x = qkvg_ref[...]
    w = RET_W
    q = x[:, 0:w].astype(F32)
    k = x[:, w:2 * w].astype(F32)
    v = x[:, 2 * w:3 * w]
    g = x[:, 3 * w:4 * w].astype(F32)
    half = RET_HEAD_DIM // 2
    lane = lax.broadcasted_iota(jnp.int32, q.shape, 1)
    first_half = (lane % RET_HEAD_DIM) < half

    def rot(t):
        swapped = jnp.where(first_half, pltpu.roll(t, w - half, 1), pltpu.roll(t, half, 1))
        return t * cos_ref[...] + swapped * sin_ref[...]

    qr = rot(q)
    kr = rot(k) * (RET_HEAD_DIM ** -0.5)
    qb = qr.astype(BF16)
    kb = kr.astype(BF16)
    qd = (qr * qdec_ref[...]).astype(BF16)
    kd = (kr * kdec_ref[...]).astype(BF16)
    outs = []
    for h in range(RET_HEADS):
        sl = slice(h * RET_HEAD_DIM, (h + 1) * RET_HEAD_DIM)
        s = _dot_nt(qb[:, sl], kb[:, sl]) * decay_ref[h]
        state = s_ref[h]
        y = _dot(s.astype(BF16), v[:, sl]) + _dot(qd[:, sl], state.astype(BF16))
        s_ref[h] = state * cdec_ref[h] + _dot_tn(kd[:, sl], v[:, sl])
        outs.append(y * lax.rsqrt(jnp.mean(y * y, axis=-1, keepdims=True) + EPS))
    yr = jnp.concatenate(outs, axis=-1)
    out_ref[...] = (_silu(g) * (yr * gn_ref[...])).astype(out_ref.dtype)


def _retention_tables(seq):
    c = RET_CHUNK
    dh = RET_HEAD_DIM
    inv = ROPE_BASE ** (-jnp.arange(0, dh, 2, dtype=F32) / dh)
    ang = jnp.arange(seq, dtype=F32)[:, None] * inv[None, :]
    cos, sin = jnp.cos(ang), jnp.sin(ang)
    cos4 = jnp.tile(jnp.concatenate([cos, cos], axis=-1), (1, RET_HEADS))
    sin4 = jnp.tile(jnp.concatenate([-sin, sin], axis=-1), (1, RET_HEADS))
    lg = jnp.log1p(-(2.0 ** (-5.0 - jnp.arange(RET_HEADS, dtype=F32))))
    i = jnp.arange(c, dtype=F32)
    rel = i[:, None] - i[None, :]
    decay = jnp.where(rel[None] >= 0, jnp.exp(lg[:, None, None] * jnp.maximum(rel, 0.0)[None]), 0.0)
    per_head = lambda t: jnp.repeat(t.T, dh, axis=1)
    qdec = per_head(jnp.exp(lg[:, None] * (i + 1.0)[None]))
    kdec = per_head(jnp.exp(lg[:, None] * (c - 1.0 - i)[None]))
    cdec = jnp.broadcast_to(jnp.exp(lg * c)[:, None, None], (RET_HEADS, dh, dh))
    return cos4, sin4, decay, qdec, kdec, cdec


def _retention(qkvg, tables, gn, batch, seq):
    c = RET_CHUNK
    nc = seq // c
    cos4, sin4, decay, qdec, kdec, cdec = tables
    w = RET_W
    full = lambda shape: pl.BlockSpec(shape, lambda b, j: (0,) * len(shape))
    return pl.pallas_call(
        _retention_kernel,
        grid=(batch, nc),
        in_specs=[pl.BlockSpec((c, _QKVG_W), lambda b, j: (b * nc + j, 0)),
                  pl.BlockSpec((c, w), lambda b, j: (j, 0)),
                  pl.BlockSpec((c, w), lambda b, j: (j, 0)),
                  full((RET_HEADS, c, c)), full((c, w)), full((c, w)),
                  full((RET_HEADS, RET_HEAD_DIM, RET_HEAD_DIM)), full((1, w))],
        out_specs=pl.BlockSpec((c, w), lambda b, j: (b * nc + j, 0)),
        out_shape=jax.ShapeDtypeStruct((batch * seq, w), BF16),
        scratch_shapes=[pltpu.VMEM((RET_HEADS, RET_HEAD_DIM, RET_HEAD_DIM), F32)],
        compiler_params=_cparams("parallel", "arbitrary"),
        name="retention",
    )(qkvg, cos4, sin4, decay, qdec, kdec, cdec, gn)


def _split3(v):
    hi = v.astype(BF16)
    rest = v - hi.astype(F32)
    mid = rest.astype(BF16)
    return hi, mid, (rest - mid.astype(F32)).astype(BF16)


def _ssd_kernel(z_ref, xbc_ref, dt_ref, shift_ref, expand_ref, cw_ref, cb_ref, dtb_ref, a_ref, d_ref,
                ng_ref, out_ref, xcat_ref, s_ref):
    c = SSD_CHUNK
    p = SSD_HEAD_DIM
    n = SSD_STATE

    @pl.when(pl.program_id(1) == 0)
    def _():
        xcat_ref[0:c, :] = jnp.zeros((c, SSD_XBC_W), BF16)
        s_ref[...] = jnp.zeros_like(s_ref)

    x_cur = xbc_ref[...]
    xcat_ref[c:2 * c, :] = x_cur
    x_cat = xcat_ref[...]
    conv = cb_ref[...] + x_cur.astype(F32) * cw_ref[SSD_CONV - 1:SSD_CONV, :]
    shifted = _dot(shift_ref[...], x_cat)
    for j in range(SSD_CONV - 1):
        conv = conv + shifted[j * c:(j + 1) * c, :] * cw_ref[j:j + 1, :]
    xcat_ref[0:c, :] = x_cur
    act = _silu(conv)
    xs = act[:, :SSD_W]
    bm = act[:, SSD_W:SSD_W + SSD_GROUPS * n]
    cm = act[:, SSD_W + SSD_GROUPS * n:]

    dt_in = dt_ref[...] + dtb_ref[...]
    dt = jnp.maximum(dt_in, 0.0) + jnp.log1p(jnp.exp(-jnp.abs(dt_in)))
    row = lax.broadcasted_iota(jnp.int32, (c, c), 0)
    col = lax.broadcasted_iota(jnp.int32, (c, c), 1)
    causal = row >= col
    acc = _dot(causal.astype(BF16), jnp.concatenate(_split3(dt * a_ref[...]), axis=1))
    a_cum = acc[:, 0:LANES] + acc[:, LANES:2 * LANES] + acc[:, 2 * LANES:]
    a_cum_t = a_cum.T

    terms = _split3(dt) + _split3(a_cum)
    wide = _dot(jnp.concatenate(terms, axis=0), expand_ref[...])
    dt_w = wide[0:c] + wide[c:2 * c] + wide[2 * c:3 * c]
    a_w = wide[3 * c:4 * c] + wide[4 * c:5 * c] + wide[5 * c:6 * c]
    a_last_w = a_w[c - 1:c, :]
    chunk_dec_w = jnp.exp(a_last_w)
    xdt = xs * dt_w
    xdt_b = xdt.astype(BF16)
    xdec_b = (xdt * jnp.exp(a_last_w - a_w)).astype(BF16)

    heads_per_group = SSD_HEADS // SSD_GROUPS
    gw = heads_per_group * p
    y_diag, y_off = [], []
    for g in range(SSD_GROUPS):
        bg = bm[:, g * n:(g + 1) * n].astype(BF16)
        cg = cm[:, g * n:(g + 1) * n].astype(BF16)
        cb = _dot_nt(cg, bg)
        gl = slice(g * gw, (g + 1) * gw)
        states = s_ref[g]
        y_off.append(_dot(cg, states.astype(BF16)))
        s_ref[g] = states * chunk_dec_w[:, gl] + _dot_tn(bg, xdec_b[:, gl])
        for r in range(heads_per_group):
            h = g * heads_per_group + r
            lmat = jnp.exp(jnp.where(causal, a_cum[:, h:h + 1] - a_cum_t[h:h + 1, :], -jnp.inf))
            y_diag.append(_dot((cb * lmat).astype(BF16), xdt_b[:, h * p:(h + 1) * p]))
    y = (jnp.concatenate(y_diag, axis=-1) + jnp.concatenate(y_off, axis=-1) * jnp.exp(a_w)
         + xs * d_ref[...])
    out_ref[...] = _rms(y * _silu(z_ref[...].astype(F32)), ng_ref[...]).astype(out_ref.dtype)


def _ssd(z, xbc, dt, conv_w, conv_b, dt_bias, a_log, d_skip, norm_g, batch, seq):
    c = SSD_CHUNK
    nc = seq // c
    pad = lambda v: jnp.pad(v, (0, LANES - v.shape[0]))[None, :]
    a_neg = pad(-jnp.exp(a_log))
    d_wide = jnp.repeat(d_skip, SSD_HEAD_DIM)[None, :]
    t_idx = jnp.arange(c)[None, :, None]
    lag = (SSD_CONV - 1 - jnp.arange(SSD_CONV - 1))[:, None, None]
    shift = (jnp.arange(2 * c)[None, None, :] == c + t_idx - lag).astype(BF16).reshape(-1, 2 * c)
    expand = (jnp.arange(LANES)[:, None] == jnp.arange(SSD_W)[None, :] // SSD_HEAD_DIM).astype(BF16)
    full = lambda shape: pl.BlockSpec(shape, lambda b, j: (0,) * len(shape))
    blk = lambda w: pl.BlockSpec((c, w), lambda b, j: (b * nc + j, 0))
    return pl.pallas_call(
        _ssd_kernel,
        grid=(batch, nc),
        in_specs=[blk(SSD_W), blk(SSD_XBC_W), blk(LANES), full(((SSD_CONV - 1) * c, 2 * c)),
                  full((LANES, SSD_W)), full((SSD_CONV, SSD_XBC_W)), full((1, SSD_XBC_W)),
                  full((1, LANES)), full((1, LANES)), full((1, SSD_W)), full((1, SSD_W))],
        out_specs=blk(SSD_W),
        out_shape=jax.ShapeDtypeStruct((batch * seq, SSD_W), BF16),
        scratch_shapes=[pltpu.VMEM((2 * c, SSD_XBC_W), BF16),
                        pltpu.VMEM((SSD_GROUPS, SSD_STATE, SSD_W // SSD_GROUPS), F32)],
        compiler_params=_cparams("parallel", "arbitrary"),
        name="ssd",
    )(z, xbc, dt, shift, expand, conv_w, conv_b[None, :], pad(dt_bias), a_neg, d_wide, norm_g[None, :])


def _s5_kernel(u_ref, t1_ref, pre_ref, pim_ref, qre_ref, qim_ref, are_ref, aim_ref, y_ref,
               ere_ref, eim_ref, xre_ref, xim_ref, sre_ref, sim_ref):
    batch, tb, _ = u_ref.shape
    cs = S5_CHUNK
    ns = tb // cs

    @pl.when(pl.program_id(1) == 0)
    def _():
        sre_ref[...] = jnp.zeros_like(sre_ref)
        sim_ref[...] = jnp.zeros_like(sim_ref)

    u = jnp.concatenate(
        [jnp.concatenate([u_ref[b, pl.ds(s, ns, stride=cs), :] for s in range(cs)], axis=-1)
         for b in range(batch)], axis=0).astype(BF16)
    n_tiles = ere_ref.shape[0]
    lanes_of = lambda j: slice(j * LANES, (j + 1) * LANES)
    e_re = _dot(u, pre_ref[0])
    e_im = _dot(u, pim_ref[0])
    for j in range(n_tiles):
        ere_ref[j] = e_re[:, lanes_of(j)]
        eim_ref[j] = e_im[:, lanes_of(j)]
    shape = (batch, LANES)
    ar = [jnp.broadcast_to(are_ref[0, :, lanes_of(j)], shape) for j in range(n_tiles)]
    ai = [jnp.broadcast_to(aim_ref[0, :, lanes_of(j)], shape) for j in range(n_tiles)]

    def step(n, carry):
        rows = pl.ds(n, batch, stride=ns)
        out = []
        for j in range(n_tiles):
            xr, xi = carry[j]
            xre_ref[j, rows, :] = xr
            xim_ref[j, rows, :] = xi
            out.append((ar[j] * xr - ai[j] * xi + ere_ref[j, rows, :],
                        ar[j] * xi + ai[j] * xr + eim_ref[j, rows, :]))
        return tuple(out)

    init = tuple((sre_ref[j], sim_ref[j]) for j in range(n_tiles))
    final = lax.fori_loop(0, ns, step, init, unroll=8)
    for j in range(n_tiles):
        sre_ref[j], sim_ref[j] = final[j]
    x_re = jnp.concatenate([xre_ref[j] for j in range(n_tiles)], axis=-1).astype(BF16)
    x_im = jnp.concatenate([xim_ref[j] for j in range(n_tiles)], axis=-1).astype(BF16)
    y = _dot(u, t1_ref[0]) + _dot(x_re, qre_ref[0]) + _dot(x_im, qim_ref[0])
    for b in range(batch):
        for s in range(cs):
            y_ref[b, pl.ds(s, ns, stride=cs), :] = y[b * ns:(b + 1) * ns, s * LANES:(s + 1) * LANES]


def _s5_operators(a_re, a_im, b_re, b_im, c_re, c_im, log_step):
    cs = S5_CHUNK
    ein = functools.partial(jnp.einsum, precision=HIGHEST)
    delta = jnp.exp(log_step)[:, None]
    ar, ai = a_re, a_im
    mag = jnp.exp(ar * delta)
    ang = ai * delta
    lr, li = mag * jnp.cos(ang), mag * jnp.sin(ang)
    den = ar * ar + ai * ai
    nr, ni = lr - 1.0, li
    cr = (nr * ar + ni * ai) / den
    ci = (ni * ar - nr * ai) / den
    bbr = cr[..., None] * b_re - ci[..., None] * b_im
    bbi = cr[..., None] * b_im + ci[..., None] * b_re
    k = jnp.arange(cs + 1, dtype=F32)
    pmag = jnp.exp((ar * delta)[..., None] * k)
    pang = ang[..., None] * k
    pr, pi = pmag * jnp.cos(pang), pmag * jnp.sin(pang)
    clr = c_re[..., None] * pr[:, None] - c_im[..., None] * pi[:, None]
    cli = c_re[..., None] * pi[:, None] + c_im[..., None] * pr[:, None]
    kern = ein('gcpk,gpd->gkcd', clr, bbr) - ein('gcpk,gpd->gkcd', cli, bbi)
    kern = jnp.concatenate([kern[:, :cs], jnp.zeros_like(kern[:, :1])], axis=1)
    s = jnp.arange(cs)
    lag = jnp.where(s[None, :] >= s[:, None], s[None, :] - s[:, None], cs)
    nh, ng = S5_HALVES, S5_LANE_GROUPS
    eye = jnp.eye(ng, dtype=F32)
    halves = lambda t: t.reshape((nh, ng) + t.shape[1:])
    bd = halves(kern).transpose(0, 2, 1, 4, 3)
    bd = (bd[:, :, :, :, None, :] * eye[None, None, :, None, :, None]).reshape(nh, cs + 1, LANES, LANES)
    t1 = bd.astype(BF16)[:, lag].transpose(0, 1, 3, 2, 4).reshape(nh, cs * LANES, cs * LANES)
    rev = cs - 1 - s
    prr, pri = pr[..., rev], pi[..., rev]
    p_re = prr[..., None] * bbr[:, :, None] - pri[..., None] * bbi[:, :, None]
    p_im = prr[..., None] * bbi[:, :, None] + pri[..., None] * bbr[:, :, None]

    def flat_p(t):
        t = halves(t.transpose(0, 2, 3, 1)).transpose(0, 2, 1, 3, 4)
        t = t[:, :, :, :, None, :] * eye[None, None, :, None, :, None]
        return t.reshape(nh, cs * LANES, ng * S5_STATE).astype(BF16)

    def flat_q(t):
        t = halves(t.transpose(0, 2, 3, 1))
        t = t[:, :, :, :, None, :] * eye[None, :, None, None, :, None]
        return t.reshape(nh, ng * S5_STATE, cs * LANES).astype(BF16)

    a_chunk_re = pr[..., cs].reshape(nh, 1, ng * S5_STATE)
    a_chunk_im = pi[..., cs].reshape(nh, 1, ng * S5_STATE)
    return (t1, flat_p(p_re), flat_p(p_im), flat_q(clr[..., 1:]), flat_q(-cli[..., 1:]),
            a_chunk_re, a_chunk_im)


def _s5(u, ops, batch, seq):
    tb = S5_CHUNK * S5_BLOCK_STEPS
    rows = batch * S5_BLOCK_STEPS
    flat = S5_CHUNK * LANES
    nstate = S5_LANE_GROUPS * S5_STATE
    per_h = lambda a, b: pl.BlockSpec((1, a, b), lambda h, j: (h, 0, 0))
    seq_blk = pl.BlockSpec((batch, tb, LANES), lambda h, j: (0, j, h))
    y = pl.pallas_call(
        _s5_kernel,
        grid=(S5_HALVES, seq // tb),
        in_specs=[seq_blk, per_h(flat, flat), per_h(flat, nstate), per_h(flat, nstate),
                  per_h(nstate, flat), per_h(nstate, flat), per_h(1, nstate), per_h(1, nstate)],
        out_specs=seq_blk,
        out_shape=jax.ShapeDtypeStruct((batch, seq, S5_W), F32),
        scratch_shapes=([pltpu.VMEM((nstate // LANES, rows, LANES), F32)] * 4
                        + [pltpu.VMEM((nstate // LANES, batch, LANES), F32)] * 2),
        compiler_params=_cparams("parallel", "arbitrary"),
        name="s5",
    )(u.reshape(batch, seq, S5_W), *ops)
    return y.reshape(batch * seq, S5_W)


_GROUP_LANE0 = N_EXPERTS
_INFO_ROWS = 8


def _mixer_out_proj(h, out_r, out_m, y_s, u, d_s5, w_glu, b_glu, w_out):
    y = y_s + d_s5 * u
    g = jax.nn.gelu(y)
    s = g * jax.nn.sigmoid(_dot(g.astype(BF16), w_glu[...]) + b_glu)
    acc = _dot(out_r, w_out[0:RET_W, :])
    acc = acc + _dot(out_m, w_out[RET_W:RET_W + SSD_W, :])
    acc = acc + _dot(s.astype(BF16), w_out[RET_W + SSD_W:, :])
    return h + acc


def _cross_attention(h, g, wq_ref, k_ref, v_ref, wo_ref):
    d = h.shape[-1]
    dh = d // CROSS_HEADS
    q = _dot(_rms(h, g).astype(BF16), wq_ref[...]).astype(BF16)
    outs = []
    for i in range(CROSS_HEADS):
        sl = slice(i * dh, (i + 1) * dh)
        s = _dot_nt(q[:, sl], k_ref[:, sl]) * (dh ** -0.5)
        p = jnp.exp(s - jnp.max(s, axis=-1, keepdims=True))
        o = _dot(p.astype(BF16), v_ref[:, sl])
        outs.append(o / jnp.sum(p, axis=-1, keepdims=True))
    o = jnp.concatenate(outs, axis=-1).astype(BF16)
    return h + _dot(o, wo_ref[...])


def _route(xn, whi_ref, wlo_ref, bias, carry):
    x_hi = xn.astype(BF16)
    x_lo = (xn - x_hi.astype(F32)).astype(BF16)
    logits = _dot(x_hi, whi_ref[...]) + _dot(x_lo, whi_ref[...]) + _dot(x_hi, wlo_ref[...]) + bias
    tm = logits.shape[0]
    lane = lax.broadcasted_iota(jnp.int32, logits.shape, 1).astype(F32)
    neg = -jnp.inf

    def first_argmax(vals):
        m = jnp.max(vals, axis=-1, keepdims=True)
        return m, jnp.min(jnp.where(vals == m, lane, float(LANES)), axis=-1, keepdims=True)

    gl = jnp.where((lane >= _GROUP_LANE0) & (lane < _GROUP_LANE0 + MOE_GROUPS), logits, neg)
    gmax, glane = first_argmax(gl)
    pg = 1.0 / jnp.sum(jnp.exp(gl - gmax), axis=-1, keepdims=True)
    lo = (glane - _GROUP_LANE0) * EXPERTS_PER_GROUP
    el = jnp.where((lane >= lo) & (lane < lo + EXPERTS_PER_GROUP), logits, neg)
    m1, e1 = first_argmax(el)
    m2, e2 = first_argmax(jnp.where(lane == e1, neg, el))
    p2 = jnp.exp(m2 - m1)
    gate1 = pg / (1.0 + p2)
    gate2 = pg * p2 / (1.0 + p2)

    hot = jnp.where((lane == e1) | (lane == e2), 1.0, 0.0)
    row = lax.broadcasted_iota(jnp.int32, (tm, tm), 0)
    col = lax.broadcasted_iota(jnp.int32, (tm, tm), 1)
    before = jnp.where(row > col, 1.0, 0.0).astype(BF16)
    cum = _dot(before, hot.astype(BF16)) + carry
    rank1 = jnp.sum(jnp.where(lane == e1, cum, 0.0), axis=-1, keepdims=True)
    rank2 = jnp.sum(jnp.where(lane == e2, cum, 0.0), axis=-1, keepdims=True)

    info = jnp.zeros(logits.shape, F32)
    for i, val in enumerate((e1, e2, rank1, rank2, gate1, gate2)):
        info = jnp.where(lane == i, val, info)
    return info, carry + jnp.sum(hot, axis=0, keepdims=True)


def _post_mixer_kernel(h_ref, r_ref, m_ref, ys_ref, u_ref, d_ref, wg_ref, bg_ref, wo_ref,
                       gc_ref, wq_ref, k_ref, v_ref, wco_ref, gf_ref, whi_ref, wlo_ref, br_ref,
                       o_ref, info_ref, infot_ref, cnt_ref, carry_ref):
    @pl.when(pl.program_id(0) == 0)
    def _():
        carry_ref[...] = jnp.zeros_like(carry_ref)

    h = _mixer_out_proj(h_ref[...], r_ref[...], m_ref[...], ys_ref[...], u_ref[...], d_ref[...], wg_ref,
                        bg_ref[...], wo_ref)
    h = _cross_attention(h, gc_ref[...], wq_ref, k_ref, v_ref, wco_ref)
    _store_token_tiles(o_ref, h)
    info, carry = _route(_rms(h, gf_ref[...]), whi_ref, wlo_ref, br_ref[...], carry_ref[...])
    carry_ref[...] = carry
    cnt_ref[...] = carry
    info_ref[...] = info
    infot_ref[...] = info.T[0:_INFO_ROWS, :]


def _post_mixer(h, out_r, out_m, y_s, u, d_s5, w_glu, b_glu, w_out, g_cross, wq, kv, layer, w_co,
                g_ffn, w_r, b_r, seq, mem_len):
    t, d = h.shape
    nt = d // LANES
    tm = ROW_TILE
    tiles_per_seq = seq // tm
    w_hi = w_r.astype(BF16)
    w_lo = (w_r - w_hi.astype(F32)).astype(BF16)
    row = lambda w: pl.BlockSpec((tm, w), lambda i: (i, 0))
    full = lambda a, b: pl.BlockSpec((a, b), lambda i: (0, 0))
    mem_blk = lambda col: pl.BlockSpec((mem_len, d), lambda i: (i // tiles_per_seq, col))
    return pl.pallas_call(
        _post_mixer_kernel,
        grid=(t // tm,),
        in_specs=[row(d), row(RET_W), row(SSD_W), row(S5_W), row(S5_W),
                  full(1, S5_W), full(S5_W, S5_W), full(1, S5_W), full(d, d),
                  full(1, d), full(d, d), mem_blk(2 * layer), mem_blk(2 * layer + 1), full(d, d),
                  full(1, d), full(d, LANES), full(d, LANES), full(1, LANES)],
        out_specs=[pl.BlockSpec((tm * nt, LANES), lambda i: (i, 0)),
                   pl.BlockSpec((tm, LANES), lambda i: (i, 0)),
                   pl.BlockSpec((_INFO_ROWS, tm), lambda i: (0, i)),
                   pl.BlockSpec((1, LANES), lambda i: (0, 0))],
        out_shape=[jax.ShapeDtypeStruct((t * nt, LANES), F32), jax.ShapeDtypeStruct((t, LANES), F32),
                   jax.ShapeDtypeStruct((_INFO_ROWS, t), F32), jax.ShapeDtypeStruct((1, LANES), F32)],
        scratch_shapes=[pltpu.VMEM((1, LANES), F32)],
        compiler_params=_cparams("arbitrary"),
        name="post_mixer",
    )(h, out_r, out_m, y_s, u, d_s5, w_glu, b_glu, w_out, g_cross, wq, kv, kv, w_co, g_ffn, w_hi, w_lo, b_r)


def _token_copy(src_hbm, dst_vmem, src_row, dst_token, nt, sem):
    return pltpu.make_async_copy(src_hbm.at[pl.ds(pl.multiple_of(src_row, nt), nt), :],
                                 dst_vmem.at[pl.ds(dst_token * nt, nt), :], sem)


_ISSUE_UNROLL = 8


def _start_token_gather(src_hbm, idx_ref, n_tokens, dst, sem):
    nt = dst.shape[0] // n_tokens

    def body(j, carry):
        for p in range(2):
            r = 2 * j + p
            _token_copy(src_hbm, dst, idx_ref[0, 0, r], r, nt, sem).start(priority=p)
        return carry

    lax.fori_loop(0, n_tokens // 2, body, 0, unroll=_ISSUE_UNROLL)


def _wait_token_gather(src_hbm, dst, sem):
    pltpu.make_async_copy(src_hbm.at[pl.ds(0, dst.shape[0]), :], dst, sem).wait()


def _expert_kernel(be_ref, nu_ref, cur_ref, nxt_ref, h_hbm, g_ref, wg_ref, wu_ref, wd_ref, y_ref,
                   xbuf, wg_s, wu_s, wd_s, sem):
    i = pl.program_id(0)
    n_used = nu_ref[0]
    slot = i % 2
    used = i < n_used

    @pl.when((i == 0) & used)
    def _():
        _start_token_gather(h_hbm, cur_ref, MOE_BLOCK, xbuf.at[0], sem.at[0])

    @pl.when(i + 1 < n_used)
    def _():
        _start_token_gather(h_hbm, nxt_ref, MOE_BLOCK, xbuf.at[1 - slot], sem.at[1 - slot])

    @pl.when(used & ((i == 0) | (be_ref[i] != be_ref[jnp.maximum(i - 1, 0)])))
    def _():
        wg_s[...] = wg_ref[0, 0].astype(BF16)
        wu_s[...] = wu_ref[0, 0].astype(BF16)
        wd_s[...] = wd_ref[0, 0].astype(BF16)

    @pl.when(used)
    def _():
        _wait_token_gather(h_hbm, xbuf.at[slot], sem.at[slot])
        xn = _rms(_load_token_tiles(xbuf.at[slot], MOE_BLOCK), g_ref[...]).astype(BF16)
        hid = (_silu(_dot(xn, wg_s[...])) * _dot(xn, wu_s[...])).astype(BF16)
        _store_token_tiles(y_ref, _dot(hid, wd_s[...]))

    @pl.when(jnp.logical_not(used))
    def _():
        y_ref[...] = jnp.zeros_like(y_ref)


def _experts(h_tiles, g, block_e, n_used, src_row, w_gate, w_up, w_down, layer):
    d = w_gate.shape[-2]
    nt = d // LANES
    nb = block_e.shape[0]
    de = w_gate.shape[-1]
    idx_blk = lambda f: pl.BlockSpec((1, 1, MOE_BLOCK), lambda i, be, nu: (f(i), 0, 0),
                                     memory_space=pltpu.SMEM)
    grid_spec = pltpu.PrefetchScalarGridSpec(
        num_scalar_prefetch=2,
        grid=(nb,),
        in_specs=[idx_blk(lambda i: i), idx_blk(lambda i: jnp.minimum(i + 1, nb - 1)),
                  pl.BlockSpec(memory_space=pl.ANY),
                  pl.BlockSpec((1, d), lambda i, be, nu: (0, 0)),
                  pl.BlockSpec((1, 1, d, de), lambda i, be, nu: (layer, be[i], 0, 0)),
                  pl.BlockSpec((1, 1, d, de), lambda i, be, nu: (layer, be[i], 0, 0)),
                  pl.BlockSpec((1, 1, de, d), lambda i, be, nu: (layer, be[i], 0, 0))],
        out_specs=pl.BlockSpec((MOE_BLOCK * nt, LANES), lambda i, be, nu: (i, 0)),
        scratch_shapes=[pltpu.VMEM((2, MOE_BLOCK * nt, LANES), F32), pltpu.VMEM((d, de), BF16),
                        pltpu.VMEM((d, de), BF16), pltpu.VMEM((de, d), BF16),
                        pltpu.SemaphoreType.DMA((2,))],
    )
    src3 = src_row.reshape(nb, 1, MOE_BLOCK)
    return pl.pallas_call(
        _expert_kernel,
        grid_spec=grid_spec,
        out_shape=jax.ShapeDtypeStruct((nb * MOE_BLOCK * nt, LANES), F32),
        compiler_params=_cparams("arbitrary"),
        name="moe_experts",
    )(block_e, n_used, src3, src3, h_tiles, g, w_gate, w_up, w_down)


def _combine_kernel(cur_ref, nxt_ref, h_ref, info_ref, y_hbm, fg_ref, o_ref, ybuf, sem, *, final_norm):
    i = pl.program_id(0)
    slot = i % 2
    tm = info_ref.shape[0]
    nt = h_ref.shape[0] // tm

    def start_gather(dest_ref, s):
        def body(r, carry):
            for k in range(2):
                _token_copy(y_hbm, ybuf.at[s, k], dest_ref[0, 0, k * tm + r], r, nt,
                            sem.at[s, k]).start(priority=k)
            return carry

        lax.fori_loop(0, tm, body, 0, unroll=_ISSUE_UNROLL)

    @pl.when(i == 0)
    def _():
        start_gather(cur_ref, 0)

    @pl.when(i + 1 < pl.num_programs(0))
    def _():
        start_gather(nxt_ref, 1 - slot)

    for k in range(2):
        _wait_token_gather(y_hbm, ybuf.at[slot, k], sem.at[slot, k])
    info = info_ref[...]
    out = _load_token_tiles(h_ref, tm) + (info[:, 4:5] * _load_token_tiles(ybuf.at[slot, 0], tm)
                                          + info[:, 5:6] * _load_token_tiles(ybuf.at[slot, 1], tm))
    if final_norm:
        out = _rms(out, fg_ref[...])
    o_ref[...] = out


def _combine(h_tiles, info, dest, ybuf, final_g, final_norm):
    t = info.shape[0]
    d = final_g.shape[-1]
    nt = d // LANES
    tm = MOE_ROW_TILE
    n_tiles = t // tm
    dest3 = dest.reshape(2, n_tiles, tm).transpose(1, 0, 2).reshape(n_tiles, 1, 2 * tm)
    idx_blk = lambda f: pl.BlockSpec((1, 1, 2 * tm), lambda i: (f(i), 0, 0), memory_space=pltpu.SMEM)
    return pl.pallas_call(
        functools.partial(_combine_kernel, final_norm=final_norm),
        grid=(n_tiles,),
        in_specs=[idx_blk(lambda i: i), idx_blk(lambda i: jnp.minimum(i + 1, n_tiles - 1)),
                  pl.BlockSpec((tm * nt, LANES), lambda i: (i, 0)),
                  pl.BlockSpec((tm, LANES), lambda i: (i, 0)),
                  pl.BlockSpec(memory_space=pl.ANY),
                  pl.BlockSpec((1, d), lambda i: (0, 0))],
        out_specs=pl.BlockSpec((tm, d), lambda i: (i, 0)),
        out_shape=jax.ShapeDtypeStruct((t, d), F32),
        scratch_shapes=[pltpu.VMEM((2, 2, tm * nt, LANES), F32), pltpu.SemaphoreType.DMA((2, 2))],
        compiler_params=_cparams("arbitrary"),
        name="moe_combine",
    )(dest3, dest3, h_tiles, info, ybuf, final_g)


def _router_params(w_rg, b_rg, w_re, b_re):
    d = w_rg.shape[0]
    pad_cols = LANES - N_EXPERTS - MOE_GROUPS
    w_r = jnp.concatenate([w_re, w_rg, jnp.zeros((d, pad_cols), F32)], axis=1)
    b_r = jnp.concatenate([b_re, b_rg, jnp.zeros((pad_cols,), F32)])[None, :]
    return w_r, b_r


def _moe(h_tiles, routing, g, w_gate, w_up, w_down, layer, final_g, final_norm):
    info, infot, cnt = routing
    d = g.shape[-1]
    nt = d // LANES
    t = h_tiles.shape[0] // nt
    expert = infot[0:2].astype(jnp.int32)
    rank = infot[2:4].astype(jnp.int32)
    counts = cnt[0, :N_EXPERTS].astype(jnp.int32)
    padded = (counts + MOE_BLOCK - 1) // MOE_BLOCK * MOE_BLOCK
    pends = jnp.cumsum(padded)
    pstarts = pends - padded
    ids = jnp.arange(N_EXPERTS, dtype=jnp.int32)
    dest = jnp.sum(jnp.where(expert[..., None] == ids, pstarts, 0), axis=-1) + rank
    nb = (2 * t) // MOE_BLOCK + N_EXPERTS
    blk = jnp.arange(nb, dtype=jnp.int32)
    block_e = jnp.minimum(jnp.sum(pends[None, :] <= blk[:, None] * MOE_BLOCK, axis=1), N_EXPERTS - 1)
    block_e = block_e.astype(jnp.int32)
    n_used = (pends[-1:] // MOE_BLOCK).astype(jnp.int32)
    real_keys = (expert * (2 * t) + jnp.arange(t, dtype=jnp.int32)[None, :]).reshape(-1)
    pad_ends = jnp.cumsum(padded - counts)
    j = jnp.arange(nb * MOE_BLOCK - 2 * t, dtype=jnp.int32)
    pad_expert = jnp.sum(pad_ends[None, :] <= j[:, None], axis=1).astype(jnp.int32)
    pad_keys = pad_expert * (2 * t) + t + j % t
    src_tok = jnp.sort(jnp.concatenate([real_keys, pad_keys])) % t
    ybuf = _experts(h_tiles, g, block_e, n_used, src_tok * nt, w_gate, w_up, w_down, layer)
    return _combine(h_tiles, info, dest * nt, ybuf, final_g, final_norm)


def kernel(x, mem, norm_mix_g, w_in, ret_gn_g, ssd_conv_w, ssd_conv_b, ssd_dt_bias, ssd_A_log, ssd_D,
           ssd_norm_g, s5_A_re, s5_A_im, s5_B_re, s5_B_im, s5_C_re, s5_C_im, s5_log_step, s5_D, s5_w_glu,
           s5_b_glu, w_out, norm_cross_g, mem_norm_g, w_cq, w_ck, w_cv, w_co, norm_ffn_g, w_route_group,
           b_route_group, w_route_expert, b_route_expert, w_gate, w_up, w_down, norm_final_g):
    batch, seq, d = x.shape
    depth = w_in.shape[0]
    mem_len = mem.shape[1]
    t = batch * seq
    assert d == RET_W * 4 and t % ROW_TILE == 0 and seq % ROW_TILE == 0
    assert seq % RET_CHUNK == 0 and seq % SSD_CHUNK == 0 and seq % (S5_CHUNK * S5_BLOCK_STEPS) == 0
    assert t % MOE_BLOCK == 0 and t % MOE_ROW_TILE == 0
    row = lambda v: v[None, :]
    h = x.reshape(t, d)

    w_kv = jnp.concatenate([w for i in range(depth) for w in (w_ck[i], w_cv[i])], axis=1).astype(BF16)
    kv = _norm_matmul(mem.reshape(batch * mem_len, d), row(mem_norm_g), w_kv, BF16,
                      tm=mem_len, tn=d)
    ret_tables = _retention_tables(seq)
    c0 = _QKVG_W + SSD_W + SSD_XBC_W
    c1 = c0 + SSD_HEADS

    w_pack = jnp.concatenate(
        [w_in[:, :, :c0], w_in[:, :, c1:], w_in[:, :, c0:c1], jnp.zeros((depth, d, LANES - SSD_HEADS), F32)],
        axis=2).astype(BF16)
    w_glu_b, w_out_b, w_cq_b, w_co_b = (w.astype(BF16) for w in (s5_w_glu, w_out, w_cq, w_co))
    s5_ops_all = jax.vmap(_s5_operators)(s5_A_re, s5_A_im, s5_B_re, s5_B_im, s5_C_re, s5_C_im, s5_log_step)

    for i in range(depth):
        qkvg, z, xbc, u, dt = _in_proj(h, row(norm_mix_g[i]), w_pack[i])
        out_r = _retention(qkvg, ret_tables, row(ret_gn_g[i]), batch, seq)
        out_m = _ssd(z, xbc, dt, ssd_conv_w[i], ssd_conv_b[i], ssd_dt_bias[i], ssd_A_log[i], ssd_D[i],
                     ssd_norm_g[i], batch, seq)
        y_s = _s5(u, [op[i] for op in s5_ops_all], batch, seq)
        w_r, b_r = _router_params(w_route_group[i], b_route_group[i], w_route_expert[i], b_route_expert[i])
        h_tiles, *routing = _post_mixer(
            h, out_r, out_m, y_s, u, row(s5_D[i]), w_glu_b[i], row(s5_b_glu[i]), w_out_b[i],
            row(norm_cross_g[i]), w_cq_b[i], kv, i, w_co_b[i], row(norm_ffn_g[i]), w_r, b_r, seq, mem_len)
        h = _moe(h_tiles, routing, row(norm_ffn_g[i]), w_gate, w_up, w_down, i, row(norm_final_g),
                 final_norm=(i == depth - 1))
    return h.reshape(batch, seq, d)
```
